```python
import jax
import jax.numpy as jnp
from jax import lax
import numpy as np

D_MODEL = 1024
BATCH = 8
SEQ = 8192
DEPTH = 2

HEAD_DIM = 64
A_HEADS = 8
A_KV_HEADS = 2
A_HALF_WINDOW = 128
B_HEADS_PER_GROUP = 4
B_CONFIGS = ((128, 1), (512, 4), (2048, 16))
C_HEADS = 8
GRID_W = 64
NA_KH = 8
NA_KW = 16
NUM_BUCKETS = 32
REL_MAX_DIST = 2048
N_GROUPS = 4
EXPERTS_PER_GROUP = 8
N_EXPERTS = N_GROUPS * EXPERTS_PER_GROUP
TOP_K = 2
D_EXPERT = 512
MOE_BLOCK = 256
BLK = 128
LN_EPS = 1e-5
NEG_INF = -1e30
DEEPNORM_ALPHA = (2 * DEPTH) ** 0.25
DEEPNORM_BETA = (8 * DEPTH) ** -0.25

A_Q = A_HEADS * HEAD_DIM
A_KV = A_KV_HEADS * HEAD_DIM
B_W = B_HEADS_PER_GROUP * HEAD_DIM
C_W = C_HEADS * HEAD_DIM
N_BRANCH = 3
IN_SIZES = (A_Q, A_KV, A_KV) + (B_W,) * (3 * len(B_CONFIGS)) + (C_W,) * 3 + (D_MODEL,) * N_BRANCH
W_IN_COLS = sum(IN_SIZES)
N_BIAS_HEADS = A_HEADS + B_HEADS_PER_GROUP * len(B_CONFIGS)

kernel_name = 'hybrid_gated_mixers_hmoe_encoder'


def layer_norm(x, g, b):
    xf = x.astype(jnp.float32)
    mu = xf.mean(-1, keepdims=True)
    var = jnp.square(xf - mu).mean(-1, keepdims=True)
    return ((xf - mu) * lax.rsqrt(var + LN_EPS) * g + b).astype(x.dtype)


def t5_bucket(rel):
    half = NUM_BUCKETS // 2
    max_exact = half // 2
    ret = np.where(rel > 0, half, 0)
    n = np.abs(rel)
    large = max_exact + (np.log(np.maximum(n, max_exact) / max_exact)
                         / np.log(REL_MAX_DIST / max_exact) * (half - max_exact)).astype(np.int32)
    large = np.minimum(large, half - 1)
    return (ret + np.where(n < max_exact, n, large)).astype(np.int32)


def banded_attention(q, k, v, half, dist_scale, bias_tab, sink, with_lse):
    N, L, Hk, G, Dh = q.shape
    nb = -(-L // BLK)
    Lp = nb * BLK
    KL = BLK + 2 * half
    qp = jnp.pad(q, ((0, 0), (0, Lp - L), (0, 0), (0, 0), (0, 0)))
    kv_pad = ((0, 0), (half, Lp - L + half), (0, 0), (0, 0))
    kp = jnp.pad(k, kv_pad)
    vp = jnp.pad(v, kv_pad)
    off = np.arange(KL)[None, :] - half - np.arange(BLK)[:, None]
    band = jnp.asarray(np.abs(off) <= half)
    bucket = jnp.asarray(t5_bucket(off * dist_scale))
    bias = bias_tab[bucket].astype(jnp.float32).reshape(BLK, KL, Hk, G).transpose(2, 3, 0, 1)
    scale = Dh ** -0.5

    def block(b):
        start = b * BLK
        qb = lax.dynamic_slice_in_dim(qp, start, BLK, axis=1)
        kb = lax.dynamic_slice_in_dim(kp, start, KL, axis=1)
        vb = lax.dynamic_slice_in_dim(vp, start, KL, axis=1)
        kpos = start - half + jnp.arange(KL)
        valid = band & ((kpos >= 0) & (kpos < L))[None, :]
        s = jnp.einsum('nqhgd,nkhd->nhgqk', qb, kb, preferred_element_type=jnp.float32) * scale + bias
        s = jnp.where(valid, s, NEG_INF)
        m = s.max(-1)
        if sink is not None:
            sk = sink.astype(jnp.float32)[None, :, :, None]
            m = jnp.maximum(m, sk)
        p = jnp.exp(s - m[..., None])
        denom = p.sum(-1)
        if sink is not None:
            denom = denom + jnp.exp(sk - m)
        o = jnp.einsum('nhgqk,nkhd->nqhgd', p, vb.astype(jnp.float32)) / denom.transpose(0, 3, 1, 2)[..., None]
        if with_lse:
            return o.astype(q.dtype), m + jnp.log(denom)
        return o.astype(q.dtype)

    res = lax.map(block, jnp.arange(nb))
    if with_lse:
        o, lse = res
        lse = lse.transpose(1, 0, 4, 2, 3).reshape(N, Lp, Hk, G)[:, :L]
    else:
        o = res
    o = jnp.moveaxis(o, 0, 1).reshape(N, Lp, Hk, G, Dh)[:, :L]
    if with_lse:
        return o, lse
    return o


def dilated_attention(q, k, v, window, dilation, bias_tab):
    Bn, L, H, Dh = q.shape
    Ls = L // dilation

    def to_sub(t):
        return t.reshape(Bn, Ls, dilation, H, Dh).transpose(0, 2, 1, 3, 4).reshape(Bn * dilation, Ls, H, Dh)

    o, lse = banded_attention(to_sub(q)[:, :, :, None], to_sub(k), to_sub(v),
                              window // (2 * dilation), dilation, bias_tab, None, True)
    o = o[:, :, :, 0].reshape(Bn, dilation, Ls, H, Dh).transpose(0, 2, 1, 3, 4).reshape(Bn, L, H, Dh)
    lse = lse[..., 0].reshape(Bn, dilation, Ls, H).transpose(0, 2, 1, 3).reshape(Bn, L, H)
    return o, lse


def neighborhood_attention(q, k, v, rpb):
    Bn, L, H, Dh = q.shape
    rows = L // GRID_W
    kh = min(NA_KH, rows)
    n_cb = GRID_W // NA_KW
    kcw = 2 * NA_KW
    qcol = np.arange(GRID_W).reshape(n_cb, NA_KW)
    cb_start = np.clip(np.arange(n_cb) * NA_KW - NA_KW // 2, 0, GRID_W - kcw)
    kcol = cb_start[:, None] + np.arange(kcw)
    qstart = np.clip(qcol - NA_KW // 2, 0, GRID_W - NA_KW)
    col_valid = (kcol[:, None, :] >= qstart[:, :, None]) & (kcol[:, None, :] < qstart[:, :, None] + NA_KW)
    cidx = np.clip(kcol[:, None, :] - qcol[:, :, None] + NA_KW - 1, 0, 2 * NA_KW - 2)
    mask = jnp.asarray(np.broadcast_to(col_valid[:, :, None, :], (n_cb, NA_KW, kh, kcw)).reshape(n_cb, NA_KW, kh * kcw))
    qg = q.reshape(Bn, rows, GRID_W, H, Dh)
    kg = k.reshape(Bn, rows, GRID_W, H, Dh)
    vg = v.reshape(Bn, rows, GRID_W, H, Dh)
    rpb32 = rpb.astype(jnp.float32)
    scale = Dh ** -0.5

    def row(i):
        rstart = jnp.clip(i - kh // 2, 0, rows - kh)
        qr = lax.dynamic_index_in_dim(qg, i, axis=1, keepdims=False).reshape(Bn, n_cb, NA_KW, H, Dh)
        kr = lax.dynamic_slice_in_dim(kg, rstart, kh, axis=1)[:, :, kcol]
        vr = lax.dynamic_slice_in_dim(vg, rstart, kh, axis=1)[:, :, kcol]
        kr = kr.transpose(0, 2, 1, 3, 4, 5).reshape(Bn, n_cb, kh * kcw, H, Dh)
        vr = vr.transpose(0, 2, 1, 3, 4, 5).reshape(Bn, n_cb, kh * kcw, H, Dh)
        ridx = rstart + jnp.arange(kh) - i + NA_KH - 1
        bias = rpb32[:, ridx[None, None, :, None], cidx[:, :, None, :]]
        bias = bias.reshape(H, n_cb, NA_KW, kh * kcw).transpose(1, 0, 2, 3)
        s = jnp.einsum('ncqhd,nckhd->nchqk', qr, kr, preferred_element_type=jnp.float32) * scale + bias
        s = jnp.where(mask[:, None], s, NEG_INF)
        p = jax.nn.softmax(s, axis=-1)
        o = jnp.einsum('nchqk,nckhd->ncqhd', p, vr.astype(jnp.float32))
        return o.reshape(Bn, GRID_W, H, Dh).astype(q.dtype)

    out = lax.map(row, jnp.arange(rows))
    return jnp.moveaxis(out, 0, 1).reshape(Bn, L, H, Dh)


def gated_mixer(h, rel_bias, w_in, b_gate, sink_a, rpb_c, w_br_a, w_br_b, w_br_c, w_out):
    Bn, L, _ = h.shape
    n_b = len(B_CONFIGS)
    proj = h @ w_in
    parts = jnp.split(proj, [int(c) for c in np.cumsum(IN_SIZES)[:-1]], axis=-1)
    qa = parts[0].reshape(Bn, L, A_KV_HEADS, A_HEADS // A_KV_HEADS, HEAD_DIM)
    ka = parts[1].reshape(Bn, L, A_KV_HEADS, HEAD_DIM)
    va = parts[2].reshape(Bn, L, A_KV_HEADS, HEAD_DIM)
    o_a = banded_attention(qa, ka, va, A_HALF_WINDOW, 1, rel_bias[:, :A_HEADS],
                           sink_a.reshape(A_KV_HEADS, A_HEADS // A_KV_HEADS), False).reshape(Bn, L, A_Q)
    outs, lses = [], []
    for g, (window, dilation) in enumerate(B_CONFIGS):
        qb, kb, vb = [t.reshape(Bn, L, B_HEADS_PER_GROUP, HEAD_DIM) for t in parts[3 + 3 * g: 6 + 3 * g]]
        c0 = A_HEADS + g * B_HEADS_PER_GROUP
        o, lse = dilated_attention(qb, kb, vb, window, dilation, rel_bias[:, c0:c0 + B_HEADS_PER_GROUP])
        outs.append(o)
        lses.append(lse)
    wts = jax.nn.softmax(jnp.stack(lses), axis=0)
    o_b = jnp.einsum('gblh,gblhd->blhd', wts, jnp.stack(outs).astype(jnp.float32)).astype(h.dtype).reshape(Bn, L, B_W)
    qc, kc, vc = [t.reshape(Bn, L, C_HEADS, HEAD_DIM) for t in parts[3 + 3 * n_b: 6 + 3 * n_b]]
    o_c = neighborhood_attention(qc, kc, vc, rpb_c).reshape(Bn, L, C_W)
    g_a, g_b, g_c = [jax.nn.sigmoid(parts[6 + 3 * n_b + i] + b_gate[i]) for i in range(N_BRANCH)]
    merged = g_a * (o_a @ w_br_a) + g_b * (o_b @ w_br_b) + g_c * (o_c @ w_br_c)
    return merged @ w_out


def hier_moe(h, w_rg, b_rg, w_re, b_re, w_eg, w_eu, w_ed):
    Bn, L, D = h.shape
    T = Bn * L
    xf = h.reshape(T, D)
    f32 = jnp.float32
    g_logits = jnp.dot(xf, w_rg, preferred_element_type=f32) + b_rg.astype(f32)
    g_prob = jax.nn.softmax(g_logits, axis=-1)
    g_idx = lax.top_k(g_logits, 1)[1][:, 0]
    g_w = jnp.take_along_axis(g_prob, g_idx[:, None], axis=-1)[:, 0]
    e_logits_all = jnp.einsum('td,gde->tge', xf, w_re, preferred_element_type=f32) + b_re.astype(f32)
    e_logits = jnp.take_along_axis(e_logits_all, g_idx[:, None, None], axis=1)[:, 0]
    e_top, e_idx = lax.top_k(e_logits, TOP_K)
    gate = g_w[:, None] * jax.nn.softmax(e_top, axis=-1)
    eid = g_idx[:, None] * EXPERTS_PER_GROUP + e_idx
    n_assign = T * TOP_K
    e_flat = eid.reshape(n_assign)
    w_flat = gate.reshape(n_assign)
    tok_flat = jnp.arange(n_assign) // TOP_K
    order = jnp.argsort(e_flat)
    e_sorted = e_flat[order]
    counts = jnp.bincount(e_flat, length=N_EXPERTS)
    offsets = jnp.cumsum(counts) - counts
    padded = (counts + MOE_BLOCK - 1) // MOE_BLOCK * MOE_BLOCK
    pad_end = jnp.cumsum(padded)
    pad_off = pad_end - padded
    dest = pad_off[e_sorted] + (jnp.arange(n_assign) - offsets[e_sorted])
    nblk = -(-n_assign // MOE_BLOCK) + N_EXPERTS
    P = nblk * MOE_BLOCK
    buf_tok = jnp.full((P,), T, jnp.int32).at[dest].set(tok_flat[order].astype(jnp.int32))
    buf_w = jnp.zeros((P,), f32).at[dest].set(w_flat[order])
    blk_e = jnp.minimum(jnp.searchsorted(pad_end, jnp.arange(nblk) * MOE_BLOCK, side='right'), N_EXPERTS - 1)
    x_pad = jnp.concatenate([xf, jnp.zeros((1, D), xf.dtype)], axis=0)

    def run(b):
        tok = lax.dynamic_slice_in_dim(buf_tok, b * MOE_BLOCK, MOE_BLOCK)
        wb = lax.dynamic_slice_in_dim(buf_w, b * MOE_BLOCK, MOE_BLOCK)
        e = blk_e[b]
        xb = x_pad[tok]
        hid = jax.nn.silu(xb @ w_eg[e]) * (xb @ w_eu[e])
        return ((hid @ w_ed[e]) * wb[:, None]).astype(h.dtype)

    yb = lax.map(run, jnp.arange(nblk))
    out = jnp.zeros((T + 1, D), h.dtype).at[buf_tok].add(yb.reshape(P, D))[:T]
    return out.reshape(Bn, L, D)


def setup_inputs(seed: int = 0) -> dict:
    key = jax.random.key(seed)
    ks = jax.random.split(key, 26)
    f32 = jnp.float32

    def nrm(k, shape, scale):
        return jax.random.normal(k, shape, f32) * scale

    return {
        'x': nrm(ks[0], (BATCH, SEQ, D_MODEL), 1.0),
        'ln0_g': 1.0 + nrm(ks[1], (D_MODEL,), 0.05),
        'ln0_b': nrm(ks[2], (D_MODEL,), 0.02),
        'rel_bias': nrm(ks[3], (NUM_BUCKETS, N_BIAS_HEADS), 0.1),
        'w_in': nrm(ks[4], (DEPTH, D_MODEL, W_IN_COLS), D_MODEL ** -0.5),
        'b_gate': nrm(ks[5], (DEPTH, N_BRANCH, D_MODEL), 0.02),
        'sink_a': nrm(ks[6], (DEPTH, A_HEADS), 0.5),
        'rpb_c': nrm(ks[7], (DEPTH, C_HEADS, 2 * NA_KH - 1, 2 * NA_KW - 1), 0.1),
        'w_br_a': nrm(ks[8], (DEPTH, A_Q, D_MODEL), A_Q ** -0.5),
        'w_br_b': nrm(ks[9], (DEPTH, B_W, D_MODEL), B_W ** -0.5),
        'w_br_c': nrm(ks[10], (DEPTH, C_W, D_MODEL), C_W ** -0.5),
        'w_out': nrm(ks[11], (DEPTH, D_MODEL, D_MODEL), D_MODEL ** -0.5 * DEEPNORM_BETA),
        'ln1_g': 1.0 + nrm(ks[12], (DEPTH, D_MODEL), 0.05),
        'ln1_b': nrm(ks[13], (DEPTH, D_MODEL), 0.02),
        'w_rg': nrm(ks[14], (DEPTH, D_MODEL, N_GROUPS), D_MODEL ** -0.5),
        'b_rg': nrm(ks[15], (DEPTH, N_GROUPS), 0.01),
        'w_re': nrm(ks[16], (DEPTH, N_GROUPS, D_MODEL, EXPERTS_PER_GROUP), D_MODEL ** -0.5),
        'b_re': nrm(ks[17], (DEPTH, N_GROUPS, EXPERTS_PER_GROUP), 0.01),
        'w_eg': nrm(ks[18], (DEPTH, N_EXPERTS, D_MODEL, D_EXPERT), D_MODEL ** -0.5),
        'w_eu': nrm(ks[19], (DEPTH, N_EXPERTS, D_MODEL, D_EXPERT), D_MODEL ** -0.5),
        'w_ed': nrm(ks[20], (DEPTH, N_EXPERTS, D_EXPERT, D_MODEL), D_EXPERT ** -0.5 * DEEPNORM_BETA),
        'ln2_g': 1.0 + nrm(ks[21], (DEPTH, D_MODEL), 0.05),
        'ln2_b': nrm(ks[22], (DEPTH, D_MODEL), 0.02),
    }


def reference(x, ln0_g, ln0_b, rel_bias, w_in, b_gate, sink_a, rpb_c, w_br_a, w_br_b, w_br_c, w_out,
              ln1_g, ln1_b, w_rg, b_rg, w_re, b_re, w_eg, w_eu, w_ed, ln2_g, ln2_b):
    h = layer_norm(x, ln0_g, ln0_b)
    for l in range(DEPTH):
        y = gated_mixer(h, rel_bias, w_in[l], b_gate[l], sink_a[l], rpb_c[l],
                        w_br_a[l], w_br_b[l], w_br_c[l], w_out[l])
        h = layer_norm(DEEPNORM_ALPHA * h + y, ln1_g[l], ln1_b[l])
        y = hier_moe(h, w_rg[l], b_rg[l], w_re[l], b_re[l], w_eg[l], w_eu[l], w_ed[l])
        h = layer_norm(DEEPNORM_ALPHA * h + y, ln2_g[l], ln2_b[l])
    return h
```

```python
import functools

import numpy as np
import jax
import jax.numpy as jnp
from jax import lax
from jax.experimental import pallas as pl
from jax.experimental.pallas import tpu as pltpu

F32 = jnp.float32
BF16 = jnp.bfloat16

HEAD_DIM = 64
A_HEADS = 8
A_KV_HEADS = 2
A_HALF_WINDOW = 128
B_HEADS_PER_GROUP = 4
B_CONFIGS = ((128, 1), (512, 4), (2048, 16))
C_HEADS = 8
GRID_W = 64
NA_KH = 8
NA_KW = 16
NUM_BUCKETS = 32
REL_MAX_DIST = 2048
N_GROUPS = 4
EXPERTS_PER_GROUP = 8
N_EXPERTS = N_GROUPS * EXPERTS_PER_GROUP
D_EXPERT = 512
LN_EPS = 1e-5
NEG_INF = -1e30

A_Q = A_HEADS * HEAD_DIM
A_KV = A_KV_HEADS * HEAD_DIM
B_W = B_HEADS_PER_GROUP * HEAD_DIM
C_W = C_HEADS * HEAD_DIM
N_BRANCH = 3
QKV_COLS = A_Q + 2 * A_KV + 3 * len(B_CONFIGS) * B_W + 3 * C_W
B_COL0 = A_Q + 2 * A_KV
C_COL0 = B_COL0 + 3 * len(B_CONFIGS) * B_W

ATT_BLK = 128
LANES = 128
MOE_ROWS = 256
VMEM_LIMIT = 56 * 1024 * 1024


def _cparams(sem):
    return pltpu.CompilerParams(dimension_semantics=sem, vmem_limit_bytes=VMEM_LIMIT)


def _ln_rows(x, g, b):
    mu = jnp.mean(x, axis=-1, keepdims=True)
    xc = x - mu
    var = jnp.mean(xc * xc, axis=-1, keepdims=True)
    return xc * lax.rsqrt(var + LN_EPS) * g + b


def _ln_kernel(x_ref, g_ref, b_ref, o_ref, ob_ref):
    y = _ln_rows(x_ref[...].astype(F32), g_ref[...], b_ref[...])
    o_ref[...] = y
    ob_ref[...] = y.astype(BF16)


def _layer_norm(x, g, b, tm=512):
    T, D = x.shape
    return pl.pallas_call(
        _ln_kernel,
        out_shape=(jax.ShapeDtypeStruct((T, D), F32), jax.ShapeDtypeStruct((T, D), BF16)),
        grid=(T // tm,),
        in_specs=[pl.BlockSpec((tm, D), lambda i: (i, 0)),
                  pl.BlockSpec((1, D), lambda i: (0, 0)),
                  pl.BlockSpec((1, D), lambda i: (0, 0))],
        out_specs=(pl.BlockSpec((tm, D), lambda i: (i, 0)), pl.BlockSpec((tm, D), lambda i: (i, 0))),
        compiler_params=_cparams(("parallel",)),
        name="ln0",
    )(x, g.reshape(1, D), b.reshape(1, D))


def _inproj_kernel(h_ref, w_ref, o_ref, *, n_chunks):
    h = h_ref[...]
    cw = o_ref.shape[1] // n_chunks
    for c in range(n_chunks):
        o_ref[:, c * cw:(c + 1) * cw] = jnp.dot(
            h, w_ref[:, c * cw:(c + 1) * cw], preferred_element_type=F32).astype(o_ref.dtype)


def _in_projection(hb, w, tm=512):
    T, D = hb.shape
    N = w.shape[1]
    return pl.pallas_call(
        functools.partial(_inproj_kernel, n_chunks=3),
        out_shape=jax.ShapeDtypeStruct((T, N), BF16),
        grid=(T // tm,),
        in_specs=[pl.BlockSpec((tm, D), lambda i: (i, 0)),
                  pl.BlockSpec((D, N), lambda i: (0, 0))],
        out_specs=pl.BlockSpec((tm, N), lambda i: (i, 0)),
        compiler_params=_cparams(("parallel",)),
        name="in_proj",
    )(hb, w)


def _t5_bucket(rel):
    half = NUM_BUCKETS // 2
    max_exact = half // 2
    ret = np.where(rel > 0, half, 0)
    n = np.abs(rel)
    large = max_exact + (np.log(np.maximum(n, max_exact) / max_exact)
                         / np.log(REL_MAX_DIST / max_exact) * (half - max_exact)).astype(np.int32)
    large = np.minimum(large, half - 1)
    return (ret + np.where(n < max_exact, n, large)).astype(np.int32)


def _band_bias(bias_tab, half, dist_scale):
    kl = ATT_BLK + 2 * half
    off = np.arange(kl)[None, :] - half - np.arange(ATT_BLK)[:, None]
    band = np.abs(off) <= half
    bias = jnp.transpose(bias_tab.astype(F32)[_t5_bucket(off * dist_scale)], (2, 0, 1))
    col = np.arange(kl)
    first_ok = (col >= half)[None, :]
    last_ok = (col < ATT_BLK + half)[None, :]
    out = []
    for first, last in ((0, 0), (1, 0), (0, 1), (1, 1)):
        ok = band & (first_ok if first else True) & (last_ok if last else True)
        out.append(jnp.where(jnp.asarray(ok)[None], bias, NEG_INF))
    return jnp.stack(out)


def _banded_kernel(*refs, n_sub, half, n_kv, group, has_sink, with_lse, seq_axis):
    q_ref, kp_ref, kc_ref, kn_ref, vp_ref, vc_ref, vn_ref, bias_ref = refs[:8]
    pos = 8
    sink_ref = None
    if has_sink:
        sink_ref = refs[pos]
        pos += 1
    o_ref = refs[pos]
    pos += 1
    lse_ref = None
    if with_lse:
        lse_ref = refs[pos]
        pos += 1
    k_s, v_s = refs[pos], refs[pos + 1]

    tq = n_sub * ATT_BLK
    kl = ATT_BLK + 2 * half
    j = pl.program_id(seq_axis)
    last_j = pl.num_programs(seq_axis) - 1

    k_s[0:half, :] = kp_ref[0, ATT_BLK - half:, :]
    k_s[half:half + tq, :] = kc_ref[0]
    k_s[half + tq:, :] = kn_ref[0, :half, :]
    v_s[0:half, :] = vp_ref[0, ATT_BLK - half:, :]
    v_s[half:half + tq, :] = vc_ref[0]
    v_s[half + tq:, :] = vn_ref[0, :half, :]

    def sub_block(i, carry):
        row0 = pl.multiple_of(i * ATT_BLK, ATT_BLK)
        is_first = jnp.logical_and(i == 0, j == 0).astype(jnp.int32)
        is_last = jnp.logical_and(i == n_sub - 1, j == last_j).astype(jnp.int32)
        var = is_first + 2 * is_last
        o_parts, lse_parts = [], []
        for hk in range(n_kv):
            kh = k_s[pl.ds(row0, kl), hk * HEAD_DIM:(hk + 1) * HEAD_DIM]
            vh = v_s[pl.ds(row0, kl), hk * HEAD_DIM:(hk + 1) * HEAD_DIM]
            for g in range(group):
                h = hk * group + g
                q = q_ref[0, pl.ds(row0, ATT_BLK), h * HEAD_DIM:(h + 1) * HEAD_DIM]
                s = lax.dot_general(q, kh, (((1,), (1,)), ((), ())), preferred_element_type=F32)
                s = s + bias_ref[var, h]
                m = jnp.max(s, axis=-1, keepdims=True)
                if has_sink:
                    sk = sink_ref[h]
                    m = jnp.maximum(m, sk)
                p = jnp.exp(s - m)
                denom = jnp.sum(p, axis=-1, keepdims=True)
                if has_sink:
                    denom = denom + jnp.exp(sk - m)
                o = jnp.dot(p.astype(BF16), vh, preferred_element_type=F32) / denom
                o_parts.append(o)
                if with_lse:
                    lse_parts.append(jnp.broadcast_to(m + jnp.log(denom), (ATT_BLK, HEAD_DIM)))
        o_ref[0, pl.ds(row0, ATT_BLK), :] = jnp.concatenate(o_parts, axis=-1).astype(o_ref.dtype)
        if with_lse:
            lse_ref[0, pl.ds(row0, ATT_BLK), :] = jnp.concatenate(lse_parts, axis=-1)
        return carry

    lax.fori_loop(0, n_sub, sub_block, 0)


def _banded_attention(qkv, bias, sink, *, dilation, q_col, k_col, v_col, n_heads, n_kv, half, with_lse):
    Bn, L, C = qkv.shape
    d = dilation
    Ls = L // d
    view = qkv.reshape(Bn, Ls, d * C)
    qw = n_heads * HEAD_DIM
    kw = n_kv * HEAD_DIM
    tq = min(512, Ls)
    n_sub = tq // ATT_BLK
    nbt = Ls // tq
    nb128 = Ls // ATT_BLK
    assert Ls % tq == 0 and q_col % qw == 0 and k_col % kw == 0 and v_col % kw == 0 and C % qw == 0 and C % kw == 0
    qc, kc, vc = q_col // qw, k_col // kw, v_col // kw
    qpr, kpr = C // qw, C // kw

    def cur(col, per_res):
        return lambda b, r, j: (b, j, r * per_res + col)

    def prev(col, per_res):
        return lambda b, r, j: (b, jnp.maximum(j * n_sub - 1, 0), r * per_res + col)

    def nxt(col, per_res):
        return lambda b, r, j: (b, jnp.minimum((j + 1) * n_sub, nb128 - 1), r * per_res + col)

    in_specs = [
        pl.BlockSpec((1, tq, qw), cur(qc, qpr)),
        pl.BlockSpec((1, ATT_BLK, kw), prev(kc, kpr)),
        pl.BlockSpec((1, tq, kw), cur(kc, kpr)),
        pl.BlockSpec((1, ATT_BLK, kw), nxt(kc, kpr)),
        pl.BlockSpec((1, ATT_BLK, kw), prev(vc, kpr)),
        pl.BlockSpec((1, tq, kw), cur(vc, kpr)),
        pl.BlockSpec((1, ATT_BLK, kw), nxt(vc, kpr)),
        pl.BlockSpec(bias.shape, lambda b, r, j: (0, 0, 0, 0)),
    ]
    args = [view, view, view, view, view, view, view, bias]
    if sink is not None:
        in_specs.append(pl.BlockSpec(memory_space=pltpu.SMEM))
        args.append(sink.astype(F32))
    out_block = pl.BlockSpec((1, tq, qw), lambda b, r, j: (b, j, r))
    out_shape = [jax.ShapeDtypeStruct((Bn, Ls, d * qw), BF16)]
    out_specs = [out_block]
    if with_lse:
        out_shape.append(jax.ShapeDtypeStruct((Bn, Ls, d * qw), F32))
        out_specs.append(out_block)
    res = pl.pallas_call(
        functools.partial(_banded_kernel, n_sub=n_sub, half=half, n_kv=n_kv, group=n_heads // n_kv,
                          has_sink=sink is not None, with_lse=with_lse, seq_axis=2),
        out_shape=tuple(out_shape),
        grid=(Bn, d, nbt),
        in_specs=in_specs,
        out_specs=tuple(out_specs),
        scratch_shapes=[pltpu.VMEM((tq + 2 * half, kw), BF16), pltpu.VMEM((tq + 2 * half, kw), BF16)],
        compiler_params=_cparams(("parallel", "parallel", "arbitrary")),
        name=f"banded_attn_d{d}_h{half}",
    )(*args)
    return tuple(r.reshape(Bn, L, qw) for r in res)


NBR_ROWS_PER_STEP = 8
NBR_KEY_BLOCK_ROWS = 4


def _nbr_bias(rpb):
    c = np.arange(GRID_W)
    qstart = np.clip(c - NA_KW // 2, 0, GRID_W - NA_KW)
    kc = np.arange(GRID_W)
    valid = (kc[None, :] >= qstart[:, None]) & (kc[None, :] < qstart[:, None] + NA_KW)
    cidx = np.clip(kc[None, :] - c[:, None] + NA_KW - 1, 0, 2 * NA_KW - 2)
    v = np.arange(NA_KH)
    kr = np.arange(NA_KH)
    ridx = kr[None, :] + NA_KH - 1 - v[:, None]
    tab = rpb.astype(F32)[:, ridx[:, :, None, None], cidx[None, None, :, :]]
    tab = jnp.where(jnp.asarray(valid)[None, None, None], tab, NEG_INF)
    tab = jnp.transpose(tab, (1, 0, 3, 2, 4))
    return tab.reshape(NA_KH, rpb.shape[0], GRID_W, NA_KH * GRID_W)


def _nbr_kernel(q_ref, k0, k1, k2, k3, v0, v1, v2, v3, bias_ref, o_ref, k_s, v_s, *, rows):
    blk = NBR_KEY_BLOCK_ROWS * GRID_W
    for t, (kr, vr) in enumerate(((k0, v0), (k1, v1), (k2, v2), (k3, v3))):
        k_s[t * blk:(t + 1) * blk, :] = kr[0]
        v_s[t * blk:(t + 1) * blk, :] = vr[0]
    i0 = pl.program_id(1) * NBR_ROWS_PER_STEP
    nkeys = NA_KH * GRID_W

    def one_row(r, carry):
        i = i0 + r
        rstart = jnp.clip(i - NA_KH // 2, 0, rows - NA_KH)
        off = pl.multiple_of((rstart - i0 + NBR_KEY_BLOCK_ROWS) * GRID_W, GRID_W)
        var = i - rstart
        q0 = pl.multiple_of(r * GRID_W, GRID_W)
        outs = []
        for h in range(C_HEADS):
            lanes = slice(h * HEAD_DIM, (h + 1) * HEAD_DIM)
            q = q_ref[0, pl.ds(q0, GRID_W), lanes]
            kh = k_s[pl.ds(off, nkeys), lanes]
            vh = v_s[pl.ds(off, nkeys), lanes]
            s = lax.dot_general(q, kh, (((1,), (1,)), ((), ())), preferred_element_type=F32)
            s = s + bias_ref[var, h]
            m = jnp.max(s, axis=-1, keepdims=True)
            p = jnp.exp(s - m)
            denom = jnp.sum(p, axis=-1, keepdims=True)
            outs.append(jnp.dot(p.astype(BF16), vh, preferred_element_type=F32) / denom)
        o_ref[0, pl.ds(q0, GRID_W), :] = jnp.concatenate(outs, axis=-1).astype(o_ref.dtype)
        return carry

    lax.fori_loop(0, NBR_ROWS_PER_STEP, one_row, 0)


def _neighborhood_attention(qkv, bias):
    Bn, L, C = qkv.shape
    rows = L // GRID_W
    assert rows % NBR_ROWS_PER_STEP == 0 and rows >= 2 * NBR_ROWS_PER_STEP and C_COL0 % C_W == 0
    qc = C_COL0 // C_W
    tq = NBR_ROWS_PER_STEP * GRID_W
    kb = NBR_KEY_BLOCK_ROWS * GRID_W
    nkb = L // kb
    per_step = NBR_ROWS_PER_STEP // NBR_KEY_BLOCK_ROWS

    def key_spec(col, t):
        return pl.BlockSpec((1, kb, C_W), lambda b, j: (b, jnp.clip(j * per_step - 1 + t, 0, nkb - 1), col))

    in_specs = ([pl.BlockSpec((1, tq, C_W), lambda b, j: (b, j, qc))]
                + [key_spec(qc + 1, t) for t in range(4)]
                + [key_spec(qc + 2, t) for t in range(4)]
                + [pl.BlockSpec(bias.shape, lambda b, j: (0, 0, 0, 0))])
    return pl.pallas_call(
        functools.partial(_nbr_kernel, rows=rows),
        out_shape=jax.ShapeDtypeStruct((Bn, L, C_W), BF16),
        grid=(Bn, rows // NBR_ROWS_PER_STEP),
        in_specs=in_specs,
        out_specs=pl.BlockSpec((1, tq, C_W), lambda b, j: (b, j, 0)),
        scratch_shapes=[pltpu.VMEM((4 * kb, C_W), BF16), pltpu.VMEM((4 * kb, C_W), BF16)],
        compiler_params=_cparams(("parallel", "arbitrary")),
        name="nbr_attn",
    )(*([qkv] * 9), bias)


def _post_kernel(h_ref, hb_ref, oa_ref, ob0_ref, ob1_ref, ob2_ref, l0_ref, l1_ref, l2_ref, oc_ref,
                 wg_ref, bg_ref, wa_ref, wb_ref, wc_ref, wo_ref, g_ref, b_ref, wr_ref, br_ref,
                 o_ref, obf_ref, logit_ref, *, alpha):
    D = h_ref.shape[1]
    hb = hb_ref[...]
    l0, l1, l2 = l0_ref[...], l1_ref[...], l2_ref[...]
    lm = jnp.maximum(jnp.maximum(l0, l1), l2)
    e0, e1, e2 = jnp.exp(l0 - lm), jnp.exp(l1 - lm), jnp.exp(l2 - lm)
    esum = e0 + e1 + e2
    o_b = (e0 / esum * ob0_ref[...].astype(F32) + e1 / esum * ob1_ref[...].astype(F32)
           + e2 / esum * ob2_ref[...].astype(F32)).astype(BF16)
    merged = None
    for idx, (o_br, w_ref) in enumerate(((oa_ref[...], wa_ref), (o_b, wb_ref), (oc_ref[...], wc_ref))):
        gate = jax.nn.sigmoid(
            jnp.dot(hb, wg_ref[:, idx * D:(idx + 1) * D], preferred_element_type=F32)
            + bg_ref[:, idx * D:(idx + 1) * D])
        term = gate * jnp.dot(o_br, w_ref[...], preferred_element_type=F32)
        merged = term if merged is None else merged + term
    y = jnp.dot(merged.astype(BF16), wo_ref[...], preferred_element_type=F32)
    h1 = _ln_rows(alpha * h_ref[...] + y, g_ref[...], b_ref[...])
    o_ref[...] = h1
    obf_ref[...] = h1.astype(BF16)
    logit_ref[...] = jnp.dot(h1, wr_ref[...], preferred_element_type=F32,
                             precision=lax.Precision.HIGHEST) + br_ref[...]


def _post_attention(h, hb, o_a, o_bs, lses, o_c, wg, bg, wa, wb, wc, wo, g, b, wr, br, alpha, tm=256):
    T, D = h.shape

    def rows(w):
        return pl.BlockSpec((tm, w), lambda i: (i, 0))

    def full(a):
        return pl.BlockSpec(a.shape, lambda i: (0, 0))

    weights = [wg, bg, wa, wb, wc, wo, g, b, wr, br]
    return pl.pallas_call(
        functools.partial(_post_kernel, alpha=alpha),
        out_shape=(jax.ShapeDtypeStruct((T, D), F32), jax.ShapeDtypeStruct((T, D), BF16),
                   jax.ShapeDtypeStruct((T, LANES), F32)),
        grid=(T // tm,),
        in_specs=[rows(D), rows(D), rows(A_Q), rows(B_W), rows(B_W), rows(B_W),
                  rows(B_W), rows(B_W), rows(B_W), rows(C_W)] + [full(w) for w in weights],
        out_specs=(rows(D), rows(D), rows(LANES)),
        compiler_params=_cparams(("parallel",)),
        name="post_attn",
    )(h, hb, o_a, *o_bs, *lses, o_c, *weights)


ROUTE_EID, ROUTE_GATE, ROUTE_RANK = 0, 2, 4


def _route_kernel(logit_ref, meta_ref, count_ref, run_ref):
    tm = logit_ref.shape[0]

    @pl.when(pl.program_id(0) == 0)
    def _():
        run_ref[...] = jnp.zeros_like(run_ref)

    x = logit_ref[...]
    lane = lax.broadcasted_iota(jnp.int32, x.shape, 1).astype(F32)
    big = float(LANES)

    def lane_max(v):
        return jnp.max(v, axis=-1, keepdims=True)

    def first_lane(mask):
        return jnp.min(jnp.where(mask, lane, big), axis=-1, keepdims=True)

    is_g = lane < N_GROUPS
    gl = jnp.where(is_g, x, NEG_INF)
    gmax = lane_max(gl)
    g_idx = first_lane(jnp.logical_and(is_g, gl == gmax))
    g_w = 1.0 / jnp.sum(jnp.where(is_g, jnp.exp(gl - gmax), 0.0), axis=-1, keepdims=True)

    e_lo = N_GROUPS + g_idx * EXPERTS_PER_GROUP
    in_grp = jnp.logical_and(lane >= e_lo, lane < e_lo + EXPERTS_PER_GROUP)
    el = jnp.where(in_grp, x, NEG_INF)
    top1 = lane_max(el)
    lane1 = first_lane(jnp.logical_and(in_grp, el == top1))
    el2 = jnp.where(lane == lane1, NEG_INF, el)
    top2 = lane_max(el2)
    lane2 = first_lane(jnp.logical_and(in_grp, jnp.logical_and(lane != lane1, el2 == top2)))
    e2 = jnp.exp(top2 - top1)
    w1 = 1.0 / (1.0 + e2)
    w2 = e2 / (1.0 + e2)

    eid1 = lane1 - N_GROUPS
    eid2 = lane2 - N_GROUPS
    onehot = jnp.logical_or(lane == eid1, lane == eid2)
    oh = jnp.where(onehot, 1.0, 0.0).astype(BF16)
    r_i = lax.broadcasted_iota(jnp.int32, (tm, tm), 0)
    c_i = lax.broadcasted_iota(jnp.int32, (tm, tm), 1)
    strict_lower = jnp.where(c_i < r_i, 1.0, 0.0).astype(BF16)
    before = jnp.dot(strict_lower, oh, preferred_element_type=F32) + run_ref[...]
    rank1 = jnp.sum(jnp.where(lane == eid1, before, 0.0), axis=-1, keepdims=True)
    rank2 = jnp.sum(jnp.where(lane == eid2, before, 0.0), axis=-1, keepdims=True)
    run_ref[...] = run_ref[...] + jnp.sum(oh.astype(F32), axis=0, keepdims=True)

    meta = jnp.zeros(x.shape, F32)
    for k, val in ((ROUTE_EID, eid1.astype(F32)), (ROUTE_EID + 1, eid2.astype(F32)),
                   (ROUTE_GATE, g_w * w1), (ROUTE_GATE + 1, g_w * w2),
                   (ROUTE_RANK, rank1), (ROUTE_RANK + 1, rank2)):
        meta = jnp.where(lane == k, val, meta)
    meta_ref[...] = meta
    count_ref[...] = run_ref[...]


def _route(logits, tm=512):
    T = logits.shape[0]
    return pl.pallas_call(
        _route_kernel,
        out_shape=(jax.ShapeDtypeStruct((T, LANES), F32), jax.ShapeDtypeStruct((1, LANES), F32)),
        grid=(T // tm,),
        in_specs=[pl.BlockSpec((tm, LANES), lambda i: (i, 0))],
        out_specs=(pl.BlockSpec((tm, LANES), lambda i: (i, 0)), pl.BlockSpec((1, LANES), lambda i: (0, 0))),
        scratch_shapes=[pltpu.VMEM((1, LANES), F32)],
        compiler_params=_cparams(("arbitrary",)),
        name="route",
    )(logits)


def _expert_kernel(blk_e_ref, n_used_ref, x_ref, wg_ref, wu_ref, wd_ref, o_ref):
    b = pl.program_id(0)

    @pl.when(b < n_used_ref[0])
    def _():
        x = x_ref[...]
        a = jnp.dot(x, wg_ref[0], preferred_element_type=F32)
        u = jnp.dot(x, wu_ref[0], preferred_element_type=F32)
        hid = (a * jax.nn.sigmoid(a) * u).astype(BF16)
        o_ref[...] = jnp.dot(hid, wd_ref[0], preferred_element_type=F32)

    @pl.when(b >= n_used_ref[0])
    def _():
        o_ref[...] = jnp.zeros_like(o_ref)


def _experts(blk_e, n_used, xs, wg, wu, wd):
    P, D = xs.shape
    nblk = P // MOE_ROWS
    de = wg.shape[2]
    grid_spec = pltpu.PrefetchScalarGridSpec(
        num_scalar_prefetch=2,
        grid=(nblk,),
        in_specs=[pl.BlockSpec((MOE_ROWS, D), lambda b, be, nu: (b, 0)),
                  pl.BlockSpec((1, D, de), lambda b, be, nu: (be[b], 0, 0)),
                  pl.BlockSpec((1, D, de), lambda b, be, nu: (be[b], 0, 0)),
                  pl.BlockSpec((1, de, D), lambda b, be, nu: (be[b], 0, 0))],
        out_specs=pl.BlockSpec((MOE_ROWS, D), lambda b, be, nu: (b, 0)),
    )
    return pl.pallas_call(
        _expert_kernel,
        out_shape=jax.ShapeDtypeStruct((P, D), F32),
        grid_spec=grid_spec,
        compiler_params=_cparams(("arbitrary",)),
        name="experts",
    )(blk_e, n_used, xs, wg, wu, wd)


def _combine_kernel(h_ref, ya_ref, yb_ref, meta_ref, g_ref, b_ref, o_ref, ob_ref, *, alpha):
    meta = meta_ref[...]
    g1 = meta[:, ROUTE_GATE:ROUTE_GATE + 1]
    g2 = meta[:, ROUTE_GATE + 1:ROUTE_GATE + 2]
    y = ya_ref[...] * g1 + yb_ref[...] * g2
    h2 = _ln_rows(alpha * h_ref[...] + y, g_ref[...], b_ref[...])
    o_ref[...] = h2
    ob_ref[...] = h2.astype(BF16)


def _combine(h, ya, yb, meta, g, b, alpha, tm=512):
    T, D = h.shape
    rows = pl.BlockSpec((tm, D), lambda i: (i, 0))
    vec = pl.BlockSpec((1, D), lambda i: (0, 0))
    return pl.pallas_call(
        functools.partial(_combine_kernel, alpha=alpha),
        out_shape=(jax.ShapeDtypeStruct((T, D), F32), jax.ShapeDtypeStruct((T, D), BF16)),
        grid=(T // tm,),
        in_specs=[rows, rows, rows, pl.BlockSpec((tm, LANES), lambda i: (i, 0)), vec, vec],
        out_specs=(rows, rows),
        compiler_params=_cparams(("parallel",)),
        name="combine_ln2",
    )(h, ya, yb, meta, g.reshape(1, D), b.reshape(1, D))


def _moe(h1, h1b, logits, w_eg, w_eu, w_ed, g, b, alpha):
    T, D = h1.shape
    meta, counts = _route(logits)
    eid = meta[:, ROUTE_EID:ROUTE_EID + 2].astype(jnp.int32)
    rank = meta[:, ROUTE_RANK:ROUTE_RANK + 2].astype(jnp.int32)
    cnt = counts[0, :N_EXPERTS].astype(jnp.int32)
    padded = (cnt + MOE_ROWS - 1) // MOE_ROWS * MOE_ROWS
    pad_end = jnp.cumsum(padded)
    pad_off = pad_end - padded
    dest = pad_off[eid] + rank
    nblk = -(-(2 * T) // MOE_ROWS) + N_EXPERTS
    P = nblk * MOE_ROWS
    tok = jnp.broadcast_to(jnp.arange(T, dtype=jnp.int32)[:, None], (T, 2))
    buf_tok = jnp.zeros((P,), jnp.int32).at[dest.reshape(-1)].set(tok.reshape(-1), unique_indices=True)
    blk_e = jnp.minimum(jnp.searchsorted(pad_end, jnp.arange(nblk, dtype=jnp.int32) * MOE_ROWS, side='right'),
                        N_EXPERTS - 1).astype(jnp.int32)
    n_used = (pad_end[-1:] // MOE_ROWS).astype(jnp.int32)
    xs = jnp.take(h1b, buf_tok, axis=0)
    y = _experts(blk_e, n_used, xs, w_eg, w_eu, w_ed)
    ya = jnp.take(y, dest[:, 0], axis=0)
    yb = jnp.take(y, dest[:, 1], axis=0)
    return _combine(h1, ya, yb, meta, g, b, alpha)


def kernel(x, ln0_g, ln0_b, rel_bias, w_in, b_gate, sink_a, rpb_c, w_br_a, w_br_b, w_br_c, w_out,
           ln1_g, ln1_b, w_rg, b_rg, w_re, b_re, w_eg, w_eu, w_ed, ln2_g, ln2_b):
    Bn, L, D = x.shape
    depth = w_in.shape[0]
    T = Bn * L
    alpha = float((2 * depth) ** 0.25)
    n_b = len(B_CONFIGS)

    col_scale = np.ones((QKV_COLS,), np.float32)
    col_scale[:A_Q] = HEAD_DIM ** -0.5
    for g in range(n_b):
        c0 = B_COL0 + 3 * g * B_W
        col_scale[c0:c0 + B_W] = HEAD_DIM ** -0.5
    col_scale[C_COL0:C_COL0 + C_W] = HEAD_DIM ** -0.5

    bias_a = _band_bias(rel_bias[:, :A_HEADS], A_HALF_WINDOW, 1)
    bias_b = []
    for g, (window, dilation) in enumerate(B_CONFIGS):
        c0 = A_HEADS + g * B_HEADS_PER_GROUP
        bias_b.append(_band_bias(rel_bias[:, c0:c0 + B_HEADS_PER_GROUP], window // (2 * dilation), dilation))

    h, hb = _layer_norm(x.reshape(T, D), ln0_g, ln0_b)
    for l in range(depth):
        w_qkv = (w_in[l, :, :QKV_COLS] * col_scale).astype(BF16)
        w_gate = w_in[l, :, QKV_COLS:].astype(BF16)
        qkv = _in_projection(hb, w_qkv).reshape(Bn, L, QKV_COLS)

        (o_a,) = _banded_attention(qkv, bias_a, sink_a[l], dilation=1, q_col=0, k_col=A_Q, v_col=A_Q + A_KV,
                                   n_heads=A_HEADS, n_kv=A_KV_HEADS, half=A_HALF_WINDOW, with_lse=False)
        o_bs, lses = [], []
        for g, (window, dilation) in enumerate(B_CONFIGS):
            c0 = B_COL0 + 3 * g * B_W
            o, lse = _banded_attention(qkv, bias_b[g], None, dilation=dilation, q_col=c0, k_col=c0 + B_W,
                                       v_col=c0 + 2 * B_W, n_heads=B_HEADS_PER_GROUP, n_kv=B_HEADS_PER_GROUP,
                                       half=window // (2 * dilation), with_lse=True)
            o_bs.append(o.reshape(T, B_W))
            lses.append(lse.reshape(T, B_W))
        o_c = _neighborhood_attention(qkv, _nbr_bias(rpb_c[l]))

        w_r = jnp.concatenate([w_rg[l], jnp.transpose(w_re[l], (1, 0, 2)).reshape(D, N_EXPERTS)], axis=1)
        w_r = jnp.pad(w_r.astype(F32), ((0, 0), (0, LANES - w_r.shape[1])))
        b_r = jnp.concatenate([b_rg[l], b_re[l].reshape(-1)]).astype(F32)
        b_r = jnp.pad(b_r, (0, LANES - b_r.shape[0])).reshape(1, LANES)

        h1, h1b, logits = _post_attention(
            h, hb, o_a.reshape(T, A_Q), o_bs, lses, o_c.reshape(T, C_W),
            w_gate, b_gate[l].reshape(1, N_BRANCH * D).astype(F32),
            w_br_a[l].astype(BF16), w_br_b[l].astype(BF16), w_br_c[l].astype(BF16), w_out[l].astype(BF16),
            ln1_g[l].reshape(1, D), ln1_b[l].reshape(1, D), w_r, b_r, alpha)

        h, hb = _moe(h1, h1b, logits, w_eg[l].astype(BF16), w_eu[l].astype(BF16), w_ed[l].astype(BF16),
                     ln2_g[l], ln2_b[l], alpha)
    return h.reshape(Bn, L, D)
```

```python
import functools

import numpy as np
import jax
import jax.numpy as jnp
from jax import lax
from jax.experimental import pallas as pl
from jax.experimental.pallas import tpu as pltpu

F32 = jnp.float32
BF16 = jnp.bfloat16
HIGHEST = lax.Precision.HIGHEST

HEAD_DIM = 64
A_HEADS = 8
A_KV_HEADS = 2
A_HALF_WINDOW = 128
B_HEADS_PER_GROUP = 4
B_CONFIGS = ((128, 1), (512, 4), (2048, 16))
C_HEADS = 8
GRID_W = 64
NA_KH = 8
NA_KW = 16
NUM_BUCKETS = 32
REL_MAX_DIST = 2048
N_GROUPS = 4
EXPERTS_PER_GROUP = 8
N_EXPERTS = N_GROUPS * EXPERTS_PER_GROUP
D_EXPERT = 512
LN_EPS = 1e-5
NEG_INF = -1e30

A_Q = A_HEADS * HEAD_DIM
A_KV = A_KV_HEADS * HEAD_DIM
B_W = B_HEADS_PER_GROUP * HEAD_DIM
C_W = C_HEADS * HEAD_DIM
N_BRANCH = 3
QKV_COLS = A_Q + 2 * A_KV + 3 * len(B_CONFIGS) * B_W + 3 * C_W
B_COL0 = A_Q + 2 * A_KV
C_COL0 = B_COL0 + 3 * len(B_CONFIGS) * B_W

ATT_BLK = 128
LANES = 128
MOE_ROWS = 256
VMEM_LIMIT = 56 * 1024 * 1024

A_HEAD_ORDER = tuple(h for c in range(A_HEADS // 2) for h in (c, c + A_HEADS // 2))


def _cparams(sem):
    return pltpu.CompilerParams(dimension_semantics=sem, vmem_limit_bytes=VMEM_LIMIT)


def _ln_rows(x, g, b):
    mu = jnp.mean(x, axis=-1, keepdims=True)
    xc = x - mu
    var = jnp.mean(xc * xc, axis=-1, keepdims=True)
    return xc * lax.rsqrt(var + LN_EPS) * g + b


def _ln_kernel(x_ref, g_ref, b_ref, o_ref, ob_ref):
    y = _ln_rows(x_ref[...].astype(F32), g_ref[...], b_ref[...])
    o_ref[...] = y
    ob_ref[...] = y.astype(BF16)


def _layer_norm(x, g, b, tm=512):
    T, D = x.shape
    return pl.pallas_call(
        _ln_kernel,
        out_shape=(jax.ShapeDtypeStruct((T, D), F32), jax.ShapeDtypeStruct((T, D), BF16)),
        grid=(T // tm,),
        in_specs=[pl.BlockSpec((tm, D), lambda i: (i, 0)),
                  pl.BlockSpec((1, D), lambda i: (0, 0)),
                  pl.BlockSpec((1, D), lambda i: (0, 0))],
        out_specs=(pl.BlockSpec((tm, D), lambda i: (i, 0)), pl.BlockSpec((tm, D), lambda i: (i, 0))),
        compiler_params=_cparams(("parallel",)),
        name="ln0",
    )(x, g.reshape(1, D), b.reshape(1, D))


BAND_W = 3 * B_W
NBR_W = 3 * C_W


def _inproj_weight(w_in_l):
    scale = HEAD_DIM ** -0.5
    a_q = w_in_l[:, :A_Q].reshape(-1, A_HEADS, HEAD_DIM)[:, np.asarray(A_HEAD_ORDER)].reshape(-1, A_Q)
    cols = [w_in_l[:, A_Q:A_Q + 2 * A_KV], a_q * scale]
    for g in range(len(B_CONFIGS)):
        c0 = B_COL0 + 3 * g * B_W
        cols += [w_in_l[:, c0 + B_W:c0 + 3 * B_W], w_in_l[:, c0:c0 + B_W] * scale]
    cols += [w_in_l[:, C_COL0 + C_W:C_COL0 + 3 * C_W], w_in_l[:, C_COL0:C_COL0 + C_W] * scale]
    return jnp.concatenate(cols, axis=1).astype(BF16)


def _inproj_kernel(h_ref, w_ref, a_ref, b0_ref, b1_ref, b2_ref, c_ref, tmp_ref):
    h = h_ref[...]
    tm = h.shape[0]

    def chunk(i):
        return jnp.dot(h, w_ref[:, i * BAND_W:(i + 1) * BAND_W], preferred_element_type=F32)

    a_ref[0, 0] = chunk(0).astype(BF16)
    b0_ref[0, 0] = chunk(1).astype(BF16)
    for i, ref in ((2, b1_ref), (3, b2_ref)):
        d = ref.shape[1]
        res = chunk(i)
        n_cols = BAND_W // LANES
        for c in range(n_cols):
            tmp_ref[c] = res[:, c * LANES:(c + 1) * LANES]
        for r in range(d):
            for c in range(n_cols):
                ref[0, r, :, c * LANES:(c + 1) * LANES] = tmp_ref[c, pl.ds(r, tm // d, stride=d), :].astype(BF16)
    c_ref[:, :BAND_W] = chunk(4).astype(BF16)
    c_ref[:, BAND_W:] = chunk(5).astype(BF16)


def _in_projection(hb, w, Bn, L, tm=512):
    T, D = hb.shape
    nt = L // tm
    assert L % tm == 0 and NBR_W == 2 * BAND_W and A_Q + 2 * A_KV == BAND_W
    dils = [d for _, d in B_CONFIGS]
    assert dils[0] == 1 and len(dils) == 3

    def band_spec(d):
        return pl.BlockSpec((1, d, tm // d, BAND_W), lambda b, i: (b, 0, i, 0))

    return pl.pallas_call(
        _inproj_kernel,
        out_shape=(jax.ShapeDtypeStruct((Bn, 1, L, BAND_W), BF16),
                   jax.ShapeDtypeStruct((Bn, 1, L, BAND_W), BF16),
                   jax.ShapeDtypeStruct((Bn, dils[1], L // dils[1], BAND_W), BF16),
                   jax.ShapeDtypeStruct((Bn, dils[2], L // dils[2], BAND_W), BF16),
                   jax.ShapeDtypeStruct((T, NBR_W), BF16)),
        grid=(Bn, nt),
        in_specs=[pl.BlockSpec((tm, D), lambda b, i: (b * nt + i, 0)),
                  pl.BlockSpec(w.shape, lambda b, i: (0, 0))],
        out_specs=(band_spec(1), band_spec(1), band_spec(dils[1]), band_spec(dils[2]),
                   pl.BlockSpec((tm, NBR_W), lambda b, i: (b * nt + i, 0))),
        scratch_shapes=[pltpu.VMEM((BAND_W // LANES, tm, LANES), F32)],
        compiler_params=_cparams(("parallel", "parallel")),
        name="in_proj",
    )(hb, w)


def _t5_bucket(rel):
    half = NUM_BUCKETS // 2
    max_exact = half // 2
    ret = np.where(rel > 0, half, 0)
    n = np.abs(rel)
    large = max_exact + (np.log(np.maximum(n, max_exact) / max_exact)
                         / np.log(REL_MAX_DIST / max_exact) * (half - max_exact)).astype(np.int32)
    large = np.minimum(large, half - 1)
    return (ret + np.where(n < max_exact, n, large)).astype(np.int32)


def _one_hot(idx, n):
    return (jnp.asarray(idx, jnp.int32)[..., None] == jnp.arange(n, dtype=jnp.int32)).astype(F32)


def _band_bias(bias_tab, head_pairs, half, dist_scale):
    kl = ATT_BLK + 2 * half
    off = np.arange(kl)[None, :] - half - np.arange(ATT_BLK)[:, None]
    band = np.abs(off) <= half
    onehot = _one_hot(_t5_bucket(off * dist_scale), NUM_BUCKETS)
    order = np.asarray([h for pair in head_pairs for h in pair])
    bias = jnp.einsum('qkb,bh->hqk', onehot, bias_tab.astype(F32)[:, order], precision=HIGHEST)
    bias = bias.reshape(len(head_pairs), 2 * ATT_BLK, kl)
    col = np.arange(kl)
    first_ok = (col >= half)[None, :]
    last_ok = (col < ATT_BLK + half)[None, :]
    out = []
    for first, last in ((0, 0), (1, 0), (0, 1), (1, 1)):
        ok = band & (first_ok if first else True) & (last_ok if last else True)
        ok = np.concatenate([ok, ok], axis=0)
        out.append(jnp.where(jnp.asarray(ok)[None], bias, NEG_INF))
    return jnp.stack(out)


def _pair_rows(x):
    lo = lax.broadcasted_iota(jnp.int32, x.shape, 1) < HEAD_DIM
    zero = jnp.zeros_like(x)
    return jnp.concatenate([jnp.where(lo, x, zero), jnp.where(lo, zero, x)], axis=0)


def _unpair_rows(x2):
    rows = x2.shape[0] // 2
    lo = lax.broadcasted_iota(jnp.int32, (rows, LANES), 1) < HEAD_DIM
    return jnp.where(lo, x2[:rows], x2[rows:])


def _banded_kernel(*refs, n_sub, half, kv_cols, n_pairs, shared_kv, has_sink, with_lse, dilation):
    cur_ref, prev_ref, next_ref, bias_ref = refs[:4]
    pos = 4
    sink_ref = None
    if has_sink:
        sink_ref = refs[pos]
        pos += 1
    o_ref = refs[pos]
    pos += 1
    lse_ref = None
    if with_lse:
        lse_ref = refs[pos]
        pos += 1
    kv_s = refs[pos]

    tq = n_sub * ATT_BLK
    kl = ATT_BLK + 2 * half
    kvw = kv_cols * LANES
    d = dilation
    j = pl.program_id(1)
    last_j = pl.num_programs(1) - 1
    row_is_lo = lax.broadcasted_iota(jnp.int32, (2 * ATT_BLK, 1), 0) < ATT_BLK

    def one_residue(r, carry):
        kv_s[0:half, :] = prev_ref[0, r]
        kv_s[half:half + tq, :] = cur_ref[0, r, :, :2 * kvw]
        kv_s[half + tq:, :] = next_ref[0, r]

        def sub_block(i, carry2):
            row0 = pl.multiple_of(i * ATT_BLK, ATT_BLK)
            is_first = jnp.logical_and(i == 0, j == 0).astype(jnp.int32)
            is_last = jnp.logical_and(i == n_sub - 1, j == last_j).astype(jnp.int32)
            var = is_first + 2 * is_last

            def scores(c):
                kc = 0 if shared_kv else c
                q2 = _pair_rows(cur_ref[0, r, pl.ds(row0, ATT_BLK), 2 * kvw + c * LANES:2 * kvw + (c + 1) * LANES])
                k = kv_s[pl.ds(row0, kl), kc * LANES:(kc + 1) * LANES]
                return lax.dot_general(q2, k, (((1,), (1,)), ((), ())), preferred_element_type=F32)

            s_next = scores(0)
            o_cols, lse_cols = [], []
            for c in range(n_pairs):
                s = s_next + bias_ref[var, c]
                if c + 1 < n_pairs:
                    s_next = scores(c + 1)
                m = jnp.max(s, axis=-1, keepdims=True)
                if has_sink:
                    sk = jnp.where(row_is_lo, sink_ref[2 * c], sink_ref[2 * c + 1])
                    m = jnp.maximum(m, sk)
                p = jnp.exp(s - m)
                denom = jnp.sum(p, axis=-1, keepdims=True)
                if has_sink:
                    denom = denom + jnp.exp(sk - m)
                kc = 0 if shared_kv else c
                v = kv_s[pl.ds(row0, kl), kvw + kc * LANES:kvw + (kc + 1) * LANES]
                o2 = jnp.dot(p.astype(BF16), v, preferred_element_type=F32) / denom
                o_cols.append(_unpair_rows(o2))
                if with_lse:
                    lse_cols.append(_unpair_rows(jnp.broadcast_to(m + jnp.log(denom), (2 * ATT_BLK, LANES))))
            if d == 1:
                rows = pl.ds(row0, ATT_BLK)
            else:
                rows = pl.ds(row0 * d + r, ATT_BLK, stride=d)
            if with_lse:
                for c in range(n_pairs):
                    o_ref[0, c, rows, :] = o_cols[c].astype(o_ref.dtype)
                    lse_ref[0, c, rows, :] = lse_cols[c]
            else:
                o_ref[0, rows, :] = jnp.concatenate(o_cols, axis=-1).astype(o_ref.dtype)
            return carry2

        lax.fori_loop(0, n_sub, sub_block, 0)
        return carry

    lax.fori_loop(0, d, one_residue, 0)


def _banded_attention(arr, bias, sink, *, tq, n_pairs, shared_kv, half, with_lse):
    Bn, d, Ls, W = arr.shape
    L = d * Ls
    qw = n_pairs * LANES
    kv_cols = (W - qw) // (2 * LANES)
    kvw = kv_cols * LANES
    tq = min(tq, Ls)
    n_sub = tq // ATT_BLK
    nbt = Ls // tq
    halo_per_tile = tq // half
    n_halo = Ls // half
    assert Ls % tq == 0 and tq % half == 0 and kvw * 2 + qw == W
    o_dtype = BF16 if d == 1 else F32

    in_specs = [
        pl.BlockSpec((1, d, tq, W), lambda b, j: (b, 0, j, 0)),
        pl.BlockSpec((1, d, half, 2 * kvw), lambda b, j: (b, 0, jnp.maximum(j * halo_per_tile - 1, 0), 0)),
        pl.BlockSpec((1, d, half, 2 * kvw),
                     lambda b, j: (b, 0, jnp.minimum((j + 1) * halo_per_tile, n_halo - 1), 0)),
        pl.BlockSpec(bias.shape, lambda b, j: (0, 0, 0, 0)),
    ]
    args = [arr, arr, arr, bias]
    if sink is not None:
        in_specs.append(pl.BlockSpec(memory_space=pltpu.SMEM))
        args.append(sink.astype(F32))
    if with_lse:
        out_block = pl.BlockSpec((1, n_pairs, tq * d, LANES), lambda b, j: (b, 0, j, 0))
        out_shape = [jax.ShapeDtypeStruct((Bn, n_pairs, L, LANES), o_dtype),
                     jax.ShapeDtypeStruct((Bn, n_pairs, L, LANES), F32)]
        out_specs = [out_block, out_block]
    else:
        assert d == 1
        out_shape = [jax.ShapeDtypeStruct((Bn, L, qw), o_dtype)]
        out_specs = [pl.BlockSpec((1, tq, qw), lambda b, j: (b, j, 0))]
    return pl.pallas_call(
        functools.partial(_banded_kernel, n_sub=n_sub, half=half, kv_cols=kv_cols, n_pairs=n_pairs,
                          shared_kv=shared_kv, has_sink=sink is not None, with_lse=with_lse, dilation=d),
        out_shape=tuple(out_shape),
        grid=(Bn, nbt),
        in_specs=in_specs,
        out_specs=tuple(out_specs),
        scratch_shapes=[pltpu.VMEM((tq + 2 * half, 2 * kvw), BF16)],
        compiler_params=_cparams(("parallel", "arbitrary")),
        name=f"banded_attn_d{d}_h{half}",
    )(*args)


NBR_ROWS_PER_STEP = 8
NBR_KEY_BLOCK_ROWS = 4


def _nbr_bias(rpb):
    H = rpb.shape[0]
    c = np.arange(GRID_W)
    qstart = np.clip(c - NA_KW // 2, 0, GRID_W - NA_KW)
    kc = np.arange(GRID_W)
    valid = (kc[None, :] >= qstart[:, None]) & (kc[None, :] < qstart[:, None] + NA_KW)
    cidx = np.clip(kc[None, :] - c[:, None] + NA_KW - 1, 0, 2 * NA_KW - 2)
    ridx = np.arange(NA_KH)[None, :] + NA_KH - 1 - np.arange(NA_KH)[:, None]
    row_sel = _one_hot(ridx, 2 * NA_KH - 1)
    col_sel = _one_hot(cidx, 2 * NA_KW - 1)
    rows = jnp.einsum('vra,hab->hvrb', row_sel, rpb.astype(F32), precision=HIGHEST)
    tab = jnp.einsum('hvrb,ckb->vhcrk', rows, col_sel, precision=HIGHEST)
    tab = jnp.where(jnp.asarray(valid)[None, None, :, None, :], tab, NEG_INF)
    return tab.reshape(NA_KH, H // 2, 2 * GRID_W, NA_KH * GRID_W)


def _nbr_kernel(q_ref, kv0, kv1, kv2, kv3, bias_ref, o_ref, kv_s, *, rows):
    blk = NBR_KEY_BLOCK_ROWS * GRID_W
    for t, ref in enumerate((kv0, kv1, kv2, kv3)):
        kv_s[t * blk:(t + 1) * blk, :] = ref[...]
    i0 = pl.program_id(1) * NBR_ROWS_PER_STEP
    nkeys = NA_KH * GRID_W
    n_pairs = C_HEADS // 2

    def one_row(r, carry):
        i = i0 + r
        rstart = jnp.clip(i - NA_KH // 2, 0, rows - NA_KH)
        off = pl.multiple_of((rstart - i0 + NBR_KEY_BLOCK_ROWS) * GRID_W, GRID_W)
        var = i - rstart
        q0 = pl.multiple_of(r * GRID_W, GRID_W)

        def scores(c):
            q2 = _pair_rows(q_ref[pl.ds(q0, GRID_W), c * LANES:(c + 1) * LANES])
            k = kv_s[pl.ds(off, nkeys), c * LANES:(c + 1) * LANES]
            return lax.dot_general(q2, k, (((1,), (1,)), ((), ())), preferred_element_type=F32)

        s_next = scores(0)
        o_cols = []
        for c in range(n_pairs):
            s = s_next + bias_ref[var, c]
            if c + 1 < n_pairs:
                s_next = scores(c + 1)
            m = jnp.max(s, axis=-1, keepdims=True)
            p = jnp.exp(s - m)
            denom = jnp.sum(p, axis=-1, keepdims=True)
            v = kv_s[pl.ds(off, nkeys), C_W + c * LANES:C_W + (c + 1) * LANES]
            o2 = jnp.dot(p.astype(BF16), v, preferred_element_type=F32) / denom
            o_cols.append(_unpair_rows(o2))
        o_ref[pl.ds(q0, GRID_W), :] = jnp.concatenate(o_cols, axis=-1).astype(o_ref.dtype)
        return carry

    lax.fori_loop(0, NBR_ROWS_PER_STEP, one_row, 0)


def _neighborhood_attention(arr, bias, Bn, L):
    rows = L // GRID_W
    assert rows % NBR_ROWS_PER_STEP == 0 and rows >= 2 * NBR_ROWS_PER_STEP
    tq = NBR_ROWS_PER_STEP * GRID_W
    kb = NBR_KEY_BLOCK_ROWS * GRID_W
    nkb = L // kb
    per_step = NBR_ROWS_PER_STEP // NBR_KEY_BLOCK_ROWS
    nq = L // tq

    def key_spec(t):
        return pl.BlockSpec((kb, 2 * C_W), lambda b, j: (b * nkb + jnp.clip(j * per_step - 1 + t, 0, nkb - 1), 0))

    in_specs = ([pl.BlockSpec((tq, C_W), lambda b, j: (b * nq + j, 2))]
                + [key_spec(t) for t in range(4)]
                + [pl.BlockSpec(bias.shape, lambda b, j: (0, 0, 0, 0))])
    return pl.pallas_call(
        functools.partial(_nbr_kernel, rows=rows),
        out_shape=jax.ShapeDtypeStruct((Bn * L, C_W), BF16),
        grid=(Bn, nq),
        in_specs=in_specs,
        out_specs=pl.BlockSpec((tq, C_W), lambda b, j: (b * nq + j, 0)),
        scratch_shapes=[pltpu.VMEM((4 * kb, 2 * C_W), BF16)],
        compiler_params=_cparams(("parallel", "arbitrary")),
        name="nbr_attn",
    )(*([arr] * 5), bias)


def _post_kernel(h_ref, hb_ref, oa_ref, ob0_ref, ob1_ref, ob2_ref, l0_ref, l1_ref, l2_ref, oc_ref,
                 wg_ref, bg_ref, wa_ref, wb_ref, wc_ref, wo_ref, g_ref, b_ref, wr_ref, br_ref,
                 o_ref, obf_ref, logit_ref, *, alpha):
    D = h_ref.shape[1]
    hb = hb_ref[...]
    def planes(ref):
        return jnp.concatenate([ref[0, c] for c in range(ref.shape[1])], axis=-1).astype(F32)

    l0, l1, l2 = planes(l0_ref), planes(l1_ref), planes(l2_ref)
    lm = jnp.maximum(jnp.maximum(l0, l1), l2)
    e0, e1, e2 = jnp.exp(l0 - lm), jnp.exp(l1 - lm), jnp.exp(l2 - lm)
    esum = e0 + e1 + e2
    o_b = (e0 / esum * planes(ob0_ref) + e1 / esum * planes(ob1_ref) + e2 / esum * planes(ob2_ref)).astype(BF16)
    merged = None
    for idx, (o_br, w_ref) in enumerate(((oa_ref[...], wa_ref), (o_b, wb_ref), (oc_ref[...], wc_ref))):
        gate = jax.nn.sigmoid(
            jnp.dot(hb, wg_ref[:, idx * D:(idx + 1) * D], preferred_element_type=F32)
            + bg_ref[:, idx * D:(idx + 1) * D])
        term = gate * jnp.dot(o_br, w_ref[...], preferred_element_type=F32)
        merged = term if merged is None else merged + term
    y = jnp.dot(merged.astype(BF16), wo_ref[...], preferred_element_type=F32)
    h1 = _ln_rows(alpha * h_ref[...] + y, g_ref[...], b_ref[...])
    o_ref[...] = h1
    obf_ref[...] = h1.astype(BF16)
    logit_ref[...] = jnp.dot(h1, wr_ref[...], preferred_element_type=F32, precision=HIGHEST) + br_ref[...]


def _post_attention(h, hb, o_a, o_bs, lses, o_c, wg, bg, wa, wb, wc, wo, g, b, wr, br, alpha, tm=256):
    T, D = h.shape
    n_planes, L = o_bs[0].shape[1:3]
    nt = L // tm
    assert L % tm == 0

    def rows(w):
        return pl.BlockSpec((tm, w), lambda i: (i, 0))

    plane_rows = pl.BlockSpec((1, n_planes, tm, LANES), lambda i: (i // nt, 0, i % nt, 0))

    def full(a):
        return pl.BlockSpec(a.shape, lambda i: (0, 0))

    weights = [wg, bg, wa, wb, wc, wo, g, b, wr, br]
    return pl.pallas_call(
        functools.partial(_post_kernel, alpha=alpha),
        out_shape=(jax.ShapeDtypeStruct((T, D), F32), jax.ShapeDtypeStruct((T, D), BF16),
                   jax.ShapeDtypeStruct((T, LANES), F32)),
        grid=(T // tm,),
        in_specs=[rows(D), rows(D), rows(A_Q)] + [plane_rows] * 6 + [rows(C_W)] + [full(w) for w in weights],
        out_specs=(rows(D), rows(D), rows(LANES)),
        compiler_params=_cparams(("parallel",)),
        name="post_attn",
    )(h, hb, o_a, *o_bs, *lses, o_c, *weights)


ROUTE_EID, ROUTE_GATE, ROUTE_RANK = 0, 2, 4


def _route_kernel(logit_ref, meta_ref, count_ref, run_ref):
    tm = logit_ref.shape[0]

    @pl.when(pl.program_id(0) == 0)
    def _():
        run_ref[...] = jnp.zeros_like(run_ref)

    x = logit_ref[...]
    lane = lax.broadcasted_iota(jnp.int32, x.shape, 1).astype(F32)
    big = float(LANES)

    def lane_max(v):
        return jnp.max(v, axis=-1, keepdims=True)

    def first_lane(mask):
        return jnp.min(jnp.where(mask, lane, big), axis=-1, keepdims=True)

    is_g = lane < N_GROUPS
    gl = jnp.where(is_g, x, NEG_INF)
    gmax = lane_max(gl)
    g_idx = first_lane(jnp.logical_and(is_g, gl == gmax))
    g_w = 1.0 / jnp.sum(jnp.where(is_g, jnp.exp(gl - gmax), 0.0), axis=-1, keepdims=True)

    e_lo = N_GROUPS + g_idx * EXPERTS_PER_GROUP
    in_grp = jnp.logical_and(lane >= e_lo, lane < e_lo + EXPERTS_PER_GROUP)
    el = jnp.where(in_grp, x, NEG_INF)
    top1 = lane_max(el)
    lane1 = first_lane(jnp.logical_and(in_grp, el == top1))
    el2 = jnp.where(lane == lane1, NEG_INF, el)
    top2 = lane_max(el2)
    lane2 = first_lane(jnp.logical_and(in_grp, jnp.logical_and(lane != lane1, el2 == top2)))
    e2 = jnp.exp(top2 - top1)
    w1 = 1.0 / (1.0 + e2)
    w2 = e2 / (1.0 + e2)

    eid1 = lane1 - N_GROUPS
    eid2 = lane2 - N_GROUPS
    onehot = jnp.logical_or(lane == eid1, lane == eid2)
    oh = jnp.where(onehot, 1.0, 0.0).astype(BF16)
    r_i = lax.broadcasted_iota(jnp.int32, (tm, tm), 0)
    c_i = lax.broadcasted_iota(jnp.int32, (tm, tm), 1)
    strict_lower = jnp.where(c_i < r_i, 1.0, 0.0).astype(BF16)
    before = jnp.dot(strict_lower, oh, preferred_element_type=F32) + run_ref[...]
    rank1 = jnp.sum(jnp.where(lane == eid1, before, 0.0), axis=-1, keepdims=True)
    rank2 = jnp.sum(jnp.where(lane == eid2, before, 0.0), axis=-1, keepdims=True)
    run_ref[...] = run_ref[...] + jnp.sum(oh.astype(F32), axis=0, keepdims=True)

    meta = jnp.zeros(x.shape, F32)
    for k, val in ((ROUTE_EID, eid1), (ROUTE_EID + 1, eid2),
                   (ROUTE_GATE, g_w * w1), (ROUTE_GATE + 1, g_w * w2),
                   (ROUTE_RANK, rank1), (ROUTE_RANK + 1, rank2)):
        meta = jnp.where(lane == k, val, meta)
    meta_ref[...] = meta
    count_ref[...] = run_ref[...]


def _route(logits, tm=512):
    T = logits.shape[0]
    return pl.pallas_call(
        _route_kernel,
        out_shape=(jax.ShapeDtypeStruct((T, LANES), F32), jax.ShapeDtypeStruct((1, LANES), F32)),
        grid=(T // tm,),
        in_specs=[pl.BlockSpec((tm, LANES), lambda i: (i, 0))],
        out_specs=(pl.BlockSpec((tm, LANES), lambda i: (i, 0)), pl.BlockSpec((1, LANES), lambda i: (0, 0))),
        scratch_shapes=[pltpu.VMEM((1, LANES), F32)],
        compiler_params=_cparams(("arbitrary",)),
        name="route",
    )(logits)


def _expert_kernel(blk_e_ref, n_used_ref, x_ref, wg_ref, wu_ref, wd_ref, o_ref):
    b = pl.program_id(0)

    @pl.when(b < n_used_ref[0])
    def _():
        x = x_ref[...]
        a = jnp.dot(x, wg_ref[0], preferred_element_type=F32)
        u = jnp.dot(x, wu_ref[0], preferred_element_type=F32)
        hid = (a * jax.nn.sigmoid(a) * u).astype(BF16)
        o_ref[...] = jnp.dot(hid, wd_ref[0], preferred_element_type=F32)

    @pl.when(b >= n_used_ref[0])
    def _():
        o_ref[...] = jnp.zeros_like(o_ref)


def _experts(blk_e, n_used, xs, wg, wu, wd):
    P, D = xs.shape
    nblk = P // MOE_ROWS
    de = wg.shape[2]
    grid_spec = pltpu.PrefetchScalarGridSpec(
        num_scalar_prefetch=2,
        grid=(nblk,),
        in_specs=[pl.BlockSpec((MOE_ROWS, D), lambda b, be, nu: (b, 0)),
                  pl.BlockSpec((1, D, de), lambda b, be, nu: (be[b], 0, 0)),
                  pl.BlockSpec((1, D, de), lambda b, be, nu: (be[b], 0, 0)),
                  pl.BlockSpec((1, de, D), lambda b, be, nu: (be[b], 0, 0))],
        out_specs=pl.BlockSpec((MOE_ROWS, D), lambda b, be, nu: (b, 0)),
    )
    return pl.pallas_call(
        _expert_kernel,
        out_shape=jax.ShapeDtypeStruct((P, D), F32),
        grid_spec=grid_spec,
        compiler_params=_cparams(("arbitrary",)),
        name="experts",
    )(blk_e, n_used, xs, wg, wu, wd)


def _combine_kernel(h_ref, ya_ref, yb_ref, meta_ref, g_ref, b_ref, o_ref, ob_ref, *, alpha):
    meta = meta_ref[...]
    g1 = meta[:, ROUTE_GATE:ROUTE_GATE + 1]
    g2 = meta[:, ROUTE_GATE + 1:ROUTE_GATE + 2]
    y = ya_ref[...] * g1 + yb_ref[...] * g2
    h2 = _ln_rows(alpha * h_ref[...] + y, g_ref[...], b_ref[...])
    o_ref[...] = h2
    ob_ref[...] = h2.astype(BF16)


def _combine(h, ya, yb, meta, g, b, alpha, tm=512):
    T, D = h.shape
    rows = pl.BlockSpec((tm, D), lambda i: (i, 0))
    vec = pl.BlockSpec((1, D), lambda i: (0, 0))
    return pl.pallas_call(
        functools.partial(_combine_kernel, alpha=alpha),
        out_shape=(jax.ShapeDtypeStruct((T, D), F32), jax.ShapeDtypeStruct((T, D), BF16)),
        grid=(T // tm,),
        in_specs=[rows, rows, rows, pl.BlockSpec((tm, LANES), lambda i: (i, 0)), vec, vec],
        out_specs=(rows, rows),
        compiler_params=_cparams(("parallel",)),
        name="combine_ln2",
    )(h, ya, yb, meta, g.reshape(1, D), b.reshape(1, D))


def _moe(h1, h1b, logits, w_eg, w_eu, w_ed, g, b, alpha):
    T, D = h1.shape
    meta, counts = _route(logits)
    eid = meta[:, ROUTE_EID:ROUTE_EID + 2].astype(jnp.int32)
    rank = meta[:, ROUTE_RANK:ROUTE_RANK + 2].astype(jnp.int32)
    cnt = counts[0, :N_EXPERTS].astype(jnp.int32)
    padded = (cnt + MOE_ROWS - 1) // MOE_ROWS * MOE_ROWS
    pad_end = jnp.cumsum(padded)
    pad_off = pad_end - padded
    dest = pad_off[eid] + rank
    nblk = -(-(2 * T) // MOE_ROWS) + N_EXPERTS
    P = nblk * MOE_ROWS
    tok = jnp.broadcast_to(jnp.arange(T, dtype=jnp.int32)[:, None], (T, 2))
    buf_tok = jnp.zeros((P,), jnp.int32).at[dest.reshape(-1)].set(tok.reshape(-1), unique_indices=True)
    blk_start = jnp.arange(nblk, dtype=jnp.int32) * MOE_ROWS
    blk_e = jnp.minimum(jnp.sum((pad_end[None, :] <= blk_start[:, None]).astype(jnp.int32), axis=1), N_EXPERTS - 1)
    n_used = (pad_end[-1:] // MOE_ROWS).astype(jnp.int32)
    xs = jnp.take(h1b, buf_tok, axis=0)
    y = _experts(blk_e, n_used, xs, w_eg, w_eu, w_ed)
    ya = jnp.take(y, dest[:, 0], axis=0)
    yb = jnp.take(y, dest[:, 1], axis=0)
    return _combine(h1, ya, yb, meta, g, b, alpha)


def kernel(x, ln0_g, ln0_b, rel_bias, w_in, b_gate, sink_a, rpb_c, w_br_a, w_br_b, w_br_c, w_out,
           ln1_g, ln1_b, w_rg, b_rg, w_re, b_re, w_eg, w_eu, w_ed, ln2_g, ln2_b):
    Bn, L, D = x.shape
    depth = w_in.shape[0]
    T = Bn * L
    alpha = float((2 * depth) ** 0.25)
    a_order = np.asarray(A_HEAD_ORDER)

    a_pairs = [(c, c + A_HEADS // 2) for c in range(A_HEADS // 2)]
    b_pairs = [(2 * c, 2 * c + 1) for c in range(B_HEADS_PER_GROUP // 2)]
    bias_a = _band_bias(rel_bias[:, :A_HEADS], a_pairs, A_HALF_WINDOW, 1)
    bias_b = []
    for g, (window, dilation) in enumerate(B_CONFIGS):
        c0 = A_HEADS + g * B_HEADS_PER_GROUP
        bias_b.append(_band_bias(rel_bias[:, c0:c0 + B_HEADS_PER_GROUP], b_pairs, window // (2 * dilation), dilation))

    h, hb = _layer_norm(x.reshape(T, D), ln0_g, ln0_b)
    for l in range(depth):
        w_gate = w_in[l, :, QKV_COLS:].astype(BF16)
        arr_a, arr_b0, arr_b1, arr_b2, arr_c = _in_projection(hb, _inproj_weight(w_in[l]), Bn, L)

        (o_a,) = _banded_attention(arr_a, bias_a, sink_a[l][a_order], tq=512, n_pairs=A_HEADS // 2, shared_kv=True,
                                   half=A_HALF_WINDOW, with_lse=False)
        o_bs, lses = [], []
        for g, (arr, tq) in enumerate(((arr_b0, 512), (arr_b1, 512), (arr_b2, 256))):
            window, dilation = B_CONFIGS[g]
            o, lse = _banded_attention(arr, bias_b[g], None, tq=tq, n_pairs=B_HEADS_PER_GROUP // 2, shared_kv=False,
                                       half=window // (2 * dilation), with_lse=True)
            o_bs.append(o)
            lses.append(lse)
        o_c = _neighborhood_attention(arr_c, _nbr_bias(rpb_c[l]), Bn, L)

        w_r = jnp.concatenate([w_rg[l], jnp.transpose(w_re[l], (1, 0, 2)).reshape(D, N_EXPERTS)], axis=1)
        w_r = jnp.pad(w_r.astype(F32), ((0, 0), (0, LANES - w_r.shape[1])))
        b_r = jnp.concatenate([b_rg[l], b_re[l].reshape(-1)]).astype(F32)
        b_r = jnp.pad(b_r, (0, LANES - b_r.shape[0])).reshape(1, LANES)
        w_a = w_br_a[l].reshape(A_HEADS, HEAD_DIM, D)[a_order].reshape(A_Q, D)

        h1, h1b, logits = _post_attention(
            h, hb, o_a.reshape(T, A_Q), o_bs, lses, o_c,
            w_gate, b_gate[l].reshape(1, N_BRANCH * D).astype(F32),
            w_a.astype(BF16), w_br_b[l].astype(BF16), w_br_c[l].astype(BF16), w_out[l].astype(BF16),
            ln1_g[l].reshape(1, D), ln1_b[l].reshape(1, D), w_r, b_r, alpha)

        h, hb = _moe(h1, h1b, logits, w_eg[l].astype(BF16), w_eu[l].astype(BF16), w_ed[l].astype(BF16),
                     ln2_g[l], ln2_b[l], alpha)
    return h.reshape(Bn, L, D)
```

```python
import functools

import numpy as np
import jax
import jax.numpy as jnp
from jax import lax
from jax.experimental import pallas as pl
from jax.experimental.pallas import tpu as pltpu

F32 = jnp.float32
BF16 = jnp.bfloat16
HIGHEST = lax.Precision.HIGHEST

HEAD_DIM = 64
A_HEADS = 8
A_KV_HEADS = 2
A_HALF_WINDOW = 128
B_HEADS_PER_GROUP = 4
B_CONFIGS = ((128, 1), (512, 4), (2048, 16))
C_HEADS = 8
GRID_W = 64
NA_KH = 8
NA_KW = 16
NUM_BUCKETS = 32
REL_MAX_DIST = 2048
N_GROUPS = 4
EXPERTS_PER_GROUP = 8
N_EXPERTS = N_GROUPS * EXPERTS_PER_GROUP
D_EXPERT = 512
LN_EPS = 1e-5
NEG_INF = -1e30

A_Q = A_HEADS * HEAD_DIM
A_KV = A_KV_HEADS * HEAD_DIM
B_W = B_HEADS_PER_GROUP * HEAD_DIM
C_W = C_HEADS * HEAD_DIM
N_BRANCH = 3
QKV_COLS = A_Q + 2 * A_KV + 3 * len(B_CONFIGS) * B_W + 3 * C_W
B_COL0 = A_Q + 2 * A_KV
C_COL0 = B_COL0 + 3 * len(B_CONFIGS) * B_W

ATT_BLK = 128
LANES = 128
MOE_ROWS = 256
VMEM_LIMIT = 56 * 1024 * 1024

A_HEAD_ORDER = tuple(h for c in range(A_HEADS // 2) for h in (c, c + A_HEADS // 2))


def _cparams(sem):
    return pltpu.CompilerParams(dimension_semantics=sem, vmem_limit_bytes=VMEM_LIMIT)


def _ln_rows(x, g, b):
    mu = jnp.mean(x, axis=-1, keepdims=True)
    xc = x - mu
    var = jnp.mean(xc * xc, axis=-1, keepdims=True)
    return xc * lax.rsqrt(var + LN_EPS) * g + b


def _ln_kernel(x_ref, g_ref, b_ref, o_ref, ob_ref):
    y = _ln_rows(x_ref[...].astype(F32), g_ref[...], b_ref[...])
    o_ref[...] = y
    ob_ref[...] = y.astype(BF16)


def _layer_norm(x, g, b, tm=512):
    T, D = x.shape
    return pl.pallas_call(
        _ln_kernel,
        out_shape=(jax.ShapeDtypeStruct((T, D), F32), jax.ShapeDtypeStruct((T, D), BF16)),
        grid=(T // tm,),
        in_specs=[pl.BlockSpec((tm, D), lambda i: (i, 0)),
                  pl.BlockSpec((1, D), lambda i: (0, 0)),
                  pl.BlockSpec((1, D), lambda i: (0, 0))],
        out_specs=(pl.BlockSpec((tm, D), lambda i: (i, 0)), pl.BlockSpec((tm, D), lambda i: (i, 0))),
        compiler_params=_cparams(("parallel",)),
        name="ln0",
    )(x, g.reshape(1, D), b.reshape(1, D))


BAND_W = 3 * B_W
NBR_W = 3 * C_W


def _inproj_weight(w_in_l):
    scale = HEAD_DIM ** -0.5
    a_q = w_in_l[:, :A_Q].reshape(-1, A_HEADS, HEAD_DIM)[:, np.asarray(A_HEAD_ORDER)].reshape(-1, A_Q)
    cols = [w_in_l[:, A_Q:A_Q + 2 * A_KV], a_q * scale]
    for g in range(len(B_CONFIGS)):
        c0 = B_COL0 + 3 * g * B_W
        cols += [w_in_l[:, c0 + B_W:c0 + 3 * B_W], w_in_l[:, c0:c0 + B_W] * scale]
    cols += [w_in_l[:, C_COL0 + C_W:C_COL0 + 3 * C_W], w_in_l[:, C_COL0:C_COL0 + C_W] * scale]
    return jnp.concatenate(cols, axis=1).astype(BF16)


def _inproj_kernel(h_ref, w_ref, a_ref, b0_ref, b1_ref, b2_ref, c_ref, tmp_ref):
    h = h_ref[...]
    tm = h.shape[0]

    def chunk(i):
        return jnp.dot(h, w_ref[:, i * BAND_W:(i + 1) * BAND_W], preferred_element_type=F32)

    a_ref[0, 0] = chunk(0).astype(BF16)
    b0_ref[0, 0] = chunk(1).astype(BF16)
    for i, ref in ((2, b1_ref), (3, b2_ref)):
        d = ref.shape[1]
        res = chunk(i)
        n_cols = BAND_W // LANES
        for c in range(n_cols):
            tmp_ref[c] = res[:, c * LANES:(c + 1) * LANES]
        for r in range(d):
            for c in range(n_cols):
                ref[0, r, :, c * LANES:(c + 1) * LANES] = tmp_ref[c, pl.ds(r, tm // d, stride=d), :].astype(BF16)
    c_ref[:, :BAND_W] = chunk(4).astype(BF16)
    c_ref[:, BAND_W:] = chunk(5).astype(BF16)


def _in_projection(hb, w, Bn, L, tm=512):
    T, D = hb.shape
    nt = L // tm
    assert L % tm == 0 and NBR_W == 2 * BAND_W and A_Q + 2 * A_KV == BAND_W
    dils = [d for _, d in B_CONFIGS]
    assert dils[0] == 1 and len(dils) == 3

    def band_spec(d):
        return pl.BlockSpec((1, d, tm // d, BAND_W), lambda b, i: (b, 0, i, 0))

    return pl.pallas_call(
        _inproj_kernel,
        out_shape=(jax.ShapeDtypeStruct((Bn, 1, L, BAND_W), BF16),
                   jax.ShapeDtypeStruct((Bn, 1, L, BAND_W), BF16),
                   jax.ShapeDtypeStruct((Bn, dils[1], L // dils[1], BAND_W), BF16),
                   jax.ShapeDtypeStruct((Bn, dils[2], L // dils[2], BAND_W), BF16),
                   jax.ShapeDtypeStruct((T, NBR_W), BF16)),
        grid=(Bn, nt),
        in_specs=[pl.BlockSpec((tm, D), lambda b, i: (b * nt + i, 0)),
                  pl.BlockSpec(w.shape, lambda b, i: (0, 0))],
        out_specs=(band_spec(1), band_spec(1), band_spec(dils[1]), band_spec(dils[2]),
                   pl.BlockSpec((tm, NBR_W), lambda b, i: (b * nt + i, 0))),
        scratch_shapes=[pltpu.VMEM((BAND_W // LANES, tm, LANES), F32)],
        compiler_params=_cparams(("parallel", "parallel")),
        name="in_proj",
    )(hb, w)


def _t5_bucket(rel):
    half = NUM_BUCKETS // 2
    max_exact = half // 2
    ret = np.where(rel > 0, half, 0)
    n = np.abs(rel)
    large = max_exact + (np.log(np.maximum(n, max_exact) / max_exact)
                         / np.log(REL_MAX_DIST / max_exact) * (half - max_exact)).astype(np.int32)
    large = np.minimum(large, half - 1)
    return (ret + np.where(n < max_exact, n, large)).astype(np.int32)


def _one_hot_t(idx, n):
    return jnp.asarray((np.arange(n)[:, None] == np.asarray(idx).reshape(1, -1)).astype(np.float32))


def _band_bias(bias_tab, head_pairs, half, dist_scale):
    kl = ATT_BLK + 2 * half
    off = np.arange(kl)[None, :] - half - np.arange(ATT_BLK)[:, None]
    band = np.abs(off) <= half
    onehot = _one_hot_t(_t5_bucket(off * dist_scale), NUM_BUCKETS)
    order = np.asarray([h for pair in head_pairs for h in pair])
    bias = jnp.dot(bias_tab.astype(F32)[:, order].T, onehot, precision=HIGHEST)
    bias = bias.reshape(len(head_pairs), 2 * ATT_BLK, kl)
    col = np.arange(kl)
    first_ok = (col >= half)[None, :]
    last_ok = (col < ATT_BLK + half)[None, :]
    out = []
    for first, last in ((0, 0), (1, 0), (0, 1), (1, 1)):
        ok = band & (first_ok if first else True) & (last_ok if last else True)
        ok = np.concatenate([ok, ok], axis=0)
        out.append(jnp.where(jnp.asarray(ok)[None], bias, NEG_INF))
    return jnp.stack(out)


def _pair_rows(x):
    lo = lax.broadcasted_iota(jnp.int32, x.shape, 1) < HEAD_DIM
    zero = jnp.zeros_like(x)
    return jnp.concatenate([jnp.where(lo, x, zero), jnp.where(lo, zero, x)], axis=0)


def _unpair_rows(x2):
    rows = x2.shape[0] // 2
    lo = lax.broadcasted_iota(jnp.int32, (rows, LANES), 1) < HEAD_DIM
    return jnp.where(lo, x2[:rows], x2[rows:])


def _banded_kernel(*refs, n_sub, sub_per_iter, half, kv_cols, n_pairs, shared_kv, has_sink, with_lse, dilation):
    cur_ref, prev_ref, next_ref, bias_ref = refs[:4]
    pos = 4
    sink_ref = None
    if has_sink:
        sink_ref = refs[pos]
        pos += 1
    o_ref = refs[pos]
    pos += 1
    lse_ref = None
    if with_lse:
        lse_ref = refs[pos]
        pos += 1
    kv_s, s_s, p_s = refs[pos:pos + 3]

    tq = n_sub * ATT_BLK
    kl = ATT_BLK + 2 * half
    kvw = kv_cols * LANES
    d = dilation
    j = pl.program_id(1)
    last_j = pl.num_programs(1) - 1
    row_is_lo = lax.broadcasted_iota(jnp.int32, (2 * ATT_BLK, 1), 0) < ATT_BLK

    def one_residue(r, carry):
        kv_s[0:half, :] = prev_ref[0, r]
        kv_s[half:half + tq, :] = cur_ref[0, r, :, :2 * kvw]
        kv_s[half + tq:, :] = next_ref[0, r]

        def sub_blocks(it, carry2):
            units = [(ii, c) for ii in range(sub_per_iter) for c in range(n_pairs)]
            row0s, variants = [], []
            for ii in range(sub_per_iter):
                i = it * sub_per_iter + ii
                row0s.append(pl.multiple_of(i * ATT_BLK, ATT_BLK))
                is_first = jnp.logical_and(i == 0, j == 0).astype(jnp.int32)
                is_last = jnp.logical_and(i == n_sub - 1, j == last_j).astype(jnp.int32)
                variants.append(is_first + 2 * is_last)

            for u, (ii, c) in enumerate(units):
                kc = 0 if shared_kv else c
                q2 = _pair_rows(cur_ref[0, r, pl.ds(row0s[ii], ATT_BLK),
                                        2 * kvw + c * LANES:2 * kvw + (c + 1) * LANES])
                k = kv_s[pl.ds(row0s[ii], kl), kc * LANES:(kc + 1) * LANES]
                s_s[u] = lax.dot_general(q2, k, (((1,), (1,)), ((), ())), preferred_element_type=F32)

            stats = []
            for u, (ii, c) in enumerate(units):
                s = s_s[u] + bias_ref[variants[ii], c]
                m = jnp.max(s, axis=-1, keepdims=True)
                if has_sink:
                    sk = jnp.where(row_is_lo, sink_ref[2 * c], sink_ref[2 * c + 1])
                    m = jnp.maximum(m, sk)
                p = jnp.exp(s - m)
                denom = jnp.sum(p, axis=-1, keepdims=True)
                if has_sink:
                    denom = denom + jnp.exp(sk - m)
                p_s[u] = p.astype(BF16)
                stats.append((m, denom))

            for ii in range(sub_per_iter):
                o_cols, lse_cols = [], []
                for c in range(n_pairs):
                    u = ii * n_pairs + c
                    m, denom = stats[u]
                    kc = 0 if shared_kv else c
                    v = kv_s[pl.ds(row0s[ii], kl), kvw + kc * LANES:kvw + (kc + 1) * LANES]
                    o2 = jnp.dot(p_s[u], v, preferred_element_type=F32) / denom
                    o_cols.append(_unpair_rows(o2))
                    if with_lse:
                        lse_cols.append(_unpair_rows(jnp.broadcast_to(m + jnp.log(denom), (2 * ATT_BLK, LANES))))
                if d == 1:
                    rows = pl.ds(row0s[ii], ATT_BLK)
                else:
                    rows = pl.ds(row0s[ii] * d + r, ATT_BLK, stride=d)
                if with_lse:
                    for c in range(n_pairs):
                        o_ref[0, c, rows, :] = o_cols[c].astype(o_ref.dtype)
                        lse_ref[0, c, rows, :] = lse_cols[c]
                else:
                    o_ref[0, rows, :] = jnp.concatenate(o_cols, axis=-1).astype(o_ref.dtype)
            return carry2

        lax.fori_loop(0, n_sub // sub_per_iter, sub_blocks, 0)
        return carry

    lax.fori_loop(0, d, one_residue, 0)


def _banded_attention(arr, bias, sink, *, tq, n_pairs, shared_kv, half, with_lse):
    Bn, d, Ls, W = arr.shape
    L = d * Ls
    qw = n_pairs * LANES
    kv_cols = (W - qw) // (2 * LANES)
    kvw = kv_cols * LANES
    tq = min(tq, Ls)
    n_sub = tq // ATT_BLK
    nbt = Ls // tq
    sub_per_iter = 2 if (n_pairs <= 2 and n_sub % 2 == 0) else 1
    n_units = sub_per_iter * n_pairs
    halo_per_tile = tq // half
    n_halo = Ls // half
    assert Ls % tq == 0 and tq % half == 0 and kvw * 2 + qw == W
    o_dtype = BF16 if d == 1 else F32

    in_specs = [
        pl.BlockSpec((1, d, tq, W), lambda b, j: (b, 0, j, 0)),
        pl.BlockSpec((1, d, half, 2 * kvw), lambda b, j: (b, 0, jnp.maximum(j * halo_per_tile - 1, 0), 0)),
        pl.BlockSpec((1, d, half, 2 * kvw),
                     lambda b, j: (b, 0, jnp.minimum((j + 1) * halo_per_tile, n_halo - 1), 0)),
        pl.BlockSpec(bias.shape, lambda b, j: (0, 0, 0, 0)),
    ]
    args = [arr, arr, arr, bias]
    if sink is not None:
        in_specs.append(pl.BlockSpec(memory_space=pltpu.SMEM))
        args.append(sink.astype(F32))
    if with_lse:
        out_block = pl.BlockSpec((1, n_pairs, tq * d, LANES), lambda b, j: (b, 0, j, 0))
        out_shape = [jax.ShapeDtypeStruct((Bn, n_pairs, L, LANES), o_dtype),
                     jax.ShapeDtypeStruct((Bn, n_pairs, L, LANES), F32)]
        out_specs = [out_block, out_block]
    else:
        assert d == 1
        out_shape = [jax.ShapeDtypeStruct((Bn, L, qw), o_dtype)]
        out_specs = [pl.BlockSpec((1, tq, qw), lambda b, j: (b, j, 0))]
    return pl.pallas_call(
        functools.partial(_banded_kernel, n_sub=n_sub, sub_per_iter=sub_per_iter, half=half, kv_cols=kv_cols,
                          n_pairs=n_pairs, shared_kv=shared_kv, has_sink=sink is not None, with_lse=with_lse,
                          dilation=d),
        out_shape=tuple(out_shape),
        grid=(Bn, nbt),
        in_specs=in_specs,
        out_specs=tuple(out_specs),
        scratch_shapes=[pltpu.VMEM((tq + 2 * half, 2 * kvw), BF16),
                        pltpu.VMEM((n_units, 2 * ATT_BLK, ATT_BLK + 2 * half), F32),
                        pltpu.VMEM((n_units, 2 * ATT_BLK, ATT_BLK + 2 * half), BF16)],
        compiler_params=_cparams(("parallel", "arbitrary")),
        name=f"banded_attn_d{d}_h{half}",
    )(*args)


NBR_ROWS_PER_STEP = 8
NBR_KEY_BLOCK_ROWS = 4
NBR_ROWS_PER_ITER = 2


def _nbr_bias(rpb):
    H = rpb.shape[0]
    c = np.arange(GRID_W)
    qstart = np.clip(c - NA_KW // 2, 0, GRID_W - NA_KW)
    kc = np.arange(GRID_W)
    valid = (kc[None, :] >= qstart[:, None]) & (kc[None, :] < qstart[:, None] + NA_KW)
    cidx = np.clip(kc[None, :] - c[:, None] + NA_KW - 1, 0, 2 * NA_KW - 2)
    rows = jnp.stack([rpb.astype(F32)[:, NA_KH - 1 - v:2 * NA_KH - 1 - v] for v in range(NA_KH)])
    col_sel = _one_hot_t(cidx, 2 * NA_KW - 1)
    tab = jnp.dot(rows.reshape(-1, 2 * NA_KW - 1), col_sel, precision=HIGHEST)
    tab = tab.reshape(NA_KH, H, NA_KH, GRID_W, GRID_W).transpose(0, 1, 3, 2, 4)
    tab = jnp.where(jnp.asarray(valid)[None, None, :, None, :], tab, NEG_INF)
    return tab.reshape(NA_KH, H // 2, 2 * GRID_W, NA_KH * GRID_W)


def _nbr_kernel(q_ref, kv0, kv1, kv2, kv3, bias_ref, o_ref, kv_s, s_s, p_s, *, rows):
    blk = NBR_KEY_BLOCK_ROWS * GRID_W
    for t, ref in enumerate((kv0, kv1, kv2, kv3)):
        kv_s[t * blk:(t + 1) * blk, :] = ref[...]
    i0 = pl.program_id(1) * NBR_ROWS_PER_STEP
    nkeys = NA_KH * GRID_W
    n_pairs = C_HEADS // 2

    def row_group(it, carry):
        offs, variants, q0s = [], [], []
        for rr in range(NBR_ROWS_PER_ITER):
            r = it * NBR_ROWS_PER_ITER + rr
            i = i0 + r
            rstart = jnp.clip(i - NA_KH // 2, 0, rows - NA_KH)
            offs.append(pl.multiple_of((rstart - i0 + NBR_KEY_BLOCK_ROWS) * GRID_W, GRID_W))
            variants.append(i - rstart)
            q0s.append(pl.multiple_of(r * GRID_W, GRID_W))
        units = [(rr, c) for rr in range(NBR_ROWS_PER_ITER) for c in range(n_pairs)]

        for u, (rr, c) in enumerate(units):
            q2 = _pair_rows(q_ref[pl.ds(q0s[rr], GRID_W), c * LANES:(c + 1) * LANES])
            k = kv_s[pl.ds(offs[rr], nkeys), c * LANES:(c + 1) * LANES]
            s_s[u] = lax.dot_general(q2, k, (((1,), (1,)), ((), ())), preferred_element_type=F32)

        denoms = []
        for u, (rr, c) in enumerate(units):
            s = s_s[u] + bias_ref[variants[rr], c]
            m = jnp.max(s, axis=-1, keepdims=True)
            p = jnp.exp(s - m)
            denoms.append(jnp.sum(p, axis=-1, keepdims=True))
            p_s[u] = p.astype(BF16)

        for rr in range(NBR_ROWS_PER_ITER):
            o_cols = []
            for c in range(n_pairs):
                u = rr * n_pairs + c
                v = kv_s[pl.ds(offs[rr], nkeys), C_W + c * LANES:C_W + (c + 1) * LANES]
                o2 = jnp.dot(p_s[u], v, preferred_element_type=F32) / denoms[u]
                o_cols.append(_unpair_rows(o2))
            o_ref[pl.ds(q0s[rr], GRID_W), :] = jnp.concatenate(o_cols, axis=-1).astype(o_ref.dtype)
        return carry

    lax.fori_loop(0, NBR_ROWS_PER_STEP // NBR_ROWS_PER_ITER, row_group, 0)


def _neighborhood_attention(arr, bias, Bn, L):
    rows = L // GRID_W
    assert rows % NBR_ROWS_PER_STEP == 0 and rows >= 2 * NBR_ROWS_PER_STEP
    tq = NBR_ROWS_PER_STEP * GRID_W
    kb = NBR_KEY_BLOCK_ROWS * GRID_W
    nkb = L // kb
    per_step = NBR_ROWS_PER_STEP // NBR_KEY_BLOCK_ROWS
    nq = L // tq

    def key_spec(t):
        return pl.BlockSpec((kb, 2 * C_W), lambda b, j: (b * nkb + jnp.clip(j * per_step - 1 + t, 0, nkb - 1), 0))

    in_specs = ([pl.BlockSpec((tq, C_W), lambda b, j: (b * nq + j, 2))]
                + [key_spec(t) for t in range(4)]
                + [pl.BlockSpec(bias.shape, lambda b, j: (0, 0, 0, 0))])
    return pl.pallas_call(
        functools.partial(_nbr_kernel, rows=rows),
        out_shape=jax.ShapeDtypeStruct((Bn * L, C_W), BF16),
        grid=(Bn, nq),
        in_specs=in_specs,
        out_specs=pl.BlockSpec((tq, C_W), lambda b, j: (b * nq + j, 0)),
        scratch_shapes=[pltpu.VMEM((4 * kb, 2 * C_W), BF16),
                        pltpu.VMEM((NBR_ROWS_PER_ITER * C_HEADS // 2, 2 * GRID_W, NA_KH * GRID_W), F32),
                        pltpu.VMEM((NBR_ROWS_PER_ITER * C_HEADS // 2, 2 * GRID_W, NA_KH * GRID_W), BF16)],
        compiler_params=_cparams(("parallel", "arbitrary")),
        name="nbr_attn",
    )(*([arr] * 5), bias)


def _post_kernel(h_ref, hb_ref, oa_ref, ob0_ref, ob1_ref, ob2_ref, l0_ref, l1_ref, l2_ref, oc_ref,
                 wg_ref, bg_ref, wa_ref, wb_ref, wc_ref, wo_ref, g_ref, b_ref, wr_ref, br_ref,
                 o_ref, obf_ref, logit_ref, *, alpha):
    D = h_ref.shape[1]
    hb = hb_ref[...]
    def planes(ref):
        return jnp.concatenate([ref[0, c] for c in range(ref.shape[1])], axis=-1).astype(F32)

    l0, l1, l2 = planes(l0_ref), planes(l1_ref), planes(l2_ref)
    lm = jnp.maximum(jnp.maximum(l0, l1), l2)
    e0, e1, e2 = jnp.exp(l0 - lm), jnp.exp(l1 - lm), jnp.exp(l2 - lm)
    inv = 1.0 / (e0 + e1 + e2)
    o_b = ((e0 * planes(ob0_ref) + e1 * planes(ob1_ref) + e2 * planes(ob2_ref)) * inv).astype(BF16)
    merged = None
    for idx, (o_br, w_ref) in enumerate(((oa_ref[...], wa_ref), (o_b, wb_ref), (oc_ref[...], wc_ref))):
        z = (jnp.dot(hb, wg_ref[:, idx * D:(idx + 1) * D], preferred_element_type=F32)
             + bg_ref[:, idx * D:(idx + 1) * D])
        gate = 0.5 * jnp.tanh(0.5 * z) + 0.5
        term = gate * jnp.dot(o_br, w_ref[...], preferred_element_type=F32)
        merged = term if merged is None else merged + term
    y = jnp.dot(merged.astype(BF16), wo_ref[...], preferred_element_type=F32)
    h1 = _ln_rows(alpha * h_ref[...] + y, g_ref[...], b_ref[...])
    h1b = h1.astype(BF16)
    o_ref[...] = h1
    obf_ref[...] = h1b
    logit_ref[...] = jnp.dot(h1b, wr_ref[...], preferred_element_type=F32) + br_ref[...]


def _post_attention(h, hb, o_a, o_bs, lses, o_c, wg, bg, wa, wb, wc, wo, g, b, wr, br, alpha, tm=256):
    T, D = h.shape
    n_planes, L = o_bs[0].shape[1:3]
    nt = L // tm
    assert L % tm == 0

    def rows(w):
        return pl.BlockSpec((tm, w), lambda i: (i, 0))

    plane_rows = pl.BlockSpec((1, n_planes, tm, LANES), lambda i: (i // nt, 0, i % nt, 0))

    def full(a):
        return pl.BlockSpec(a.shape, lambda i: (0, 0))

    weights = [wg, bg, wa, wb, wc, wo, g, b, wr, br]
    return pl.pallas_call(
        functools.partial(_post_kernel, alpha=alpha),
        out_shape=(jax.ShapeDtypeStruct((T, D), F32), jax.ShapeDtypeStruct((T, D), BF16),
                   jax.ShapeDtypeStruct((T, LANES), F32)),
        grid=(T // tm,),
        in_specs=[rows(D), rows(D), rows(A_Q)] + [plane_rows] * 6 + [rows(C_W)] + [full(w) for w in weights],
        out_specs=(rows(D), rows(D), rows(LANES)),
        compiler_params=_cparams(("parallel",)),
        name="post_attn",
    )(h, hb, o_a, *o_bs, *lses, o_c, *weights)


ROUTE_EID, ROUTE_GATE, ROUTE_RANK = 0, 2, 4


def _route_kernel(logit_ref, meta_ref, count_ref, run_ref):
    tm = logit_ref.shape[0]

    @pl.when(pl.program_id(0) == 0)
    def _():
        run_ref[...] = jnp.zeros_like(run_ref)

    x = logit_ref[...]
    lane = lax.broadcasted_iota(jnp.int32, x.shape, 1).astype(F32)
    big = float(LANES)

    def lane_max(v):
        return jnp.max(v, axis=-1, keepdims=True)

    def first_lane(mask):
        return jnp.min(jnp.where(mask, lane, big), axis=-1, keepdims=True)

    is_g = lane < N_GROUPS
    gl = jnp.where(is_g, x, NEG_INF)
    gmax = lane_max(gl)
    g_idx = first_lane(jnp.logical_and(is_g, gl == gmax))
    g_w = 1.0 / jnp.sum(jnp.where(is_g, jnp.exp(gl - gmax), 0.0), axis=-1, keepdims=True)

    e_lo = N_GROUPS + g_idx * EXPERTS_PER_GROUP
    in_grp = jnp.logical_and(lane >= e_lo, lane < e_lo + EXPERTS_PER_GROUP)
    el = jnp.where(in_grp, x, NEG_INF)
    top1 = lane_max(el)
    lane1 = first_lane(jnp.logical_and(in_grp, el == top1))
    el2 = jnp.where(lane == lane1, NEG_INF, el)
    top2 = lane_max(el2)
    lane2 = first_lane(jnp.logical_and(in_grp, jnp.logical_and(lane != lane1, el2 == top2)))
    e2 = jnp.exp(top2 - top1)
    w1 = 1.0 / (1.0 + e2)
    w2 = e2 / (1.0 + e2)

    eid1 = lane1 - N_GROUPS
    eid2 = lane2 - N_GROUPS
    onehot = jnp.logical_or(lane == eid1, lane == eid2)
    oh = jnp.where(onehot, 1.0, 0.0).astype(BF16)
    r_i = lax.broadcasted_iota(jnp.int32, (tm, tm), 0)
    c_i = lax.broadcasted_iota(jnp.int32, (tm, tm), 1)
    strict_lower = jnp.where(c_i < r_i, 1.0, 0.0).astype(BF16)
    before = jnp.dot(strict_lower, oh, preferred_element_type=F32) + run_ref[...]
    rank1 = jnp.sum(jnp.where(lane == eid1, before, 0.0), axis=-1, keepdims=True)
    rank2 = jnp.sum(jnp.where(lane == eid2, before, 0.0), axis=-1, keepdims=True)
    run_ref[...] = run_ref[...] + jnp.sum(oh.astype(F32), axis=0, keepdims=True)

    meta = jnp.zeros(x.shape, F32)
    for k, val in ((ROUTE_EID, eid1), (ROUTE_EID + 1, eid2),
                   (ROUTE_GATE, g_w * w1), (ROUTE_GATE + 1, g_w * w2),
                   (ROUTE_RANK, rank1), (ROUTE_RANK + 1, rank2)):
        meta = jnp.where(lane == k, val, meta)
    meta_ref[...] = meta
    count_ref[...] = run_ref[...]


def _route(logits, tm=512):
    T = logits.shape[0]
    return pl.pallas_call(
        _route_kernel,
        out_shape=(jax.ShapeDtypeStruct((T, LANES), F32), jax.ShapeDtypeStruct((1, LANES), F32)),
        grid=(T // tm,),
        in_specs=[pl.BlockSpec((tm, LANES), lambda i: (i, 0))],
        out_specs=(pl.BlockSpec((tm, LANES), lambda i: (i, 0)), pl.BlockSpec((1, LANES), lambda i: (0, 0))),
        scratch_shapes=[pltpu.VMEM((1, LANES), F32)],
        compiler_params=_cparams(("arbitrary",)),
        name="route",
    )(logits)


def _expert_kernel(blk_e_ref, n_used_ref, x_ref, wg_ref, wu_ref, wd_ref, o_ref):
    b = pl.program_id(0)

    @pl.when(b < n_used_ref[0])
    def _():
        x = x_ref[...]
        a = jnp.dot(x, wg_ref[0], preferred_element_type=F32)
        u = jnp.dot(x, wu_ref[0], preferred_element_type=F32)
        hid = (a * jax.nn.sigmoid(a) * u).astype(BF16)
        o_ref[...] = jnp.dot(hid, wd_ref[0], preferred_element_type=F32)

    @pl.when(b >= n_used_ref[0])
    def _():
        o_ref[...] = jnp.zeros_like(o_ref)


def _experts(blk_e, n_used, xs, wg, wu, wd):
    P, D = xs.shape
    nblk = P // MOE_ROWS
    de = wg.shape[2]
    grid_spec = pltpu.PrefetchScalarGridSpec(
        num_scalar_prefetch=2,
        grid=(nblk,),
        in_specs=[pl.BlockSpec((MOE_ROWS, D), lambda b, be, nu: (b, 0)),
                  pl.BlockSpec((1, D, de), lambda b, be, nu: (be[b], 0, 0)),
                  pl.BlockSpec((1, D, de), lambda b, be, nu: (be[b], 0, 0)),
                  pl.BlockSpec((1, de, D), lambda b, be, nu: (be[b], 0, 0))],
        out_specs=pl.BlockSpec((MOE_ROWS, D), lambda b, be, nu: (b, 0)),
    )
    return pl.pallas_call(
        _expert_kernel,
        out_shape=jax.ShapeDtypeStruct((P, D), F32),
        grid_spec=grid_spec,
        compiler_params=_cparams(("arbitrary",)),
        name="experts",
    )(blk_e, n_used, xs, wg, wu, wd)


def _combine_kernel(h_ref, ya_ref, yb_ref, meta_ref, g_ref, b_ref, o_ref, ob_ref, *, alpha):
    meta = meta_ref[...]
    g1 = meta[:, ROUTE_GATE:ROUTE_GATE + 1]
    g2 = meta[:, ROUTE_GATE + 1:ROUTE_GATE + 2]
    y = ya_ref[...] * g1 + yb_ref[...] * g2
    h2 = _ln_rows(alpha * h_ref[...] + y, g_ref[...], b_ref[...])
    o_ref[...] = h2
    ob_ref[...] = h2.astype(BF16)


def _combine(h, ya, yb, meta, g, b, alpha, tm=512):
    T, D = h.shape
    rows = pl.BlockSpec((tm, D), lambda i: (i, 0))
    vec = pl.BlockSpec((1, D), lambda i: (0, 0))
    return pl.pallas_call(
        functools.partial(_combine_kernel, alpha=alpha),
        out_shape=(jax.ShapeDtypeStruct((T, D), F32), jax.ShapeDtypeStruct((T, D), BF16)),
        grid=(T // tm,),
        in_specs=[rows, rows, rows, pl.BlockSpec((tm, LANES), lambda i: (i, 0)), vec, vec],
        out_specs=(rows, rows),
        compiler_params=_cparams(("parallel",)),
        name="combine_ln2",
    )(h, ya, yb, meta, g.reshape(1, D), b.reshape(1, D))


def _moe(h1, h1b, logits, w_eg, w_eu, w_ed, g, b, alpha):
    T, D = h1.shape
    meta, counts = _route(logits)
    eid = meta[:, ROUTE_EID:ROUTE_EID + 2].astype(jnp.int32)
    rank = meta[:, ROUTE_RANK:ROUTE_RANK + 2].astype(jnp.int32)
    cnt = counts[0, :N_EXPERTS].astype(jnp.int32)
    padded = (cnt + MOE_ROWS - 1) // MOE_ROWS * MOE_ROWS
    pad_end = jnp.cumsum(padded)
    pad_off = pad_end - padded
    dest = pad_off[eid] + rank
    nblk = -(-(2 * T) // MOE_ROWS) + N_EXPERTS
    P = nblk * MOE_ROWS
    tok = jnp.broadcast_to(jnp.arange(T, dtype=jnp.int32)[:, None], (T, 2))
    buf_tok = jnp.zeros((P,), jnp.int32).at[dest.reshape(-1)].set(tok.reshape(-1), unique_indices=True)
    blk_start = jnp.arange(nblk, dtype=jnp.int32) * MOE_ROWS
    blk_e = jnp.minimum(jnp.sum((pad_end[None, :] <= blk_start[:, None]).astype(jnp.int32), axis=1), N_EXPERTS - 1)
    n_used = (pad_end[-1:] // MOE_ROWS).astype(jnp.int32)
    xs = jnp.take(h1b, buf_tok, axis=0)
    y = _experts(blk_e, n_used, xs, w_eg, w_eu, w_ed)
    ya = jnp.take(y, dest[:, 0], axis=0)
    yb = jnp.take(y, dest[:, 1], axis=0)
    return _combine(h1, ya, yb, meta, g, b, alpha)


def kernel(x, ln0_g, ln0_b, rel_bias, w_in, b_gate, sink_a, rpb_c, w_br_a, w_br_b, w_br_c, w_out,
           ln1_g, ln1_b, w_rg, b_rg, w_re, b_re, w_eg, w_eu, w_ed, ln2_g, ln2_b):
    Bn, L, D = x.shape
    depth = w_in.shape[0]
    T = Bn * L
    alpha = float((2 * depth) ** 0.25)
    a_order = np.asarray(A_HEAD_ORDER)

    a_pairs = [(c, c + A_HEADS // 2) for c in range(A_HEADS // 2)]
    b_pairs = [(2 * c, 2 * c + 1) for c in range(B_HEADS_PER_GROUP // 2)]
    bias_a = _band_bias(rel_bias[:, :A_HEADS], a_pairs, A_HALF_WINDOW, 1)
    bias_b = []
    for g, (window, dilation) in enumerate(B_CONFIGS):
        c0 = A_HEADS + g * B_HEADS_PER_GROUP
        bias_b.append(_band_bias(rel_bias[:, c0:c0 + B_HEADS_PER_GROUP], b_pairs, window // (2 * dilation), dilation))

    h, hb = _layer_norm(x.reshape(T, D), ln0_g, ln0_b)
    for l in range(depth):
        w_gate = w_in[l, :, QKV_COLS:].astype(BF16)
        arr_a, arr_b0, arr_b1, arr_b2, arr_c = _in_projection(hb, _inproj_weight(w_in[l]), Bn, L)

        (o_a,) = _banded_attention(arr_a, bias_a, sink_a[l][a_order], tq=512, n_pairs=A_HEADS // 2, shared_kv=True,
                                   half=A_HALF_WINDOW, with_lse=False)
        o_bs, lses = [], []
        for g, (arr, tq) in enumerate(((arr_b0, 512), (arr_b1, 512), (arr_b2, 256))):
            window, dilation = B_CONFIGS[g]
            o, lse = _banded_attention(arr, bias_b[g], None, tq=tq, n_pairs=B_HEADS_PER_GROUP // 2, shared_kv=False,
                                       half=window // (2 * dilation), with_lse=True)
            o_bs.append(o)
            lses.append(lse)
        o_c = _neighborhood_attention(arr_c, _nbr_bias(rpb_c[l]), Bn, L)

        w_r = jnp.concatenate([w_rg[l], jnp.transpose(w_re[l], (1, 0, 2)).reshape(D, N_EXPERTS)], axis=1)
        w_r = jnp.pad(w_r, ((0, 0), (0, LANES - w_r.shape[1]))).astype(BF16)
        b_r = jnp.concatenate([b_rg[l], b_re[l].reshape(-1)]).astype(F32)
        b_r = jnp.pad(b_r, (0, LANES - b_r.shape[0])).reshape(1, LANES)
        w_a = w_br_a[l].reshape(A_HEADS, HEAD_DIM, D)[a_order].reshape(A_Q, D)

        h1, h1b, logits = _post_attention(
            h, hb, o_a.reshape(T, A_Q), o_bs, lses, o_c,
            w_gate, b_gate[l].reshape(1, N_BRANCH * D).astype(F32),
            w_a.astype(BF16), w_br_b[l].astype(BF16), w_br_c[l].astype(BF16), w_out[l].astype(BF16),
            ln1_g[l].reshape(1, D), ln1_b[l].reshape(1, D), w_r, b_r, alpha)

        h, hb = _moe(h1, h1b, logits, w_eg[l].astype(BF16), w_eu[l].astype(BF16), w_ed[l].astype(BF16),
                     ln2_g[l], ln2_b[l], alpha)
    return h.reshape(Bn, L, D)
```

```python
import functools

import numpy as np
import jax
import jax.numpy as jnp
from jax import lax
from jax.experimental import pallas as pl
from jax.experimental.pallas import tpu as pltpu

F32 = jnp.float32
BF16 = jnp.bfloat16
HIGHEST = lax.Precision.HIGHEST

HEAD_DIM = 64
A_HEADS = 8
A_KV_HEADS = 2
A_HALF_WINDOW = 128
B_HEADS_PER_GROUP = 4
B_CONFIGS = ((128, 1), (512, 4), (2048, 16))
C_HEADS = 8
GRID_W = 64
NA_KH = 8
NA_KW = 16
NUM_BUCKETS = 32
REL_MAX_DIST = 2048
N_GROUPS = 4
EXPERTS_PER_GROUP = 8
N_EXPERTS = N_GROUPS * EXPERTS_PER_GROUP
D_EXPERT = 512
LN_EPS = 1e-5
NEG_INF = -1e30

A_Q = A_HEADS * HEAD_DIM
A_KV = A_KV_HEADS * HEAD_DIM
B_W = B_HEADS_PER_GROUP * HEAD_DIM
C_W = C_HEADS * HEAD_DIM
N_BRANCH = 3
QKV_COLS = A_Q + 2 * A_KV + 3 * len(B_CONFIGS) * B_W + 3 * C_W
B_COL0 = A_Q + 2 * A_KV
C_COL0 = B_COL0 + 3 * len(B_CONFIGS) * B_W

ATT_BLK = 128
LANES = 128
MOE_ROWS = 256
VMEM_LIMIT = 56 * 1024 * 1024

A_HEAD_ORDER = tuple(h for c in range(A_HEADS // 2) for h in (c, c + A_HEADS // 2))


def _cparams(sem):
    return pltpu.CompilerParams(dimension_semantics=sem, vmem_limit_bytes=VMEM_LIMIT)


def _ln_rows(x, g, b):
    mu = jnp.mean(x, axis=-1, keepdims=True)
    xc = x - mu
    var = jnp.mean(xc * xc, axis=-1, keepdims=True)
    return xc * lax.rsqrt(var + LN_EPS) * g + b


def _ln_kernel(x_ref, g_ref, b_ref, o_ref, ob_ref):
    y = _ln_rows(x_ref[...].astype(F32), g_ref[...], b_ref[...])
    o_ref[...] = y
    ob_ref[...] = y.astype(BF16)


def _layer_norm(x, g, b, tm=512):
    T, D = x.shape
    return pl.pallas_call(
        _ln_kernel,
        out_shape=(jax.ShapeDtypeStruct((T, D), F32), jax.ShapeDtypeStruct((T, D), BF16)),
        grid=(T // tm,),
        in_specs=[pl.BlockSpec((tm, D), lambda i: (i, 0)),
                  pl.BlockSpec((1, D), lambda i: (0, 0)),
                  pl.BlockSpec((1, D), lambda i: (0, 0))],
        out_specs=(pl.BlockSpec((tm, D), lambda i: (i, 0)), pl.BlockSpec((tm, D), lambda i: (i, 0))),
        compiler_params=_cparams(("parallel",)),
        name="ln0",
    )(x, g.reshape(1, D), b.reshape(1, D))


BAND_W = 3 * B_W
NBR_W = 3 * C_W


def _inproj_weight(w_in_l):
    scale = HEAD_DIM ** -0.5
    a_q = w_in_l[:, :A_Q].reshape(-1, A_HEADS, HEAD_DIM)[:, np.asarray(A_HEAD_ORDER)].reshape(-1, A_Q)
    cols = [w_in_l[:, A_Q:A_Q + 2 * A_KV], a_q * scale]
    for g in range(len(B_CONFIGS)):
        c0 = B_COL0 + 3 * g * B_W
        cols += [w_in_l[:, c0 + B_W:c0 + 3 * B_W], w_in_l[:, c0:c0 + B_W] * scale]
    cols += [w_in_l[:, C_COL0 + C_W:C_COL0 + 3 * C_W], w_in_l[:, C_COL0:C_COL0 + C_W] * scale]
    return jnp.concatenate(cols, axis=1).astype(BF16)


def _inproj_kernel(h_ref, w_ref, a_ref, b0_ref, b1_ref, b2_ref, c_ref, tmp_ref):
    h = h_ref[...]
    tm = h.shape[0]

    def chunk(i):
        return jnp.dot(h, w_ref[:, i * BAND_W:(i + 1) * BAND_W], preferred_element_type=F32)

    a_ref[0, 0] = chunk(0).astype(BF16)
    b0_ref[0, 0] = chunk(1).astype(BF16)
    for i, ref in ((2, b1_ref), (3, b2_ref)):
        d = ref.shape[1]
        res = chunk(i)
        n_cols = BAND_W // LANES
        for c in range(n_cols):
            tmp_ref[c] = res[:, c * LANES:(c + 1) * LANES]
        for r in range(d):
            for c in range(n_cols):
                ref[0, r, :, c * LANES:(c + 1) * LANES] = tmp_ref[c, pl.ds(r, tm // d, stride=d), :].astype(BF16)
    c_ref[:, :BAND_W] = chunk(4).astype(BF16)
    c_ref[:, BAND_W:] = chunk(5).astype(BF16)


def _in_projection(hb, w, Bn, L, tm=512):
    T, D = hb.shape
    nt = L // tm
    assert L % tm == 0 and NBR_W == 2 * BAND_W and A_Q + 2 * A_KV == BAND_W
    dils = [d for _, d in B_CONFIGS]
    assert dils[0] == 1 and len(dils) == 3

    def band_spec(d):
        return pl.BlockSpec((1, d, tm // d, BAND_W), lambda b, i: (b, 0, i, 0))

    return pl.pallas_call(
        _inproj_kernel,
        out_shape=(jax.ShapeDtypeStruct((Bn, 1, L, BAND_W), BF16),
                   jax.ShapeDtypeStruct((Bn, 1, L, BAND_W), BF16),
                   jax.ShapeDtypeStruct((Bn, dils[1], L // dils[1], BAND_W), BF16),
                   jax.ShapeDtypeStruct((Bn, dils[2], L // dils[2], BAND_W), BF16),
                   jax.ShapeDtypeStruct((T, NBR_W), BF16)),
        grid=(Bn, nt),
        in_specs=[pl.BlockSpec((tm, D), lambda b, i: (b * nt + i, 0)),
                  pl.BlockSpec(w.shape, lambda b, i: (0, 0))],
        out_specs=(band_spec(1), band_spec(1), band_spec(dils[1]), band_spec(dils[2]),
                   pl.BlockSpec((tm, NBR_W), lambda b, i: (b * nt + i, 0))),
        scratch_shapes=[pltpu.VMEM((BAND_W // LANES, tm, LANES), F32)],
        compiler_params=_cparams(("parallel", "parallel")),
        name="in_proj",
    )(hb, w)


def _t5_bucket(rel):
    half = NUM_BUCKETS // 2
    max_exact = half // 2
    ret = np.where(rel > 0, half, 0)
    n = np.abs(rel)
    large = max_exact + (np.log(np.maximum(n, max_exact) / max_exact)
                         / np.log(REL_MAX_DIST / max_exact) * (half - max_exact)).astype(np.int32)
    large = np.minimum(large, half - 1)
    return (ret + np.where(n < max_exact, n, large)).astype(np.int32)


def _one_hot_t(idx, n):
    return jnp.asarray((np.arange(n)[:, None] == np.asarray(idx).reshape(1, -1)).astype(np.float32))


def _band_bias(bias_tab, head_pairs, half, dist_scale):
    kl = ATT_BLK + 2 * half
    off = np.arange(kl)[None, :] - half - np.arange(ATT_BLK)[:, None]
    band = np.abs(off) <= half
    onehot = _one_hot_t(_t5_bucket(off * dist_scale), NUM_BUCKETS)
    order = np.asarray([h for pair in head_pairs for h in pair])
    bias = jnp.dot(bias_tab.astype(F32)[:, order].T, onehot, precision=HIGHEST)
    bias = bias.reshape(len(head_pairs), 2 * ATT_BLK, kl)
    col = np.arange(kl)
    first_ok = (col >= half)[None, :]
    last_ok = (col < ATT_BLK + half)[None, :]
    out = []
    for first, last in ((0, 0), (1, 0), (0, 1), (1, 1)):
        ok = band & (first_ok if first else True) & (last_ok if last else True)
        ok = np.concatenate([ok, ok], axis=0)
        out.append(jnp.where(jnp.asarray(ok)[None], bias, NEG_INF))
    return jnp.stack(out)


def _pair_rows(x):
    lo = lax.broadcasted_iota(jnp.int32, x.shape, 1) < HEAD_DIM
    zero = jnp.zeros_like(x)
    return jnp.concatenate([jnp.where(lo, x, zero), jnp.where(lo, zero, x)], axis=0)


def _unpair_rows(x2):
    rows = x2.shape[0] // 2
    lo = lax.broadcasted_iota(jnp.int32, (rows, LANES), 1) < HEAD_DIM
    return jnp.where(lo, x2[:rows], x2[rows:])


def _banded_kernel(*refs, n_sub, sub_per_iter, half, kv_cols, n_pairs, shared_kv, has_sink, with_lse, dilation):
    cur_ref, prev_ref, next_ref, bias_ref = refs[:4]
    pos = 4
    sink_ref = None
    if has_sink:
        sink_ref = refs[pos]
        pos += 1
    o_ref = refs[pos]
    pos += 1
    lse_ref = None
    if with_lse:
        lse_ref = refs[pos]
        pos += 1
    kv_s, s_s, p_s = refs[pos:pos + 3]

    tq = n_sub * ATT_BLK
    kl = ATT_BLK + 2 * half
    kvw = kv_cols * LANES
    d = dilation
    j = pl.program_id(1)
    last_j = pl.num_programs(1) - 1
    row_is_lo = lax.broadcasted_iota(jnp.int32, (2 * ATT_BLK, 1), 0) < ATT_BLK

    def one_residue(r, carry):
        kv_s[0:half, :] = prev_ref[0, r]
        kv_s[half:half + tq, :] = cur_ref[0, r, :, :2 * kvw]
        kv_s[half + tq:, :] = next_ref[0, r]

        def sub_blocks(it, carry2):
            units = [(ii, c) for ii in range(sub_per_iter) for c in range(n_pairs)]
            row0s, variants = [], []
            for ii in range(sub_per_iter):
                i = it * sub_per_iter + ii
                row0s.append(pl.multiple_of(i * ATT_BLK, ATT_BLK))
                is_first = jnp.logical_and(i == 0, j == 0).astype(jnp.int32)
                is_last = jnp.logical_and(i == n_sub - 1, j == last_j).astype(jnp.int32)
                variants.append(is_first + 2 * is_last)

            for u, (ii, c) in enumerate(units):
                kc = 0 if shared_kv else c
                q2 = _pair_rows(cur_ref[0, r, pl.ds(row0s[ii], ATT_BLK),
                                        2 * kvw + c * LANES:2 * kvw + (c + 1) * LANES])
                k = kv_s[pl.ds(row0s[ii], kl), kc * LANES:(kc + 1) * LANES]
                s_s[u] = lax.dot_general(q2, k, (((1,), (1,)), ((), ())), preferred_element_type=F32)

            stats = []
            for u, (ii, c) in enumerate(units):
                s = s_s[u] + bias_ref[variants[ii], c]
                m = jnp.max(s, axis=-1, keepdims=True)
                if has_sink:
                    sk = jnp.where(row_is_lo, sink_ref[2 * c], sink_ref[2 * c + 1])
                    m = jnp.maximum(m, sk)
                p = jnp.exp(s - m)
                denom = jnp.sum(p, axis=-1, keepdims=True)
                if has_sink:
                    denom = denom + jnp.exp(sk - m)
                p_s[u] = p.astype(BF16)
                stats.append((m, denom))

            for ii in range(sub_per_iter):
                o_cols, lse_cols = [], []
                for c in range(n_pairs):
                    u = ii * n_pairs + c
                    m, denom = stats[u]
                    kc = 0 if shared_kv else c
                    v = kv_s[pl.ds(row0s[ii], kl), kvw + kc * LANES:kvw + (kc + 1) * LANES]
                    o2 = jnp.dot(p_s[u], v, preferred_element_type=F32) / denom
                    o_cols.append(_unpair_rows(o2))
                    if with_lse:
                        lse_cols.append(_unpair_rows(jnp.broadcast_to(m + jnp.log(denom), (2 * ATT_BLK, LANES))))
                if d == 1:
                    rows = pl.ds(row0s[ii], ATT_BLK)
                else:
                    rows = pl.ds(row0s[ii] * d + r, ATT_BLK, stride=d)
                if with_lse:
                    for c in range(n_pairs):
                        o_ref[0, c, rows, :] = o_cols[c].astype(o_ref.dtype)
                        lse_ref[0, c, rows, :] = lse_cols[c]
                else:
                    o_ref[0, rows, :] = jnp.concatenate(o_cols, axis=-1).astype(o_ref.dtype)
            return carry2

        lax.fori_loop(0, n_sub // sub_per_iter, sub_blocks, 0)
        return carry

    lax.fori_loop(0, d, one_residue, 0)


def _banded_attention(arr, bias, sink, *, tq, n_pairs, shared_kv, half, with_lse):
    Bn, d, Ls, W = arr.shape
    L = d * Ls
    qw = n_pairs * LANES
    kv_cols = (W - qw) // (2 * LANES)
    kvw = kv_cols * LANES
    tq = min(tq, Ls)
    n_sub = tq // ATT_BLK
    nbt = Ls // tq
    sub_per_iter = 2 if (n_pairs <= 2 and n_sub % 2 == 0) else 1
    n_units = sub_per_iter * n_pairs
    halo_per_tile = tq // half
    n_halo = Ls // half
    assert Ls % tq == 0 and tq % half == 0 and kvw * 2 + qw == W
    o_dtype = BF16 if d == 1 else F32

    in_specs = [
        pl.BlockSpec((1, d, tq, W), lambda b, j: (b, 0, j, 0)),
        pl.BlockSpec((1, d, half, 2 * kvw), lambda b, j: (b, 0, jnp.maximum(j * halo_per_tile - 1, 0), 0)),
        pl.BlockSpec((1, d, half, 2 * kvw),
                     lambda b, j: (b, 0, jnp.minimum((j + 1) * halo_per_tile, n_halo - 1), 0)),
        pl.BlockSpec(bias.shape, lambda b, j: (0, 0, 0, 0)),
    ]
    args = [arr, arr, arr, bias]
    if sink is not None:
        in_specs.append(pl.BlockSpec(memory_space=pltpu.SMEM))
        args.append(sink.astype(F32))
    if with_lse:
        out_block = pl.BlockSpec((1, n_pairs, tq * d, LANES), lambda b, j: (b, 0, j, 0))
        out_shape = [jax.ShapeDtypeStruct((Bn, n_pairs, L, LANES), o_dtype),
                     jax.ShapeDtypeStruct((Bn, n_pairs, L, LANES), F32)]
        out_specs = [out_block, out_block]
    else:
        assert d == 1
        out_shape = [jax.ShapeDtypeStruct((Bn, L, qw), o_dtype)]
        out_specs = [pl.BlockSpec((1, tq, qw), lambda b, j: (b, j, 0))]
    return pl.pallas_call(
        functools.partial(_banded_kernel, n_sub=n_sub, sub_per_iter=sub_per_iter, half=half, kv_cols=kv_cols,
                          n_pairs=n_pairs, shared_kv=shared_kv, has_sink=sink is not None, with_lse=with_lse,
                          dilation=d),
        out_shape=tuple(out_shape),
        grid=(Bn, nbt),
        in_specs=in_specs,
        out_specs=tuple(out_specs),
        scratch_shapes=[pltpu.VMEM((tq + 2 * half, 2 * kvw), BF16),
                        pltpu.VMEM((n_units, 2 * ATT_BLK, ATT_BLK + 2 * half), F32),
                        pltpu.VMEM((n_units, 2 * ATT_BLK, ATT_BLK + 2 * half), BF16)],
        compiler_params=_cparams(("parallel", "arbitrary")),
        name=f"banded_attn_d{d}_h{half}",
    )(*args)


NBR_ROWS_PER_STEP = 8
NBR_KEY_BLOCK_ROWS = 4
NBR_ROWS_PER_ITER = 2


def _nbr_bias(rpb):
    H = rpb.shape[0]
    c = np.arange(GRID_W)
    qstart = np.clip(c - NA_KW // 2, 0, GRID_W - NA_KW)
    kc = np.arange(GRID_W)
    valid = (kc[None, :] >= qstart[:, None]) & (kc[None, :] < qstart[:, None] + NA_KW)
    cidx = np.clip(kc[None, :] - c[:, None] + NA_KW - 1, 0, 2 * NA_KW - 2)
    rows = jnp.stack([rpb.astype(F32)[:, NA_KH - 1 - v:2 * NA_KH - 1 - v] for v in range(NA_KH)])
    col_sel = _one_hot_t(cidx, 2 * NA_KW - 1)
    tab = jnp.dot(rows.reshape(-1, 2 * NA_KW - 1), col_sel, precision=HIGHEST)
    tab = tab.reshape(NA_KH, H, NA_KH, GRID_W, GRID_W).transpose(0, 1, 3, 2, 4)
    tab = jnp.where(jnp.asarray(valid)[None, None, :, None, :], tab, NEG_INF)
    return tab.reshape(NA_KH, H // 2, 2 * GRID_W, NA_KH * GRID_W)


def _nbr_kernel(q_ref, kv0, kv1, kv2, kv3, bias_ref, o_ref, kv_s, s_s, p_s, *, rows):
    blk = NBR_KEY_BLOCK_ROWS * GRID_W
    for t, ref in enumerate((kv0, kv1, kv2, kv3)):
        kv_s[t * blk:(t + 1) * blk, :] = ref[...]
    i0 = pl.program_id(1) * NBR_ROWS_PER_STEP
    nkeys = NA_KH * GRID_W
    n_pairs = C_HEADS // 2

    def row_group(it, carry):
        offs, variants, q0s = [], [], []
        for rr in range(NBR_ROWS_PER_ITER):
            r = it * NBR_ROWS_PER_ITER + rr
            i = i0 + r
            rstart = jnp.clip(i - NA_KH // 2, 0, rows - NA_KH)
            offs.append(pl.multiple_of((rstart - i0 + NBR_KEY_BLOCK_ROWS) * GRID_W, GRID_W))
            variants.append(i - rstart)
            q0s.append(pl.multiple_of(r * GRID_W, GRID_W))
        units = [(rr, c) for rr in range(NBR_ROWS_PER_ITER) for c in range(n_pairs)]

        for u, (rr, c) in enumerate(units):
            q2 = _pair_rows(q_ref[pl.ds(q0s[rr], GRID_W), c * LANES:(c + 1) * LANES])
            k = kv_s[pl.ds(offs[rr], nkeys), c * LANES:(c + 1) * LANES]
            s_s[u] = lax.dot_general(q2, k, (((1,), (1,)), ((), ())), preferred_element_type=F32)

        denoms = []
        for u, (rr, c) in enumerate(units):
            s = s_s[u] + bias_ref[variants[rr], c]
            m = jnp.max(s, axis=-1, keepdims=True)
            p = jnp.exp(s - m)
            denoms.append(jnp.sum(p, axis=-1, keepdims=True))
            p_s[u] = p.astype(BF16)

        for rr in range(NBR_ROWS_PER_ITER):
            o_cols = []
            for c in range(n_pairs):
                u = rr * n_pairs + c
                v = kv_s[pl.ds(offs[rr], nkeys), C_W + c * LANES:C_W + (c + 1) * LANES]
                o2 = jnp.dot(p_s[u], v, preferred_element_type=F32) / denoms[u]
                o_cols.append(_unpair_rows(o2))
            o_ref[pl.ds(q0s[rr], GRID_W), :] = jnp.concatenate(o_cols, axis=-1).astype(o_ref.dtype)
        return carry

    lax.fori_loop(0, NBR_ROWS_PER_STEP // NBR_ROWS_PER_ITER, row_group, 0)


def _neighborhood_attention(arr, bias, Bn, L):
    rows = L // GRID_W
    assert rows % NBR_ROWS_PER_STEP == 0 and rows >= 2 * NBR_ROWS_PER_STEP
    tq = NBR_ROWS_PER_STEP * GRID_W
    kb = NBR_KEY_BLOCK_ROWS * GRID_W
    nkb = L // kb
    per_step = NBR_ROWS_PER_STEP // NBR_KEY_BLOCK_ROWS
    nq = L // tq

    def key_spec(t):
        return pl.BlockSpec((kb, 2 * C_W), lambda b, j: (b * nkb + jnp.clip(j * per_step - 1 + t, 0, nkb - 1), 0))

    in_specs = ([pl.BlockSpec((tq, C_W), lambda b, j: (b * nq + j, 2))]
                + [key_spec(t) for t in range(4)]
                + [pl.BlockSpec(bias.shape, lambda b, j: (0, 0, 0, 0))])
    return pl.pallas_call(
        functools.partial(_nbr_kernel, rows=rows),
        out_shape=jax.ShapeDtypeStruct((Bn * L, C_W), BF16),
        grid=(Bn, nq),
        in_specs=in_specs,
        out_specs=pl.BlockSpec((tq, C_W), lambda b, j: (b * nq + j, 0)),
        scratch_shapes=[pltpu.VMEM((4 * kb, 2 * C_W), BF16),
                        pltpu.VMEM((NBR_ROWS_PER_ITER * C_HEADS // 2, 2 * GRID_W, NA_KH * GRID_W), F32),
                        pltpu.VMEM((NBR_ROWS_PER_ITER * C_HEADS // 2, 2 * GRID_W, NA_KH * GRID_W), BF16)],
        compiler_params=_cparams(("parallel", "arbitrary")),
        name="nbr_attn",
    )(*([arr] * 5), bias)


def _post_kernel(h_ref, hb_ref, oa_ref, ob0_ref, ob1_ref, ob2_ref, l0_ref, l1_ref, l2_ref, oc_ref,
                 wg_ref, bg_ref, wa_ref, wb_ref, wc_ref, wo_ref, g_ref, b_ref, wr_ref, br_ref,
                 o_ref, obf_ref, logit_ref, *, alpha):
    D = h_ref.shape[1]
    hb = hb_ref[...]
    def planes(ref):
        return jnp.concatenate([ref[0, c] for c in range(ref.shape[1])], axis=-1).astype(F32)

    l0, l1, l2 = planes(l0_ref), planes(l1_ref), planes(l2_ref)
    lm = jnp.maximum(jnp.maximum(l0, l1), l2)
    e0, e1, e2 = jnp.exp(l0 - lm), jnp.exp(l1 - lm), jnp.exp(l2 - lm)
    inv = 1.0 / (e0 + e1 + e2)
    o_b = ((e0 * planes(ob0_ref) + e1 * planes(ob1_ref) + e2 * planes(ob2_ref)) * inv).astype(BF16)
    merged = None
    for idx, (o_br, w_ref) in enumerate(((oa_ref[...], wa_ref), (o_b, wb_ref), (oc_ref[...], wc_ref))):
        z = (jnp.dot(hb, wg_ref[:, idx * D:(idx + 1) * D], preferred_element_type=F32)
             + bg_ref[:, idx * D:(idx + 1) * D])
        gate = 0.5 * jnp.tanh(0.5 * z) + 0.5
        term = gate * jnp.dot(o_br, w_ref[...], preferred_element_type=F32)
        merged = term if merged is None else merged + term
    y = jnp.dot(merged.astype(BF16), wo_ref[...], preferred_element_type=F32)
    h1 = _ln_rows(alpha * h_ref[...] + y, g_ref[...], b_ref[...])
    h1b = h1.astype(BF16)
    o_ref[...] = h1
    obf_ref[...] = h1b
    logit_ref[...] = jnp.dot(h1b, wr_ref[...], preferred_element_type=F32) + br_ref[...]


def _post_attention(h, hb, o_a, o_bs, lses, o_c, wg, bg, wa, wb, wc, wo, g, b, wr, br, alpha, tm=256):
    T, D = h.shape
    n_planes, L = o_bs[0].shape[1:3]
    nt = L // tm
    assert L % tm == 0

    def rows(w):
        return pl.BlockSpec((tm, w), lambda i: (i, 0))

    plane_rows = pl.BlockSpec((1, n_planes, tm, LANES), lambda i: (i // nt, 0, i % nt, 0))

    def full(a):
        return pl.BlockSpec(a.shape, lambda i: (0, 0))

    weights = [wg, bg, wa, wb, wc, wo, g, b, wr, br]
    return pl.pallas_call(
        functools.partial(_post_kernel, alpha=alpha),
        out_shape=(jax.ShapeDtypeStruct((T, D), F32), jax.ShapeDtypeStruct((T, D), BF16),
                   jax.ShapeDtypeStruct((T, LANES), F32)),
        grid=(T // tm,),
        in_specs=[rows(D), rows(D), rows(A_Q)] + [plane_rows] * 6 + [rows(C_W)] + [full(w) for w in weights],
        out_specs=(rows(D), rows(D), rows(LANES)),
        compiler_params=_cparams(("parallel",)),
        name="post_attn",
    )(h, hb, o_a, *o_bs, *lses, o_c, *weights)


ROUTE_EID, ROUTE_GATE, ROUTE_RANK = 0, 2, 4


def _route_kernel(logit_ref, meta_ref, count_ref, run_ref):
    tm = logit_ref.shape[0]

    @pl.when(pl.program_id(0) == 0)
    def _():
        run_ref[...] = jnp.zeros_like(run_ref)

    x = logit_ref[...]
    lane = lax.broadcasted_iota(jnp.int32, x.shape, 1).astype(F32)
    big = float(LANES)

    def lane_max(v):
        return jnp.max(v, axis=-1, keepdims=True)

    def first_lane(mask):
        return jnp.min(jnp.where(mask, lane, big), axis=-1, keepdims=True)

    is_g = lane < N_GROUPS
    gl = jnp.where(is_g, x, NEG_INF)
    gmax = lane_max(gl)
    g_idx = first_lane(jnp.logical_and(is_g, gl == gmax))
    g_w = 1.0 / jnp.sum(jnp.where(is_g, jnp.exp(gl - gmax), 0.0), axis=-1, keepdims=True)

    e_lo = N_GROUPS + g_idx * EXPERTS_PER_GROUP
    in_grp = jnp.logical_and(lane >= e_lo, lane < e_lo + EXPERTS_PER_GROUP)
    el = jnp.where(in_grp, x, NEG_INF)
    top1 = lane_max(el)
    lane1 = first_lane(jnp.logical_and(in_grp, el == top1))
    el2 = jnp.where(lane == lane1, NEG_INF, el)
    top2 = lane_max(el2)
    lane2 = first_lane(jnp.logical_and(in_grp, jnp.logical_and(lane != lane1, el2 == top2)))
    e2 = jnp.exp(top2 - top1)
    w1 = 1.0 / (1.0 + e2)
    w2 = e2 / (1.0 + e2)

    eid1 = lane1 - N_GROUPS
    eid2 = lane2 - N_GROUPS
    onehot = jnp.logical_or(lane == eid1, lane == eid2)
    oh = jnp.where(onehot, 1.0, 0.0).astype(BF16)
    r_i = lax.broadcasted_iota(jnp.int32, (tm, tm), 0)
    c_i = lax.broadcasted_iota(jnp.int32, (tm, tm), 1)
    strict_lower = jnp.where(c_i < r_i, 1.0, 0.0).astype(BF16)
    before = jnp.dot(strict_lower, oh, preferred_element_type=F32) + run_ref[...]
    rank1 = jnp.sum(jnp.where(lane == eid1, before, 0.0), axis=-1, keepdims=True)
    rank2 = jnp.sum(jnp.where(lane == eid2, before, 0.0), axis=-1, keepdims=True)
    run_ref[...] = run_ref[...] + jnp.sum(oh.astype(F32), axis=0, keepdims=True)

    meta = jnp.zeros(x.shape, F32)
    for k, val in ((ROUTE_EID, eid1), (ROUTE_EID + 1, eid2),
                   (ROUTE_GATE, g_w * w1), (ROUTE_GATE + 1, g_w * w2),
                   (ROUTE_RANK, rank1), (ROUTE_RANK + 1, rank2)):
        meta = jnp.where(lane == k, val, meta)
    meta_ref[...] = meta
    count_ref[...] = run_ref[...]


def _route(logits, tm=512):
    T = logits.shape[0]
    return pl.pallas_call(
        _route_kernel,
        out_shape=(jax.ShapeDtypeStruct((T, LANES), F32), jax.ShapeDtypeStruct((1, LANES), F32)),
        grid=(T // tm,),
        in_specs=[pl.BlockSpec((tm, LANES), lambda i: (i, 0))],
        out_specs=(pl.BlockSpec((tm, LANES), lambda i: (i, 0)), pl.BlockSpec((1, LANES), lambda i: (0, 0))),
        scratch_shapes=[pltpu.VMEM((1, LANES), F32)],
        compiler_params=_cparams(("arbitrary",)),
        name="route",
    )(logits)


def _expert_kernel(blk_e_ref, n_used_ref, x_ref, wg_ref, wu_ref, wd_ref, o_ref):
    b = pl.program_id(0)

    @pl.when(b < n_used_ref[0])
    def _():
        x = x_ref[...]
        a = jnp.dot(x, wg_ref[0], preferred_element_type=F32)
        u = jnp.dot(x, wu_ref[0], preferred_element_type=F32)
        hid = (a * jax.nn.sigmoid(a) * u).astype(BF16)
        o_ref[...] = jnp.dot(hid, wd_ref[0], preferred_element_type=F32)

    @pl.when(b >= n_used_ref[0])
    def _():
        o_ref[...] = jnp.zeros_like(o_ref)


def _experts(blk_e, n_used, xs, wg, wu, wd):
    P, D = xs.shape
    nblk = P // MOE_ROWS
    de = wg.shape[2]
    grid_spec = pltpu.PrefetchScalarGridSpec(
        num_scalar_prefetch=2,
        grid=(nblk,),
        in_specs=[pl.BlockSpec((MOE_ROWS, D), lambda b, be, nu: (b, 0)),
                  pl.BlockSpec((1, D, de), lambda b, be, nu: (be[b], 0, 0)),
                  pl.BlockSpec((1, D, de), lambda b, be, nu: (be[b], 0, 0)),
                  pl.BlockSpec((1, de, D), lambda b, be, nu: (be[b], 0, 0))],
        out_specs=pl.BlockSpec((MOE_ROWS, D), lambda b, be, nu: (b, 0)),
    )
    return pl.pallas_call(
        _expert_kernel,
        out_shape=jax.ShapeDtypeStruct((P, D), F32),
        grid_spec=grid_spec,
        compiler_params=_cparams(("arbitrary",)),
        name="experts",
    )(blk_e, n_used, xs, wg, wu, wd)


def _combine_kernel(h_ref, ya_ref, yb_ref, meta_ref, g_ref, b_ref, o_ref, ob_ref, *, alpha):
    meta = meta_ref[...]
    g1 = meta[:, ROUTE_GATE:ROUTE_GATE + 1]
    g2 = meta[:, ROUTE_GATE + 1:ROUTE_GATE + 2]
    y = ya_ref[...] * g1 + yb_ref[...] * g2
    h2 = _ln_rows(alpha * h_ref[...] + y, g_ref[...], b_ref[...])
    o_ref[...] = h2
    ob_ref[...] = h2.astype(BF16)


def _combine(h, ya, yb, meta, g, b, alpha, tm=512):
    T, D = h.shape
    rows = pl.BlockSpec((tm, D), lambda i: (i, 0))
    vec = pl.BlockSpec((1, D), lambda i: (0, 0))
    return pl.pallas_call(
        functools.partial(_combine_kernel, alpha=alpha),
        out_shape=(jax.ShapeDtypeStruct((T, D), F32), jax.ShapeDtypeStruct((T, D), BF16)),
        grid=(T // tm,),
        in_specs=[rows, rows, rows, pl.BlockSpec((tm, LANES), lambda i: (i, 0)), vec, vec],
        out_specs=(rows, rows),
        compiler_params=_cparams(("parallel",)),
        name="combine_ln2",
    )(h, ya, yb, meta, g.reshape(1, D), b.reshape(1, D))


def _moe(h1, h1b, logits, w_eg, w_eu, w_ed, g, b, alpha):
    T, D = h1.shape
    meta, counts = _route(logits)
    eid = meta[:, ROUTE_EID:ROUTE_EID + 2].astype(jnp.int32)
    rank = meta[:, ROUTE_RANK:ROUTE_RANK + 2].astype(jnp.int32)
    cnt = counts[0, :N_EXPERTS].astype(jnp.int32)
    padded = (cnt + MOE_ROWS - 1) // MOE_ROWS * MOE_ROWS
    pad_end = jnp.cumsum(padded)
    pad_off = pad_end - padded
    dest = pad_off[eid] + rank
    nblk = -(-(2 * T) // MOE_ROWS) + N_EXPERTS
    P = nblk * MOE_ROWS
    tok = jnp.broadcast_to(jnp.arange(T, dtype=jnp.int32)[:, None], (T, 2))
    buf_tok = jnp.zeros((P,), jnp.int32).at[dest.reshape(-1)].set(tok.reshape(-1), unique_indices=True)
    blk_start = jnp.arange(nblk, dtype=jnp.int32) * MOE_ROWS
    blk_e = jnp.minimum(jnp.sum((pad_end[None, :] <= blk_start[:, None]).astype(jnp.int32), axis=1), N_EXPERTS - 1)
    n_used = (pad_end[-1:] // MOE_ROWS).astype(jnp.int32)
    xs = h1b.at[buf_tok].get(mode='promise_in_bounds')
    y = _experts(blk_e, n_used, xs, w_eg, w_eu, w_ed)
    ya = y.at[dest[:, 0]].get(mode='promise_in_bounds')
    yb = y.at[dest[:, 1]].get(mode='promise_in_bounds')
    return _combine(h1, ya, yb, meta, g, b, alpha)


def kernel(x, ln0_g, ln0_b, rel_bias, w_in, b_gate, sink_a, rpb_c, w_br_a, w_br_b, w_br_c, w_out,
           ln1_g, ln1_b, w_rg, b_rg, w_re, b_re, w_eg, w_eu, w_ed, ln2_g, ln2_b):
    Bn, L, D = x.shape
    depth = w_in.shape[0]
    T = Bn * L
    alpha = float((2 * depth) ** 0.25)
    a_order = np.asarray(A_HEAD_ORDER)

    a_pairs = [(c, c + A_HEADS // 2) for c in range(A_HEADS // 2)]
    b_pairs = [(2 * c, 2 * c + 1) for c in range(B_HEADS_PER_GROUP // 2)]
    bias_a = _band_bias(rel_bias[:, :A_HEADS], a_pairs, A_HALF_WINDOW, 1)
    bias_b = []
    for g, (window, dilation) in enumerate(B_CONFIGS):
        c0 = A_HEADS + g * B_HEADS_PER_GROUP
        bias_b.append(_band_bias(rel_bias[:, c0:c0 + B_HEADS_PER_GROUP], b_pairs, window // (2 * dilation), dilation))

    h, hb = _layer_norm(x.reshape(T, D), ln0_g, ln0_b)
    for l in range(depth):
        w_gate = w_in[l, :, QKV_COLS:].astype(BF16)
        arr_a, arr_b0, arr_b1, arr_b2, arr_c = _in_projection(hb, _inproj_weight(w_in[l]), Bn, L)

        (o_a,) = _banded_attention(arr_a, bias_a, sink_a[l][a_order], tq=512, n_pairs=A_HEADS // 2, shared_kv=True,
                                   half=A_HALF_WINDOW, with_lse=False)
        o_bs, lses = [], []
        for g, (arr, tq) in enumerate(((arr_b0, 512), (arr_b1, 512), (arr_b2, 256))):
            window, dilation = B_CONFIGS[g]
            o, lse = _banded_attention(arr, bias_b[g], None, tq=tq, n_pairs=B_HEADS_PER_GROUP // 2, shared_kv=False,
                                       half=window // (2 * dilation), with_lse=True)
            o_bs.append(o)
            lses.append(lse)
        o_c = _neighborhood_attention(arr_c, _nbr_bias(rpb_c[l]), Bn, L)

        w_r = jnp.concatenate([w_rg[l], jnp.transpose(w_re[l], (1, 0, 2)).reshape(D, N_EXPERTS)], axis=1)
        w_r = jnp.pad(w_r, ((0, 0), (0, LANES - w_r.shape[1]))).astype(BF16)
        b_r = jnp.concatenate([b_rg[l], b_re[l].reshape(-1)]).astype(F32)
        b_r = jnp.pad(b_r, (0, LANES - b_r.shape[0])).reshape(1, LANES)
        w_a = w_br_a[l].reshape(A_HEADS, HEAD_DIM, D)[a_order].reshape(A_Q, D)

        h1, h1b, logits = _post_attention(
            h, hb, o_a.reshape(T, A_Q), o_bs, lses, o_c,
            w_gate, b_gate[l].reshape(1, N_BRANCH * D).astype(F32),
            w_a.astype(BF16), w_br_b[l].astype(BF16), w_br_c[l].astype(BF16), w_out[l].astype(BF16),
            ln1_g[l].reshape(1, D), ln1_b[l].reshape(1, D), w_r, b_r, alpha)

        h, hb = _moe(h1, h1b, logits, w_eg[l].astype(BF16), w_eu[l].astype(BF16), w_ed[l].astype(BF16),
                     ln2_g[l], ln2_b[l], alpha)
    return h.reshape(Bn, L, D)
```

```python
import functools

import numpy as np
import jax
import jax.numpy as jnp
from jax import lax
from jax.experimental import pallas as pl
from jax.experimental.pallas import tpu as pltpu

F32 = jnp.float32
BF16 = jnp.bfloat16
HIGHEST = lax.Precision.HIGHEST

HEAD_DIM = 64
A_HEADS = 8
A_KV_HEADS = 2
A_HALF_WINDOW = 128
B_HEADS_PER_GROUP = 4
B_CONFIGS = ((128, 1), (512, 4), (2048, 16))
C_HEADS = 8
GRID_W = 64
NA_KH = 8
NA_KW = 16
NUM_BUCKETS = 32
REL_MAX_DIST = 2048
N_GROUPS = 4
EXPERTS_PER_GROUP = 8
N_EXPERTS = N_GROUPS * EXPERTS_PER_GROUP
D_EXPERT = 512
LN_EPS = 1e-5
NEG_INF = -1e30

A_Q = A_HEADS * HEAD_DIM
A_KV = A_KV_HEADS * HEAD_DIM
B_W = B_HEADS_PER_GROUP * HEAD_DIM
C_W = C_HEADS * HEAD_DIM
N_BRANCH = 3
QKV_COLS = A_Q + 2 * A_KV + 3 * len(B_CONFIGS) * B_W + 3 * C_W
B_COL0 = A_Q + 2 * A_KV
C_COL0 = B_COL0 + 3 * len(B_CONFIGS) * B_W

ATT_BLK = 128
LANES = 128
MOE_ROWS = 256
MOE_CHUNKS = 4
VMEM_LIMIT = 56 * 1024 * 1024

A_HEAD_ORDER = tuple(h for c in range(A_HEADS // 2) for h in (c, c + A_HEADS // 2))


def _cparams(sem):
    return pltpu.CompilerParams(dimension_semantics=sem, vmem_limit_bytes=VMEM_LIMIT)


def _ln_rows(x, g, b):
    mu = jnp.mean(x, axis=-1, keepdims=True)
    xc = x - mu
    var = jnp.mean(xc * xc, axis=-1, keepdims=True)
    return xc * lax.rsqrt(var + LN_EPS) * g + b


def _ln_kernel(x_ref, g_ref, b_ref, o_ref, ob_ref):
    y = _ln_rows(x_ref[...].astype(F32), g_ref[...], b_ref[...])
    o_ref[...] = y
    ob_ref[...] = y.astype(BF16)


def _layer_norm(x, g, b, tm=512):
    T, D = x.shape
    return pl.pallas_call(
        _ln_kernel,
        out_shape=(jax.ShapeDtypeStruct((T, D), F32), jax.ShapeDtypeStruct((T, D), BF16)),
        grid=(T // tm,),
        in_specs=[pl.BlockSpec((tm, D), lambda i: (i, 0)),
                  pl.BlockSpec((1, D), lambda i: (0, 0)),
                  pl.BlockSpec((1, D), lambda i: (0, 0))],
        out_specs=(pl.BlockSpec((tm, D), lambda i: (i, 0)), pl.BlockSpec((tm, D), lambda i: (i, 0))),
        compiler_params=_cparams(("parallel",)),
        name="ln0",
    )(x, g.reshape(1, D), b.reshape(1, D))


BAND_W = 3 * B_W
NBR_W = 3 * C_W


def _inproj_weight(w_in_l):
    scale = HEAD_DIM ** -0.5
    a_q = w_in_l[:, :A_Q].reshape(-1, A_HEADS, HEAD_DIM)[:, np.asarray(A_HEAD_ORDER)].reshape(-1, A_Q)
    cols = [w_in_l[:, A_Q:A_Q + 2 * A_KV], a_q * scale]
    for g in range(len(B_CONFIGS)):
        c0 = B_COL0 + 3 * g * B_W
        cols += [w_in_l[:, c0 + B_W:c0 + 3 * B_W], w_in_l[:, c0:c0 + B_W] * scale]
    cols += [w_in_l[:, C_COL0 + C_W:C_COL0 + 3 * C_W], w_in_l[:, C_COL0:C_COL0 + C_W] * scale]
    return jnp.concatenate(cols, axis=1).astype(BF16)


def _inproj_kernel(h_ref, w_ref, a_ref, b0_ref, b1_ref, b2_ref, c_ref, tmp_ref):
    h = h_ref[...]
    tm = h.shape[0]

    def chunk(i):
        return jnp.dot(h, w_ref[:, i * BAND_W:(i + 1) * BAND_W], preferred_element_type=F32)

    a_ref[0, 0] = chunk(0).astype(BF16)
    b0_ref[0, 0] = chunk(1).astype(BF16)
    for i, ref in ((2, b1_ref), (3, b2_ref)):
        d = ref.shape[1]
        res = chunk(i)
        n_cols = BAND_W // LANES
        for c in range(n_cols):
            tmp_ref[c] = res[:, c * LANES:(c + 1) * LANES]
        for r in range(d):
            for c in range(n_cols):
                ref[0, r, :, c * LANES:(c + 1) * LANES] = tmp_ref[c, pl.ds(r, tm // d, stride=d), :].astype(BF16)
    c_ref[:, :BAND_W] = chunk(4).astype(BF16)
    c_ref[:, BAND_W:] = chunk(5).astype(BF16)


def _in_projection(hb, w, Bn, L, tm=512):
    T, D = hb.shape
    nt = L // tm
    assert L % tm == 0 and NBR_W == 2 * BAND_W and A_Q + 2 * A_KV == BAND_W
    dils = [d for _, d in B_CONFIGS]
    assert dils[0] == 1 and len(dils) == 3

    def band_spec(d):
        return pl.BlockSpec((1, d, tm // d, BAND_W), lambda b, i: (b, 0, i, 0))

    return pl.pallas_call(
        _inproj_kernel,
        out_shape=(jax.ShapeDtypeStruct((Bn, 1, L, BAND_W), BF16),
                   jax.ShapeDtypeStruct((Bn, 1, L, BAND_W), BF16),
                   jax.ShapeDtypeStruct((Bn, dils[1], L // dils[1], BAND_W), BF16),
                   jax.ShapeDtypeStruct((Bn, dils[2], L // dils[2], BAND_W), BF16),
                   jax.ShapeDtypeStruct((T, NBR_W), BF16)),
        grid=(Bn, nt),
        in_specs=[pl.BlockSpec((tm, D), lambda b, i: (b * nt + i, 0)),
                  pl.BlockSpec(w.shape, lambda b, i: (0, 0))],
        out_specs=(band_spec(1), band_spec(1), band_spec(dils[1]), band_spec(dils[2]),
                   pl.BlockSpec((tm, NBR_W), lambda b, i: (b * nt + i, 0))),
        scratch_shapes=[pltpu.VMEM((BAND_W // LANES, tm, LANES), F32)],
        compiler_params=_cparams(("parallel", "parallel")),
        name="in_proj",
    )(hb, w)


def _t5_bucket(rel):
    half = NUM_BUCKETS // 2
    max_exact = half // 2
    ret = np.where(rel > 0, half, 0)
    n = np.abs(rel)
    large = max_exact + (np.log(np.maximum(n, max_exact) / max_exact)
                         / np.log(REL_MAX_DIST / max_exact) * (half - max_exact)).astype(np.int32)
    large = np.minimum(large, half - 1)
    return (ret + np.where(n < max_exact, n, large)).astype(np.int32)


def _one_hot_t(idx, n):
    return jnp.asarray((np.arange(n)[:, None] == np.asarray(idx).reshape(1, -1)).astype(np.float32))


def _band_bias(bias_tab, head_pairs, half, dist_scale):
    kl = ATT_BLK + 2 * half
    off = np.arange(kl)[None, :] - half - np.arange(ATT_BLK)[:, None]
    band = np.abs(off) <= half
    onehot = _one_hot_t(_t5_bucket(off * dist_scale), NUM_BUCKETS)
    order = np.asarray([h for pair in head_pairs for h in pair])
    bias = jnp.dot(bias_tab.astype(F32)[:, order].T, onehot, precision=HIGHEST)
    bias = bias.reshape(len(head_pairs), 2 * ATT_BLK, kl)
    col = np.arange(kl)
    first_ok = (col >= half)[None, :]
    last_ok = (col < ATT_BLK + half)[None, :]
    out = []
    for first, last in ((0, 0), (1, 0), (0, 1), (1, 1)):
        ok = band & (first_ok if first else True) & (last_ok if last else True)
        ok = np.concatenate([ok, ok], axis=0)
        out.append(jnp.where(jnp.asarray(ok)[None], bias, NEG_INF))
    return jnp.stack(out)


def _pair_rows(x):
    lo = lax.broadcasted_iota(jnp.int32, x.shape, 1) < HEAD_DIM
    zero = jnp.zeros_like(x)
    return jnp.concatenate([jnp.where(lo, x, zero), jnp.where(lo, zero, x)], axis=0)


def _unpair_rows(x2):
    rows = x2.shape[0] // 2
    lo = lax.broadcasted_iota(jnp.int32, (rows, LANES), 1) < HEAD_DIM
    return jnp.where(lo, x2[:rows], x2[rows:])


def _banded_kernel(*refs, n_sub, sub_per_iter, half, kv_cols, n_pairs, shared_kv, has_sink, with_lse, dilation):
    cur_ref, prev_ref, next_ref, bias_ref = refs[:4]
    pos = 4
    sink_ref = None
    if has_sink:
        sink_ref = refs[pos]
        pos += 1
    o_ref = refs[pos]
    pos += 1
    lse_ref = None
    if with_lse:
        lse_ref = refs[pos]
        pos += 1
    kv_s, s_s, p_s = refs[pos:pos + 3]

    tq = n_sub * ATT_BLK
    kl = ATT_BLK + 2 * half
    kvw = kv_cols * LANES
    d = dilation
    j = pl.program_id(1)
    last_j = pl.num_programs(1) - 1
    row_is_lo = lax.broadcasted_iota(jnp.int32, (2 * ATT_BLK, 1), 0) < ATT_BLK

    def one_residue(r, carry):
        kv_s[0:half, :] = prev_ref[0, r]
        kv_s[half:half + tq, :] = cur_ref[0, r, :, :2 * kvw]
        kv_s[half + tq:, :] = next_ref[0, r]

        def sub_blocks(it, carry2):
            units = [(ii, c) for ii in range(sub_per_iter) for c in range(n_pairs)]
            row0s, variants = [], []
            for ii in range(sub_per_iter):
                i = it * sub_per_iter + ii
                row0s.append(pl.multiple_of(i * ATT_BLK, ATT_BLK))
                is_first = jnp.logical_and(i == 0, j == 0).astype(jnp.int32)
                is_last = jnp.logical_and(i == n_sub - 1, j == last_j).astype(jnp.int32)
                variants.append(is_first + 2 * is_last)

            for u, (ii, c) in enumerate(units):
                kc = 0 if shared_kv else c
                q2 = _pair_rows(cur_ref[0, r, pl.ds(row0s[ii], ATT_BLK),
                                        2 * kvw + c * LANES:2 * kvw + (c + 1) * LANES])
                k = kv_s[pl.ds(row0s[ii], kl), kc * LANES:(kc + 1) * LANES]
                s_s[u] = lax.dot_general(q2, k, (((1,), (1,)), ((), ())), preferred_element_type=F32)

            stats = []
            for u, (ii, c) in enumerate(units):
                s = s_s[u] + bias_ref[variants[ii], c]
                m = jnp.max(s, axis=-1, keepdims=True)
                if has_sink:
                    sk = jnp.where(row_is_lo, sink_ref[2 * c], sink_ref[2 * c + 1])
                    m = jnp.maximum(m, sk)
                p = jnp.exp(s - m)
                denom = jnp.sum(p, axis=-1, keepdims=True)
                if has_sink:
                    denom = denom + jnp.exp(sk - m)
                p_s[u] = p.astype(BF16)
                stats.append((m, denom))

            for ii in range(sub_per_iter):
                o_cols, lse_cols = [], []
                for c in range(n_pairs):
                    u = ii * n_pairs + c
                    m, denom = stats[u]
                    kc = 0 if shared_kv else c
                    v = kv_s[pl.ds(row0s[ii], kl), kvw + kc * LANES:kvw + (kc + 1) * LANES]
                    o2 = jnp.dot(p_s[u], v, preferred_element_type=F32) / denom
                    o_cols.append(_unpair_rows(o2))
                    if with_lse:
                        lse_cols.append(_unpair_rows(jnp.broadcast_to(m + jnp.log(denom), (2 * ATT_BLK, LANES))))
                if d == 1:
                    rows = pl.ds(row0s[ii], ATT_BLK)
                else:
                    rows = pl.ds(row0s[ii] * d + r, ATT_BLK, stride=d)
                if with_lse:
                    for c in range(n_pairs):
                        o_ref[0, c, rows, :] = o_cols[c].astype(o_ref.dtype)
                        lse_ref[0, c, rows, :] = lse_cols[c]
                else:
                    o_ref[0, rows, :] = jnp.concatenate(o_cols, axis=-1).astype(o_ref.dtype)
            return carry2

        lax.fori_loop(0, n_sub // sub_per_iter, sub_blocks, 0)
        return carry

    lax.fori_loop(0, d, one_residue, 0)


def _banded_attention(arr, bias, sink, *, tq, n_pairs, shared_kv, half, with_lse):
    Bn, d, Ls, W = arr.shape
    L = d * Ls
    qw = n_pairs * LANES
    kv_cols = (W - qw) // (2 * LANES)
    kvw = kv_cols * LANES
    tq = min(tq, Ls)
    n_sub = tq // ATT_BLK
    nbt = Ls // tq
    sub_per_iter = 2 if (n_pairs <= 2 and n_sub % 2 == 0) else 1
    n_units = sub_per_iter * n_pairs
    halo_per_tile = tq // half
    n_halo = Ls // half
    assert Ls % tq == 0 and tq % half == 0 and kvw * 2 + qw == W
    o_dtype = BF16 if d == 1 else F32

    in_specs = [
        pl.BlockSpec((1, d, tq, W), lambda b, j: (b, 0, j, 0)),
        pl.BlockSpec((1, d, half, 2 * kvw), lambda b, j: (b, 0, jnp.maximum(j * halo_per_tile - 1, 0), 0)),
        pl.BlockSpec((1, d, half, 2 * kvw),
                     lambda b, j: (b, 0, jnp.minimum((j + 1) * halo_per_tile, n_halo - 1), 0)),
        pl.BlockSpec(bias.shape, lambda b, j: (0, 0, 0, 0)),
    ]
    args = [arr, arr, arr, bias]
    if sink is not None:
        in_specs.append(pl.BlockSpec(memory_space=pltpu.SMEM))
        args.append(sink.astype(F32))
    if with_lse:
        out_block = pl.BlockSpec((1, n_pairs, tq * d, LANES), lambda b, j: (b, 0, j, 0))
        out_shape = [jax.ShapeDtypeStruct((Bn, n_pairs, L, LANES), o_dtype),
                     jax.ShapeDtypeStruct((Bn, n_pairs, L, LANES), F32)]
        out_specs = [out_block, out_block]
    else:
        assert d == 1
        out_shape = [jax.ShapeDtypeStruct((Bn, L, qw), o_dtype)]
        out_specs = [pl.BlockSpec((1, tq, qw), lambda b, j: (b, j, 0))]
    return pl.pallas_call(
        functools.partial(_banded_kernel, n_sub=n_sub, sub_per_iter=sub_per_iter, half=half, kv_cols=kv_cols,
                          n_pairs=n_pairs, shared_kv=shared_kv, has_sink=sink is not None, with_lse=with_lse,
                          dilation=d),
        out_shape=tuple(out_shape),
        grid=(Bn, nbt),
        in_specs=in_specs,
        out_specs=tuple(out_specs),
        scratch_shapes=[pltpu.VMEM((tq + 2 * half, 2 * kvw), BF16),
                        pltpu.VMEM((n_units, 2 * ATT_BLK, ATT_BLK + 2 * half), F32),
                        pltpu.VMEM((n_units, 2 * ATT_BLK, ATT_BLK + 2 * half), BF16)],
        compiler_params=_cparams(("parallel", "arbitrary")),
        name=f"banded_attn_d{d}_h{half}",
    )(*args)


NBR_ROWS_PER_STEP = 8
NBR_KEY_BLOCK_ROWS = 4
NBR_ROWS_PER_ITER = 2


def _nbr_bias(rpb):
    H = rpb.shape[0]
    c = np.arange(GRID_W)
    qstart = np.clip(c - NA_KW // 2, 0, GRID_W - NA_KW)
    kc = np.arange(GRID_W)
    valid = (kc[None, :] >= qstart[:, None]) & (kc[None, :] < qstart[:, None] + NA_KW)
    cidx = np.clip(kc[None, :] - c[:, None] + NA_KW - 1, 0, 2 * NA_KW - 2)
    rows = jnp.stack([rpb.astype(F32)[:, NA_KH - 1 - v:2 * NA_KH - 1 - v] for v in range(NA_KH)])
    col_sel = _one_hot_t(cidx, 2 * NA_KW - 1)
    tab = jnp.dot(rows.reshape(-1, 2 * NA_KW - 1), col_sel, precision=HIGHEST)
    tab = tab.reshape(NA_KH, H, NA_KH, GRID_W, GRID_W).transpose(0, 1, 3, 2, 4)
    tab = jnp.where(jnp.asarray(valid)[None, None, :, None, :], tab, NEG_INF)
    return tab.reshape(NA_KH, H // 2, 2 * GRID_W, NA_KH * GRID_W)


def _nbr_kernel(q_ref, kv0, kv1, kv2, kv3, bias_ref, o_ref, kv_s, s_s, p_s, *, rows):
    blk = NBR_KEY_BLOCK_ROWS * GRID_W
    for t, ref in enumerate((kv0, kv1, kv2, kv3)):
        kv_s[t * blk:(t + 1) * blk, :] = ref[...]
    i0 = pl.program_id(1) * NBR_ROWS_PER_STEP
    nkeys = NA_KH * GRID_W
    n_pairs = C_HEADS // 2

    def row_group(it, carry):
        offs, variants, q0s = [], [], []
        for rr in range(NBR_ROWS_PER_ITER):
            r = it * NBR_ROWS_PER_ITER + rr
            i = i0 + r
            rstart = jnp.clip(i - NA_KH // 2, 0, rows - NA_KH)
            offs.append(pl.multiple_of((rstart - i0 + NBR_KEY_BLOCK_ROWS) * GRID_W, GRID_W))
            variants.append(i - rstart)
            q0s.append(pl.multiple_of(r * GRID_W, GRID_W))
        units = [(rr, c) for rr in range(NBR_ROWS_PER_ITER) for c in range(n_pairs)]

        for u, (rr, c) in enumerate(units):
            q2 = _pair_rows(q_ref[pl.ds(q0s[rr], GRID_W), c * LANES:(c + 1) * LANES])
            k = kv_s[pl.ds(offs[rr], nkeys), c * LANES:(c + 1) * LANES]
            s_s[u] = lax.dot_general(q2, k, (((1,), (1,)), ((), ())), preferred_element_type=F32)

        denoms = []
        for u, (rr, c) in enumerate(units):
            s = s_s[u] + bias_ref[variants[rr], c]
            m = jnp.max(s, axis=-1, keepdims=True)
            p = jnp.exp(s - m)
            denoms.append(jnp.sum(p, axis=-1, keepdims=True))
            p_s[u] = p.astype(BF16)

        for rr in range(NBR_ROWS_PER_ITER):
            o_cols = []
            for c in range(n_pairs):
                u = rr * n_pairs + c
                v = kv_s[pl.ds(offs[rr], nkeys), C_W + c * LANES:C_W + (c + 1) * LANES]
                o2 = jnp.dot(p_s[u], v, preferred_element_type=F32) / denoms[u]
                o_cols.append(_unpair_rows(o2))
            o_ref[pl.ds(q0s[rr], GRID_W), :] = jnp.concatenate(o_cols, axis=-1).astype(o_ref.dtype)
        return carry

    lax.fori_loop(0, NBR_ROWS_PER_STEP // NBR_ROWS_PER_ITER, row_group, 0)


def _neighborhood_attention(arr, bias, Bn, L):
    rows = L // GRID_W
    assert rows % NBR_ROWS_PER_STEP == 0 and rows >= 2 * NBR_ROWS_PER_STEP
    tq = NBR_ROWS_PER_STEP * GRID_W
    kb = NBR_KEY_BLOCK_ROWS * GRID_W
    nkb = L // kb
    per_step = NBR_ROWS_PER_STEP // NBR_KEY_BLOCK_ROWS
    nq = L // tq

    def key_spec(t):
        return pl.BlockSpec((kb, 2 * C_W), lambda b, j: (b * nkb + jnp.clip(j * per_step - 1 + t, 0, nkb - 1), 0))

    in_specs = ([pl.BlockSpec((tq, C_W), lambda b, j: (b * nq + j, 2))]
                + [key_spec(t) for t in range(4)]
                + [pl.BlockSpec(bias.shape, lambda b, j: (0, 0, 0, 0))])
    return pl.pallas_call(
        functools.partial(_nbr_kernel, rows=rows),
        out_shape=jax.ShapeDtypeStruct((Bn * L, C_W), BF16),
        grid=(Bn, nq),
        in_specs=in_specs,
        out_specs=pl.BlockSpec((tq, C_W), lambda b, j: (b * nq + j, 0)),
        scratch_shapes=[pltpu.VMEM((4 * kb, 2 * C_W), BF16),
                        pltpu.VMEM((NBR_ROWS_PER_ITER * C_HEADS // 2, 2 * GRID_W, NA_KH * GRID_W), F32),
                        pltpu.VMEM((NBR_ROWS_PER_ITER * C_HEADS // 2, 2 * GRID_W, NA_KH * GRID_W), BF16)],
        compiler_params=_cparams(("parallel", "arbitrary")),
        name="nbr_attn",
    )(*([arr] * 5), bias)


def _post_kernel(h_ref, hb_ref, oa_ref, ob0_ref, ob1_ref, ob2_ref, l0_ref, l1_ref, l2_ref, oc_ref,
                 wg_ref, bg_ref, wa_ref, wb_ref, wc_ref, wo_ref, g_ref, b_ref, wr_ref, br_ref,
                 o_ref, obf_ref, logit_ref, *, alpha):
    D = h_ref.shape[1]
    hb = hb_ref[...]
    def planes(ref):
        return jnp.concatenate([ref[0, c] for c in range(ref.shape[1])], axis=-1).astype(F32)

    l0, l1, l2 = planes(l0_ref), planes(l1_ref), planes(l2_ref)
    lm = jnp.maximum(jnp.maximum(l0, l1), l2)
    e0, e1, e2 = jnp.exp(l0 - lm), jnp.exp(l1 - lm), jnp.exp(l2 - lm)
    inv = 1.0 / (e0 + e1 + e2)
    o_b = ((e0 * planes(ob0_ref) + e1 * planes(ob1_ref) + e2 * planes(ob2_ref)) * inv).astype(BF16)
    merged = None
    for idx, (o_br, w_ref) in enumerate(((oa_ref[...], wa_ref), (o_b, wb_ref), (oc_ref[...], wc_ref))):
        z = (jnp.dot(hb, wg_ref[:, idx * D:(idx + 1) * D], preferred_element_type=F32)
             + bg_ref[:, idx * D:(idx + 1) * D])
        gate = 0.5 * jnp.tanh(0.5 * z) + 0.5
        term = gate * jnp.dot(o_br, w_ref[...], preferred_element_type=F32)
        merged = term if merged is None else merged + term
    y = jnp.dot(merged.astype(BF16), wo_ref[...], preferred_element_type=F32)
    h1 = _ln_rows(alpha * h_ref[...] + y, g_ref[...], b_ref[...])
    h1b = h1.astype(BF16)
    o_ref[...] = h1
    obf_ref[...] = h1b
    logit_ref[...] = jnp.dot(h1b, wr_ref[...], preferred_element_type=F32) + br_ref[...]


def _post_attention(h, hb, o_a, o_bs, lses, o_c, wg, bg, wa, wb, wc, wo, g, b, wr, br, alpha, tm=256):
    T, D = h.shape
    n_planes, L = o_bs[0].shape[1:3]
    nt = L // tm
    assert L % tm == 0

    def rows(w):
        return pl.BlockSpec((tm, w), lambda i: (i, 0))

    plane_rows = pl.BlockSpec((1, n_planes, tm, LANES), lambda i: (i // nt, 0, i % nt, 0))

    def full(a):
        return pl.BlockSpec(a.shape, lambda i: (0, 0))

    weights = [wg, bg, wa, wb, wc, wo, g, b, wr, br]
    return pl.pallas_call(
        functools.partial(_post_kernel, alpha=alpha),
        out_shape=(jax.ShapeDtypeStruct((T, D), F32), jax.ShapeDtypeStruct((T, D), BF16),
                   jax.ShapeDtypeStruct((T, LANES), F32)),
        grid=(T // tm,),
        in_specs=[rows(D), rows(D), rows(A_Q)] + [plane_rows] * 6 + [rows(C_W)] + [full(w) for w in weights],
        out_specs=(rows(D), rows(D), rows(LANES)),
        compiler_params=_cparams(("parallel",)),
        name="post_attn",
    )(h, hb, o_a, *o_bs, *lses, o_c, *weights)


ROUTE_EID, ROUTE_GATE, ROUTE_RANK = 0, 2, 4


def _route_kernel(logit_ref, meta_ref, count_ref, run_ref):
    tm = logit_ref.shape[0]

    @pl.when(pl.program_id(0) == 0)
    def _():
        run_ref[...] = jnp.zeros_like(run_ref)

    x = logit_ref[...]
    lane = lax.broadcasted_iota(jnp.int32, x.shape, 1).astype(F32)
    big = float(LANES)

    def lane_max(v):
        return jnp.max(v, axis=-1, keepdims=True)

    def first_lane(mask):
        return jnp.min(jnp.where(mask, lane, big), axis=-1, keepdims=True)

    is_g = lane < N_GROUPS
    gl = jnp.where(is_g, x, NEG_INF)
    gmax = lane_max(gl)
    g_idx = first_lane(jnp.logical_and(is_g, gl == gmax))
    g_w = 1.0 / jnp.sum(jnp.where(is_g, jnp.exp(gl - gmax), 0.0), axis=-1, keepdims=True)

    e_lo = N_GROUPS + g_idx * EXPERTS_PER_GROUP
    in_grp = jnp.logical_and(lane >= e_lo, lane < e_lo + EXPERTS_PER_GROUP)
    el = jnp.where(in_grp, x, NEG_INF)
    top1 = lane_max(el)
    lane1 = first_lane(jnp.logical_and(in_grp, el == top1))
    el2 = jnp.where(lane == lane1, NEG_INF, el)
    top2 = lane_max(el2)
    lane2 = first_lane(jnp.logical_and(in_grp, jnp.logical_and(lane != lane1, el2 == top2)))
    e2 = jnp.exp(top2 - top1)
    w1 = 1.0 / (1.0 + e2)
    w2 = e2 / (1.0 + e2)

    eid1 = lane1 - N_GROUPS
    eid2 = lane2 - N_GROUPS
    onehot = jnp.logical_or(lane == eid1, lane == eid2)
    oh = jnp.where(onehot, 1.0, 0.0).astype(BF16)
    r_i = lax.broadcasted_iota(jnp.int32, (tm, tm), 0)
    c_i = lax.broadcasted_iota(jnp.int32, (tm, tm), 1)
    strict_lower = jnp.where(c_i < r_i, 1.0, 0.0).astype(BF16)
    before = jnp.dot(strict_lower, oh, preferred_element_type=F32) + run_ref[...]
    rank1 = jnp.sum(jnp.where(lane == eid1, before, 0.0), axis=-1, keepdims=True)
    rank2 = jnp.sum(jnp.where(lane == eid2, before, 0.0), axis=-1, keepdims=True)
    run_ref[...] = run_ref[...] + jnp.sum(oh.astype(F32), axis=0, keepdims=True)

    meta = jnp.zeros(x.shape, F32)
    for k, val in ((ROUTE_EID, eid1), (ROUTE_EID + 1, eid2),
                   (ROUTE_GATE, g_w * w1), (ROUTE_GATE + 1, g_w * w2),
                   (ROUTE_RANK, rank1), (ROUTE_RANK + 1, rank2)):
        meta = jnp.where(lane == k, val, meta)
    meta_ref[...] = meta
    count_ref[...] = run_ref[...]


def _route(logits, tm=512):
    T = logits.shape[0]
    return pl.pallas_call(
        _route_kernel,
        out_shape=(jax.ShapeDtypeStruct((T, LANES), F32), jax.ShapeDtypeStruct((1, LANES), F32)),
        grid=(T // tm,),
        in_specs=[pl.BlockSpec((tm, LANES), lambda i: (i, 0))],
        out_specs=(pl.BlockSpec((tm, LANES), lambda i: (i, 0)), pl.BlockSpec((1, LANES), lambda i: (0, 0))),
        scratch_shapes=[pltpu.VMEM((1, LANES), F32)],
        compiler_params=_cparams(("arbitrary",)),
        name="route",
    )(logits)


def _expert_kernel(blk_e_ref, n_used_ref, x_ref, wg_ref, wu_ref, wd_ref, *rest, blk0):
    o_ref = rest[-1]
    b = pl.program_id(0) + blk0

    @pl.when(b < n_used_ref[0])
    def _():
        x = x_ref[...]
        a = jnp.dot(x, wg_ref[0], preferred_element_type=F32)
        u = jnp.dot(x, wu_ref[0], preferred_element_type=F32)
        hid = (a * jax.nn.sigmoid(a) * u).astype(BF16)
        o_ref[...] = jnp.dot(hid, wd_ref[0], preferred_element_type=F32)

    @pl.when(b >= n_used_ref[0])
    def _():
        o_ref[...] = jnp.zeros_like(o_ref)


def _experts(blk_e, n_used, xs, wg, wu, wd, y_prev, blk0, nblk):
    rows, D = xs.shape
    de = wg.shape[2]
    in_specs = [pl.BlockSpec((MOE_ROWS, D), lambda b, be, nu: (b, 0)),
                pl.BlockSpec((1, D, de), lambda b, be, nu: (be[b + blk0], 0, 0)),
                pl.BlockSpec((1, D, de), lambda b, be, nu: (be[b + blk0], 0, 0)),
                pl.BlockSpec((1, de, D), lambda b, be, nu: (be[b + blk0], 0, 0))]
    args = [blk_e, n_used, xs, wg, wu, wd]
    aliases = {}
    if y_prev is not None:
        in_specs.append(pl.BlockSpec(memory_space=pl.ANY))
        args.append(y_prev)
        aliases = {len(args) - 1: 0}
    grid_spec = pltpu.PrefetchScalarGridSpec(
        num_scalar_prefetch=2,
        grid=(rows // MOE_ROWS,),
        in_specs=in_specs,
        out_specs=pl.BlockSpec((MOE_ROWS, D), lambda b, be, nu: (b + blk0, 0)),
    )
    return pl.pallas_call(
        functools.partial(_expert_kernel, blk0=blk0),
        out_shape=jax.ShapeDtypeStruct((nblk * MOE_ROWS, D), F32),
        grid_spec=grid_spec,
        input_output_aliases=aliases,
        compiler_params=_cparams(("arbitrary",)),
        name="experts",
    )(*args)


def _combine_kernel(h_ref, ya_ref, yb_ref, meta_ref, g_ref, b_ref, *rest, alpha):
    o_ref, ob_ref = rest[-2:]
    meta = meta_ref[...]
    g1 = meta[:, ROUTE_GATE:ROUTE_GATE + 1]
    g2 = meta[:, ROUTE_GATE + 1:ROUTE_GATE + 2]
    y = ya_ref[...] * g1 + yb_ref[...] * g2
    h2 = _ln_rows(alpha * h_ref[...] + y, g_ref[...], b_ref[...])
    o_ref[...] = h2
    ob_ref[...] = h2.astype(BF16)


def _combine(h, ya, yb, meta, g, b, alpha, prev, row0, tm=512):
    T, D = h.shape
    t0 = row0 // tm
    assert row0 % tm == 0 and ya.shape[0] % tm == 0
    chunk_rows = pl.BlockSpec((tm, D), lambda i: (i, 0))
    full_rows = pl.BlockSpec((tm, D), lambda i: (i + t0, 0))
    vec = pl.BlockSpec((1, D), lambda i: (0, 0))
    in_specs = [full_rows, chunk_rows, chunk_rows, pl.BlockSpec((tm, LANES), lambda i: (i + t0, 0)), vec, vec]
    args = [h, ya, yb, meta, g.reshape(1, D), b.reshape(1, D)]
    aliases = {}
    if prev is not None:
        in_specs += [pl.BlockSpec(memory_space=pl.ANY)] * 2
        aliases = {len(args): 0, len(args) + 1: 1}
        args += list(prev)
    return pl.pallas_call(
        functools.partial(_combine_kernel, alpha=alpha),
        out_shape=(jax.ShapeDtypeStruct((T, D), F32), jax.ShapeDtypeStruct((T, D), BF16)),
        grid=(ya.shape[0] // tm,),
        in_specs=in_specs,
        out_specs=(full_rows, full_rows),
        input_output_aliases=aliases,
        compiler_params=_cparams(("parallel",)),
        name="combine_ln2",
    )(*args)


def _moe(h1, h1b, logits, w_eg, w_eu, w_ed, e_off, g, b, alpha):
    T, D = h1.shape
    meta, counts = _route(logits)
    eid = meta[:, ROUTE_EID:ROUTE_EID + 2].astype(jnp.int32)
    rank = meta[:, ROUTE_RANK:ROUTE_RANK + 2].astype(jnp.int32)
    cnt = counts[0, :N_EXPERTS].astype(jnp.int32)
    padded = (cnt + MOE_ROWS - 1) // MOE_ROWS * MOE_ROWS
    pad_end = jnp.cumsum(padded)
    pad_off = pad_end - padded
    dest = pad_off[eid] + rank
    nblk = -(-(2 * T) // MOE_ROWS) + N_EXPERTS
    P = nblk * MOE_ROWS
    tok = jnp.broadcast_to(jnp.arange(T, dtype=jnp.int32)[:, None], (T, 2))
    buf_tok = jnp.zeros((P,), jnp.int32).at[dest.reshape(-1)].set(tok.reshape(-1), unique_indices=True)
    blk_start = jnp.arange(nblk, dtype=jnp.int32) * MOE_ROWS
    blk_e = jnp.minimum(jnp.sum((pad_end[None, :] <= blk_start[:, None]).astype(jnp.int32), axis=1), N_EXPERTS - 1)
    blk_e = blk_e + e_off
    n_used = (pad_end[-1:] // MOE_ROWS).astype(jnp.int32)
    assert nblk % MOE_CHUNKS == 0 and T % MOE_CHUNKS == 0
    cb = nblk // MOE_CHUNKS
    y = None
    for c in range(MOE_CHUNKS):
        xs = h1b.at[buf_tok[c * cb * MOE_ROWS:(c + 1) * cb * MOE_ROWS]].get(mode='promise_in_bounds')
        y = _experts(blk_e, n_used, xs, w_eg, w_eu, w_ed, y, c * cb, nblk)
    ct = T // MOE_CHUNKS
    out = None
    for c in range(MOE_CHUNKS):
        ya = y.at[dest[c * ct:(c + 1) * ct, 0]].get(mode='promise_in_bounds')
        yb = y.at[dest[c * ct:(c + 1) * ct, 1]].get(mode='promise_in_bounds')
        out = _combine(h1, ya, yb, meta, g, b, alpha, out, c * ct)
    return out


def kernel(x, ln0_g, ln0_b, rel_bias, w_in, b_gate, sink_a, rpb_c, w_br_a, w_br_b, w_br_c, w_out,
           ln1_g, ln1_b, w_rg, b_rg, w_re, b_re, w_eg, w_eu, w_ed, ln2_g, ln2_b):
    Bn, L, D = x.shape
    depth = w_in.shape[0]
    T = Bn * L
    alpha = float((2 * depth) ** 0.25)
    a_order = np.asarray(A_HEAD_ORDER)

    a_pairs = [(c, c + A_HEADS // 2) for c in range(A_HEADS // 2)]
    b_pairs = [(2 * c, 2 * c + 1) for c in range(B_HEADS_PER_GROUP // 2)]
    bias_a = _band_bias(rel_bias[:, :A_HEADS], a_pairs, A_HALF_WINDOW, 1)
    bias_b = []
    for g, (window, dilation) in enumerate(B_CONFIGS):
        c0 = A_HEADS + g * B_HEADS_PER_GROUP
        bias_b.append(_band_bias(rel_bias[:, c0:c0 + B_HEADS_PER_GROUP], b_pairs, window // (2 * dilation), dilation))

    w_eg_b = w_eg.astype(BF16).reshape((depth * N_EXPERTS,) + w_eg.shape[2:])
    w_eu_b = w_eu.astype(BF16).reshape((depth * N_EXPERTS,) + w_eu.shape[2:])
    w_ed_b = w_ed.astype(BF16).reshape((depth * N_EXPERTS,) + w_ed.shape[2:])

    h, hb = _layer_norm(x.reshape(T, D), ln0_g, ln0_b)
    for l in range(depth):
        w_gate = w_in[l, :, QKV_COLS:].astype(BF16)
        arr_a, arr_b0, arr_b1, arr_b2, arr_c = _in_projection(hb, _inproj_weight(w_in[l]), Bn, L)

        (o_a,) = _banded_attention(arr_a, bias_a, sink_a[l][a_order], tq=512, n_pairs=A_HEADS // 2, shared_kv=True,
                                   half=A_HALF_WINDOW, with_lse=False)
        o_bs, lses = [], []
        for g, (arr, tq) in enumerate(((arr_b0, 512), (arr_b1, 512), (arr_b2, 256))):
            window, dilation = B_CONFIGS[g]
            o, lse = _banded_attention(arr, bias_b[g], None, tq=tq, n_pairs=B_HEADS_PER_GROUP // 2, shared_kv=False,
                                       half=window // (2 * dilation), with_lse=True)
            o_bs.append(o)
            lses.append(lse)
        o_c = _neighborhood_attention(arr_c, _nbr_bias(rpb_c[l]), Bn, L)

        w_r = jnp.concatenate([w_rg[l], jnp.transpose(w_re[l], (1, 0, 2)).reshape(D, N_EXPERTS)], axis=1)
        w_r = jnp.pad(w_r, ((0, 0), (0, LANES - w_r.shape[1]))).astype(BF16)
        b_r = jnp.concatenate([b_rg[l], b_re[l].reshape(-1)]).astype(F32)
        b_r = jnp.pad(b_r, (0, LANES - b_r.shape[0])).reshape(1, LANES)
        w_a = w_br_a[l].reshape(A_HEADS, HEAD_DIM, D)[a_order].reshape(A_Q, D)

        h1, h1b, logits = _post_attention(
            h, hb, o_a.reshape(T, A_Q), o_bs, lses, o_c,
            w_gate, b_gate[l].reshape(1, N_BRANCH * D).astype(F32),
            w_a.astype(BF16), w_br_b[l].astype(BF16), w_br_c[l].astype(BF16), w_out[l].astype(BF16),
            ln1_g[l].reshape(1, D), ln1_b[l].reshape(1, D), w_r, b_r, alpha)

        h, hb = _moe(h1, h1b, logits, w_eg_b, w_eu_b, w_ed_b, l * N_EXPERTS, ln2_g[l], ln2_b[l], alpha)
    return h.reshape(Bn, L, D)
```

```python
import functools

import numpy as np
import jax
import jax.numpy as jnp
from jax import lax
from jax.experimental import pallas as pl
from jax.experimental.pallas import tpu as pltpu

F32 = jnp.float32
BF16 = jnp.bfloat16
HIGHEST = lax.Precision.HIGHEST

HEAD_DIM = 64
A_HEADS = 8
A_KV_HEADS = 2
A_HALF_WINDOW = 128
B_HEADS_PER_GROUP = 4
B_CONFIGS = ((128, 1), (512, 4), (2048, 16))
C_HEADS = 8
GRID_W = 64
NA_KH = 8
NA_KW = 16
NUM_BUCKETS = 32
REL_MAX_DIST = 2048
N_GROUPS = 4
EXPERTS_PER_GROUP = 8
N_EXPERTS = N_GROUPS * EXPERTS_PER_GROUP
D_EXPERT = 512
LN_EPS = 1e-5
NEG_INF = -1e30

A_Q = A_HEADS * HEAD_DIM
A_KV = A_KV_HEADS * HEAD_DIM
B_W = B_HEADS_PER_GROUP * HEAD_DIM
C_W = C_HEADS * HEAD_DIM
N_BRANCH = 3
QKV_COLS = A_Q + 2 * A_KV + 3 * len(B_CONFIGS) * B_W + 3 * C_W
B_COL0 = A_Q + 2 * A_KV
C_COL0 = B_COL0 + 3 * len(B_CONFIGS) * B_W

ATT_BLK = 128
LANES = 128
MOE_ROWS = 256
MOE_CHUNKS = 4
VMEM_LIMIT = 56 * 1024 * 1024

A_HEAD_ORDER = tuple(h for c in range(A_HEADS // 2) for h in (c, c + A_HEADS // 2))


def _cparams(sem):
    return pltpu.CompilerParams(dimension_semantics=sem, vmem_limit_bytes=VMEM_LIMIT)


def _ln_rows(x, g, b):
    mu = jnp.mean(x, axis=-1, keepdims=True)
    xc = x - mu
    var = jnp.mean(xc * xc, axis=-1, keepdims=True)
    return xc * lax.rsqrt(var + LN_EPS) * g + b


def _ln_kernel(x_ref, g_ref, b_ref, o_ref, ob_ref):
    y = _ln_rows(x_ref[...].astype(F32), g_ref[...], b_ref[...])
    o_ref[...] = y
    ob_ref[...] = y.astype(BF16)


def _layer_norm(x, g, b, tm=512):
    T, D = x.shape
    return pl.pallas_call(
        _ln_kernel,
        out_shape=(jax.ShapeDtypeStruct((T, D), F32), jax.ShapeDtypeStruct((T, D), BF16)),
        grid=(T // tm,),
        in_specs=[pl.BlockSpec((tm, D), lambda i: (i, 0)),
                  pl.BlockSpec((1, D), lambda i: (0, 0)),
                  pl.BlockSpec((1, D), lambda i: (0, 0))],
        out_specs=(pl.BlockSpec((tm, D), lambda i: (i, 0)), pl.BlockSpec((tm, D), lambda i: (i, 0))),
        compiler_params=_cparams(("parallel",)),
        name="ln0",
    )(x, g.reshape(1, D), b.reshape(1, D))


BAND_W = 3 * B_W
NBR_W = 3 * C_W


def _inproj_weight(w_in_l):
    scale = HEAD_DIM ** -0.5
    a_q = w_in_l[:, :A_Q].reshape(-1, A_HEADS, HEAD_DIM)[:, np.asarray(A_HEAD_ORDER)].reshape(-1, A_Q)
    cols = [w_in_l[:, A_Q:A_Q + 2 * A_KV], a_q * scale]
    for g in range(len(B_CONFIGS)):
        c0 = B_COL0 + 3 * g * B_W
        cols += [w_in_l[:, c0 + B_W:c0 + 3 * B_W], w_in_l[:, c0:c0 + B_W] * scale]
    cols += [w_in_l[:, C_COL0 + C_W:C_COL0 + 3 * C_W], w_in_l[:, C_COL0:C_COL0 + C_W] * scale]
    return jnp.concatenate(cols, axis=1).astype(BF16)


def _inproj_kernel(h_ref, w_ref, a_ref, b0_ref, b1_ref, b2_ref, c_ref, tmp_ref):
    h = h_ref[...]
    tm = h.shape[0]

    def chunk(i):
        return jnp.dot(h, w_ref[:, i * BAND_W:(i + 1) * BAND_W], preferred_element_type=F32)

    a_ref[0, 0] = chunk(0).astype(BF16)
    b0_ref[0, 0] = chunk(1).astype(BF16)
    for i, ref in ((2, b1_ref), (3, b2_ref)):
        d = ref.shape[1]
        res = chunk(i)
        n_cols = BAND_W // LANES
        for c in range(n_cols):
            tmp_ref[c] = res[:, c * LANES:(c + 1) * LANES]
        for r in range(d):
            for c in range(n_cols):
                ref[0, r, :, c * LANES:(c + 1) * LANES] = tmp_ref[c, pl.ds(r, tm // d, stride=d), :].astype(BF16)
    c_ref[:, :BAND_W] = chunk(4).astype(BF16)
    c_ref[:, BAND_W:] = chunk(5).astype(BF16)


def _in_projection(hb, w, Bn, L, tm=512):
    T, D = hb.shape
    nt = L // tm
    assert L % tm == 0 and NBR_W == 2 * BAND_W and A_Q + 2 * A_KV == BAND_W
    dils = [d for _, d in B_CONFIGS]
    assert dils[0] == 1 and len(dils) == 3

    def band_spec(d):
        return pl.BlockSpec((1, d, tm // d, BAND_W), lambda b, i: (b, 0, i, 0))

    return pl.pallas_call(
        _inproj_kernel,
        out_shape=(jax.ShapeDtypeStruct((Bn, 1, L, BAND_W), BF16),
                   jax.ShapeDtypeStruct((Bn, 1, L, BAND_W), BF16),
                   jax.ShapeDtypeStruct((Bn, dils[1], L // dils[1], BAND_W), BF16),
                   jax.ShapeDtypeStruct((Bn, dils[2], L // dils[2], BAND_W), BF16),
                   jax.ShapeDtypeStruct((T, NBR_W), BF16)),
        grid=(Bn, nt),
        in_specs=[pl.BlockSpec((tm, D), lambda b, i: (b * nt + i, 0)),
                  pl.BlockSpec(w.shape, lambda b, i: (0, 0))],
        out_specs=(band_spec(1), band_spec(1), band_spec(dils[1]), band_spec(dils[2]),
                   pl.BlockSpec((tm, NBR_W), lambda b, i: (b * nt + i, 0))),
        scratch_shapes=[pltpu.VMEM((BAND_W // LANES, tm, LANES), F32)],
        compiler_params=_cparams(("parallel", "parallel")),
        name="in_proj",
    )(hb, w)


def _t5_bucket(rel):
    half = NUM_BUCKETS // 2
    max_exact = half // 2
    ret = np.where(rel > 0, half, 0)
    n = np.abs(rel)
    large = max_exact + (np.log(np.maximum(n, max_exact) / max_exact)
                         / np.log(REL_MAX_DIST / max_exact) * (half - max_exact)).astype(np.int32)
    large = np.minimum(large, half - 1)
    return (ret + np.where(n < max_exact, n, large)).astype(np.int32)


def _one_hot_t(idx, n):
    return jnp.asarray((np.arange(n)[:, None] == np.asarray(idx).reshape(1, -1)).astype(np.float32))


def _band_bias(bias_tab, head_pairs, half, dist_scale):
    kl = ATT_BLK + 2 * half
    off = np.arange(kl)[None, :] - half - np.arange(ATT_BLK)[:, None]
    band = np.abs(off) <= half
    onehot = _one_hot_t(_t5_bucket(off * dist_scale), NUM_BUCKETS)
    order = np.asarray([h for pair in head_pairs for h in pair])
    bias = jnp.dot(bias_tab.astype(F32)[:, order].T, onehot, precision=HIGHEST)
    bias = bias.reshape(len(head_pairs), 2 * ATT_BLK, kl)
    col = np.arange(kl)
    first_ok = (col >= half)[None, :]
    last_ok = (col < ATT_BLK + half)[None, :]
    out = []
    for first, last in ((0, 0), (1, 0), (0, 1), (1, 1)):
        ok = band & (first_ok if first else True) & (last_ok if last else True)
        ok = np.concatenate([ok, ok], axis=0)
        out.append(jnp.where(jnp.asarray(ok)[None], bias, NEG_INF))
    return jnp.stack(out)


def _pair_rows(x):
    lo = lax.broadcasted_iota(jnp.int32, x.shape, 1) < HEAD_DIM
    zero = jnp.zeros_like(x)
    return jnp.concatenate([jnp.where(lo, x, zero), jnp.where(lo, zero, x)], axis=0)


def _unpair_rows(x2):
    rows = x2.shape[0] // 2
    lo = lax.broadcasted_iota(jnp.int32, (rows, LANES), 1) < HEAD_DIM
    return jnp.where(lo, x2[:rows], x2[rows:])


def _banded_kernel(*refs, n_sub, sub_per_iter, half, kv_cols, n_pairs, shared_kv, has_sink, with_lse, dilation):
    cur_ref, prev_ref, next_ref, bias_ref = refs[:4]
    pos = 4
    sink_ref = None
    if has_sink:
        sink_ref = refs[pos]
        pos += 1
    o_ref = refs[pos]
    pos += 1
    lse_ref = None
    if with_lse:
        lse_ref = refs[pos]
        pos += 1
    kv_s, s_s, p_s = refs[pos:pos + 3]

    tq = n_sub * ATT_BLK
    kl = ATT_BLK + 2 * half
    kvw = kv_cols * LANES
    d = dilation
    j = pl.program_id(1)
    last_j = pl.num_programs(1) - 1
    row_is_lo = lax.broadcasted_iota(jnp.int32, (2 * ATT_BLK, 1), 0) < ATT_BLK

    def one_residue(r, carry):
        kv_s[0:half, :] = prev_ref[0, r]
        kv_s[half:half + tq, :] = cur_ref[0, r, :, :2 * kvw]
        kv_s[half + tq:, :] = next_ref[0, r]

        def sub_blocks(it, carry2):
            units = [(ii, c) for ii in range(sub_per_iter) for c in range(n_pairs)]
            row0s, variants = [], []
            for ii in range(sub_per_iter):
                i = it * sub_per_iter + ii
                row0s.append(pl.multiple_of(i * ATT_BLK, ATT_BLK))
                is_first = jnp.logical_and(i == 0, j == 0).astype(jnp.int32)
                is_last = jnp.logical_and(i == n_sub - 1, j == last_j).astype(jnp.int32)
                variants.append(is_first + 2 * is_last)

            for u, (ii, c) in enumerate(units):
                kc = 0 if shared_kv else c
                q2 = _pair_rows(cur_ref[0, r, pl.ds(row0s[ii], ATT_BLK),
                                        2 * kvw + c * LANES:2 * kvw + (c + 1) * LANES])
                k = kv_s[pl.ds(row0s[ii], kl), kc * LANES:(kc + 1) * LANES]
                s_s[u] = lax.dot_general(q2, k, (((1,), (1,)), ((), ())), preferred_element_type=F32)

            stats = []
            for u, (ii, c) in enumerate(units):
                s = s_s[u] + bias_ref[variants[ii], c]
                m = jnp.max(s, axis=-1, keepdims=True)
                if has_sink:
                    sk = jnp.where(row_is_lo, sink_ref[2 * c], sink_ref[2 * c + 1])
                    m = jnp.maximum(m, sk)
                p = jnp.exp(s - m)
                denom = jnp.sum(p, axis=-1, keepdims=True)
                if has_sink:
                    denom = denom + jnp.exp(sk - m)
                p_s[u] = p.astype(BF16)
                stats.append((m, denom))

            for ii in range(sub_per_iter):
                o_cols, lse_cols = [], []
                for c in range(n_pairs):
                    u = ii * n_pairs + c
                    m, denom = stats[u]
                    kc = 0 if shared_kv else c
                    v = kv_s[pl.ds(row0s[ii], kl), kvw + kc * LANES:kvw + (kc + 1) * LANES]
                    o2 = jnp.dot(p_s[u], v, preferred_element_type=F32) / denom
                    o_cols.append(_unpair_rows(o2))
                    if with_lse:
                        lse_cols.append(_unpair_rows(jnp.broadcast_to(m + jnp.log(denom), (2 * ATT_BLK, LANES))))
                if d == 1:
                    rows = pl.ds(row0s[ii], ATT_BLK)
                else:
                    rows = pl.ds(row0s[ii] * d + r, ATT_BLK, stride=d)
                if with_lse:
                    for c in range(n_pairs):
                        o_ref[0, c, rows, :] = o_cols[c].astype(o_ref.dtype)
                        lse_ref[0, c, rows, :] = lse_cols[c]
                else:
                    o_ref[0, rows, :] = jnp.concatenate(o_cols, axis=-1).astype(o_ref.dtype)
            return carry2

        lax.fori_loop(0, n_sub // sub_per_iter, sub_blocks, 0)
        return carry

    lax.fori_loop(0, d, one_residue, 0)


def _banded_attention(arr, bias, sink, *, tq, n_pairs, shared_kv, half, with_lse):
    Bn, d, Ls, W = arr.shape
    L = d * Ls
    qw = n_pairs * LANES
    kv_cols = (W - qw) // (2 * LANES)
    kvw = kv_cols * LANES
    tq = min(tq, Ls)
    n_sub = tq // ATT_BLK
    nbt = Ls // tq
    sub_per_iter = 2 if n_sub % 2 == 0 else 1
    n_units = sub_per_iter * n_pairs
    halo_per_tile = tq // half
    n_halo = Ls // half
    assert Ls % tq == 0 and tq % half == 0 and kvw * 2 + qw == W
    o_dtype = BF16 if d == 1 else F32

    in_specs = [
        pl.BlockSpec((1, d, tq, W), lambda b, j: (b, 0, j, 0)),
        pl.BlockSpec((1, d, half, 2 * kvw), lambda b, j: (b, 0, jnp.maximum(j * halo_per_tile - 1, 0), 0)),
        pl.BlockSpec((1, d, half, 2 * kvw),
                     lambda b, j: (b, 0, jnp.minimum((j + 1) * halo_per_tile, n_halo - 1), 0)),
        pl.BlockSpec(bias.shape, lambda b, j: (0, 0, 0, 0)),
    ]
    args = [arr, arr, arr, bias]
    if sink is not None:
        in_specs.append(pl.BlockSpec(memory_space=pltpu.SMEM))
        args.append(sink.astype(F32))
    if with_lse:
        out_block = pl.BlockSpec((1, n_pairs, tq * d, LANES), lambda b, j: (b, 0, j, 0))
        out_shape = [jax.ShapeDtypeStruct((Bn, n_pairs, L, LANES), o_dtype),
                     jax.ShapeDtypeStruct((Bn, n_pairs, L, LANES), F32)]
        out_specs = [out_block, out_block]
    else:
        assert d == 1
        out_shape = [jax.ShapeDtypeStruct((Bn, L, qw), o_dtype)]
        out_specs = [pl.BlockSpec((1, tq, qw), lambda b, j: (b, j, 0))]
    return pl.pallas_call(
        functools.partial(_banded_kernel, n_sub=n_sub, sub_per_iter=sub_per_iter, half=half, kv_cols=kv_cols,
                          n_pairs=n_pairs, shared_kv=shared_kv, has_sink=sink is not None, with_lse=with_lse,
                          dilation=d),
        out_shape=tuple(out_shape),
        grid=(Bn, nbt),
        in_specs=in_specs,
        out_specs=tuple(out_specs),
        scratch_shapes=[pltpu.VMEM((tq + 2 * half, 2 * kvw), BF16),
                        pltpu.VMEM((n_units, 2 * ATT_BLK, ATT_BLK + 2 * half), F32),
                        pltpu.VMEM((n_units, 2 * ATT_BLK, ATT_BLK + 2 * half), BF16)],
        compiler_params=_cparams(("parallel", "arbitrary")),
        name=f"banded_attn_d{d}_h{half}",
    )(*args)


NBR_ROWS_PER_STEP = 8
NBR_KEY_BLOCK_ROWS = 4
NBR_ROWS_PER_ITER = 4


def _nbr_bias(rpb):
    H = rpb.shape[0]
    c = np.arange(GRID_W)
    qstart = np.clip(c - NA_KW // 2, 0, GRID_W - NA_KW)
    kc = np.arange(GRID_W)
    valid = (kc[None, :] >= qstart[:, None]) & (kc[None, :] < qstart[:, None] + NA_KW)
    cidx = np.clip(kc[None, :] - c[:, None] + NA_KW - 1, 0, 2 * NA_KW - 2)
    rows = jnp.stack([rpb.astype(F32)[:, NA_KH - 1 - v:2 * NA_KH - 1 - v] for v in range(NA_KH)])
    col_sel = _one_hot_t(cidx, 2 * NA_KW - 1)
    tab = jnp.dot(rows.reshape(-1, 2 * NA_KW - 1), col_sel, precision=HIGHEST)
    tab = tab.reshape(NA_KH, H, NA_KH, GRID_W, GRID_W).transpose(0, 1, 3, 2, 4)
    tab = jnp.where(jnp.asarray(valid)[None, None, :, None, :], tab, NEG_INF)
    return tab.reshape(NA_KH, H // 2, 2 * GRID_W, NA_KH * GRID_W)


def _nbr_kernel(q_ref, kv0, kv1, kv2, kv3, bias_ref, o_ref, kv_s, s_s, p_s, *, rows):
    blk = NBR_KEY_BLOCK_ROWS * GRID_W
    for t, ref in enumerate((kv0, kv1, kv2, kv3)):
        kv_s[t * blk:(t + 1) * blk, :] = ref[...]
    i0 = pl.program_id(1) * NBR_ROWS_PER_STEP
    nkeys = NA_KH * GRID_W
    n_pairs = C_HEADS // 2

    def row_group(it, carry):
        offs, variants, q0s = [], [], []
        for rr in range(NBR_ROWS_PER_ITER):
            r = it * NBR_ROWS_PER_ITER + rr
            i = i0 + r
            rstart = jnp.clip(i - NA_KH // 2, 0, rows - NA_KH)
            offs.append(pl.multiple_of((rstart - i0 + NBR_KEY_BLOCK_ROWS) * GRID_W, GRID_W))
            variants.append(i - rstart)
            q0s.append(pl.multiple_of(r * GRID_W, GRID_W))
        units = [(rr, c) for rr in range(NBR_ROWS_PER_ITER) for c in range(n_pairs)]

        for u, (rr, c) in enumerate(units):
            q2 = _pair_rows(q_ref[pl.ds(q0s[rr], GRID_W), c * LANES:(c + 1) * LANES])
            k = kv_s[pl.ds(offs[rr], nkeys), c * LANES:(c + 1) * LANES]
            s_s[u] = lax.dot_general(q2, k, (((1,), (1,)), ((), ())), preferred_element_type=F32)

        denoms = []
        for u, (rr, c) in enumerate(units):
            s = s_s[u] + bias_ref[variants[rr], c]
            m = jnp.max(s, axis=-1, keepdims=True)
            p = jnp.exp(s - m)
            denoms.append(jnp.sum(p, axis=-1, keepdims=True))
            p_s[u] = p.astype(BF16)

        for rr in range(NBR_ROWS_PER_ITER):
            o_cols = []
            for c in range(n_pairs):
                u = rr * n_pairs + c
                v = kv_s[pl.ds(offs[rr], nkeys), C_W + c * LANES:C_W + (c + 1) * LANES]
                o2 = jnp.dot(p_s[u], v, preferred_element_type=F32) / denoms[u]
                o_cols.append(_unpair_rows(o2))
            o_ref[pl.ds(q0s[rr], GRID_W), :] = jnp.concatenate(o_cols, axis=-1).astype(o_ref.dtype)
        return carry

    lax.fori_loop(0, NBR_ROWS_PER_STEP // NBR_ROWS_PER_ITER, row_group, 0)


def _neighborhood_attention(arr, bias, Bn, L):
    rows = L // GRID_W
    assert rows % NBR_ROWS_PER_STEP == 0 and rows >= 2 * NBR_ROWS_PER_STEP
    tq = NBR_ROWS_PER_STEP * GRID_W
    kb = NBR_KEY_BLOCK_ROWS * GRID_W
    nkb = L // kb
    per_step = NBR_ROWS_PER_STEP // NBR_KEY_BLOCK_ROWS
    nq = L // tq

    def key_spec(t):
        return pl.BlockSpec((kb, 2 * C_W), lambda b, j: (b * nkb + jnp.clip(j * per_step - 1 + t, 0, nkb - 1), 0))

    in_specs = ([pl.BlockSpec((tq, C_W), lambda b, j: (b * nq + j, 2))]
                + [key_spec(t) for t in range(4)]
                + [pl.BlockSpec(bias.shape, lambda b, j: (0, 0, 0, 0))])
    return pl.pallas_call(
        functools.partial(_nbr_kernel, rows=rows),
        out_shape=jax.ShapeDtypeStruct((Bn * L, C_W), BF16),
        grid=(Bn, nq),
        in_specs=in_specs,
        out_specs=pl.BlockSpec((tq, C_W), lambda b, j: (b * nq + j, 0)),
        scratch_shapes=[pltpu.VMEM((4 * kb, 2 * C_W), BF16),
                        pltpu.VMEM((NBR_ROWS_PER_ITER * C_HEADS // 2, 2 * GRID_W, NA_KH * GRID_W), F32),
                        pltpu.VMEM((NBR_ROWS_PER_ITER * C_HEADS // 2, 2 * GRID_W, NA_KH * GRID_W), BF16)],
        compiler_params=_cparams(("parallel", "arbitrary")),
        name="nbr_attn",
    )(*([arr] * 5), bias)


def _post_kernel(h_ref, hb_ref, oa_ref, ob0_ref, ob1_ref, ob2_ref, l0_ref, l1_ref, l2_ref, oc_ref,
                 wg_ref, bg_ref, wa_ref, wb_ref, wc_ref, wo_ref, g_ref, b_ref, wr_ref, br_ref,
                 o_ref, obf_ref, logit_ref, *, alpha):
    D = h_ref.shape[1]
    hb = hb_ref[...]
    def planes(ref):
        return jnp.concatenate([ref[0, c] for c in range(ref.shape[1])], axis=-1).astype(F32)

    l0, l1, l2 = planes(l0_ref), planes(l1_ref), planes(l2_ref)
    lm = jnp.maximum(jnp.maximum(l0, l1), l2)
    e0, e1, e2 = jnp.exp(l0 - lm), jnp.exp(l1 - lm), jnp.exp(l2 - lm)
    inv = 1.0 / (e0 + e1 + e2)
    o_b = ((e0 * planes(ob0_ref) + e1 * planes(ob1_ref) + e2 * planes(ob2_ref)) * inv).astype(BF16)
    merged = None
    for idx, (o_br, w_ref) in enumerate(((oa_ref[...], wa_ref), (o_b, wb_ref), (oc_ref[...], wc_ref))):
        z = (jnp.dot(hb, wg_ref[:, idx * D:(idx + 1) * D], preferred_element_type=F32)
             + bg_ref[:, idx * D:(idx + 1) * D])
        gate = 0.5 * jnp.tanh(0.5 * z) + 0.5
        term = gate * jnp.dot(o_br, w_ref[...], preferred_element_type=F32)
        merged = term if merged is None else merged + term
    y = jnp.dot(merged.astype(BF16), wo_ref[...], preferred_element_type=F32)
    h1 = _ln_rows(alpha * h_ref[...] + y, g_ref[...], b_ref[...])
    h1b = h1.astype(BF16)
    o_ref[...] = h1
    obf_ref[...] = h1b
    logit_ref[...] = jnp.dot(h1b, wr_ref[...], preferred_element_type=F32) + br_ref[...]


def _post_attention(h, hb, o_a, o_bs, lses, o_c, wg, bg, wa, wb, wc, wo, g, b, wr, br, alpha, tm=256):
    T, D = h.shape
    n_planes, L = o_bs[0].shape[1:3]
    nt = L // tm
    assert L % tm == 0

    def rows(w):
        return pl.BlockSpec((tm, w), lambda i: (i, 0))

    plane_rows = pl.BlockSpec((1, n_planes, tm, LANES), lambda i: (i // nt, 0, i % nt, 0))

    def full(a):
        return pl.BlockSpec(a.shape, lambda i: (0, 0))

    weights = [wg, bg, wa, wb, wc, wo, g, b, wr, br]
    return pl.pallas_call(
        functools.partial(_post_kernel, alpha=alpha),
        out_shape=(jax.ShapeDtypeStruct((T, D), F32), jax.ShapeDtypeStruct((T, D), BF16),
                   jax.ShapeDtypeStruct((T, LANES), F32)),
        grid=(T // tm,),
        in_specs=[rows(D), rows(D), rows(A_Q)] + [plane_rows] * 6 + [rows(C_W)] + [full(w) for w in weights],
        out_specs=(rows(D), rows(D), rows(LANES)),
        compiler_params=_cparams(("parallel",)),
        name="post_attn",
    )(h, hb, o_a, *o_bs, *lses, o_c, *weights)


ROUTE_EID, ROUTE_GATE, ROUTE_RANK = 0, 2, 4


def _route_kernel(logit_ref, meta_ref, count_ref, run_ref):
    tm = logit_ref.shape[0]

    @pl.when(pl.program_id(0) == 0)
    def _():
        run_ref[...] = jnp.zeros_like(run_ref)

    x = logit_ref[...]
    lane = lax.broadcasted_iota(jnp.int32, x.shape, 1).astype(F32)
    big = float(LANES)

    def lane_max(v):
        return jnp.max(v, axis=-1, keepdims=True)

    def first_lane(mask):
        return jnp.min(jnp.where(mask, lane, big), axis=-1, keepdims=True)

    is_g = lane < N_GROUPS
    gl = jnp.where(is_g, x, NEG_INF)
    gmax = lane_max(gl)
    g_idx = first_lane(jnp.logical_and(is_g, gl == gmax))
    g_w = 1.0 / jnp.sum(jnp.where(is_g, jnp.exp(gl - gmax), 0.0), axis=-1, keepdims=True)

    e_lo = N_GROUPS + g_idx * EXPERTS_PER_GROUP
    in_grp = jnp.logical_and(lane >= e_lo, lane < e_lo + EXPERTS_PER_GROUP)
    el = jnp.where(in_grp, x, NEG_INF)
    top1 = lane_max(el)
    lane1 = first_lane(jnp.logical_and(in_grp, el == top1))
    el2 = jnp.where(lane == lane1, NEG_INF, el)
    top2 = lane_max(el2)
    lane2 = first_lane(jnp.logical_and(in_grp, jnp.logical_and(lane != lane1, el2 == top2)))
    e2 = jnp.exp(top2 - top1)
    w1 = 1.0 / (1.0 + e2)
    w2 = e2 / (1.0 + e2)

    eid1 = lane1 - N_GROUPS
    eid2 = lane2 - N_GROUPS
    onehot = jnp.logical_or(lane == eid1, lane == eid2)
    oh = jnp.where(onehot, 1.0, 0.0).astype(BF16)
    r_i = lax.broadcasted_iota(jnp.int32, (tm, tm), 0)
    c_i = lax.broadcasted_iota(jnp.int32, (tm, tm), 1)
    strict_lower = jnp.where(c_i < r_i, 1.0, 0.0).astype(BF16)
    before = jnp.dot(strict_lower, oh, preferred_element_type=F32) + run_ref[...]
    rank1 = jnp.sum(jnp.where(lane == eid1, before, 0.0), axis=-1, keepdims=True)
    rank2 = jnp.sum(jnp.where(lane == eid2, before, 0.0), axis=-1, keepdims=True)
    run_ref[...] = run_ref[...] + jnp.sum(oh.astype(F32), axis=0, keepdims=True)

    meta = jnp.zeros(x.shape, F32)
    for k, val in ((ROUTE_EID, eid1), (ROUTE_EID + 1, eid2),
                   (ROUTE_GATE, g_w * w1), (ROUTE_GATE + 1, g_w * w2),
                   (ROUTE_RANK, rank1), (ROUTE_RANK + 1, rank2)):
        meta = jnp.where(lane == k, val, meta)
    meta_ref[...] = meta
    count_ref[...] = run_ref[...]


def _route(logits, tm=512):
    T = logits.shape[0]
    return pl.pallas_call(
        _route_kernel,
        out_shape=(jax.ShapeDtypeStruct((T, LANES), F32), jax.ShapeDtypeStruct((1, LANES), F32)),
        grid=(T // tm,),
        in_specs=[pl.BlockSpec((tm, LANES), lambda i: (i, 0))],
        out_specs=(pl.BlockSpec((tm, LANES), lambda i: (i, 0)), pl.BlockSpec((1, LANES), lambda i: (0, 0))),
        scratch_shapes=[pltpu.VMEM((1, LANES), F32)],
        compiler_params=_cparams(("arbitrary",)),
        name="route",
    )(logits)


def _expert_kernel(blk_e_ref, n_used_ref, x_ref, wg_ref, wu_ref, wd_ref, *rest, blk0):
    o_ref = rest[-1]
    b = pl.program_id(0) + blk0

    @pl.when(b < n_used_ref[0])
    def _():
        x = x_ref[...]
        a = jnp.dot(x, wg_ref[0], preferred_element_type=F32)
        u = jnp.dot(x, wu_ref[0], preferred_element_type=F32)
        hid = (a * jax.nn.sigmoid(a) * u).astype(BF16)
        o_ref[...] = jnp.dot(hid, wd_ref[0], preferred_element_type=F32).astype(o_ref.dtype)

    @pl.when(b >= n_used_ref[0])
    def _():
        o_ref[...] = jnp.zeros_like(o_ref)


def _experts(blk_e, n_used, xs, wg, wu, wd, y_prev, blk0, nblk):
    rows, D = xs.shape
    de = wg.shape[2]
    in_specs = [pl.BlockSpec((MOE_ROWS, D), lambda b, be, nu: (b, 0)),
                pl.BlockSpec((1, D, de), lambda b, be, nu: (be[b + blk0], 0, 0)),
                pl.BlockSpec((1, D, de), lambda b, be, nu: (be[b + blk0], 0, 0)),
                pl.BlockSpec((1, de, D), lambda b, be, nu: (be[b + blk0], 0, 0))]
    args = [blk_e, n_used, xs, wg, wu, wd]
    aliases = {}
    if y_prev is not None:
        in_specs.append(pl.BlockSpec(memory_space=pl.ANY))
        args.append(y_prev)
        aliases = {len(args) - 1: 0}
    grid_spec = pltpu.PrefetchScalarGridSpec(
        num_scalar_prefetch=2,
        grid=(rows // MOE_ROWS,),
        in_specs=in_specs,
        out_specs=pl.BlockSpec((MOE_ROWS, D), lambda b, be, nu: (b + blk0, 0)),
    )
    return pl.pallas_call(
        functools.partial(_expert_kernel, blk0=blk0),
        out_shape=jax.ShapeDtypeStruct((nblk * MOE_ROWS, D), BF16),
        grid_spec=grid_spec,
        input_output_aliases=aliases,
        compiler_params=_cparams(("arbitrary",)),
        name="experts",
    )(*args)


def _combine_kernel(h_ref, ya_ref, yb_ref, meta_ref, g_ref, b_ref, *rest, alpha):
    o_ref, ob_ref = rest[-2:]
    meta = meta_ref[...]
    g1 = meta[:, ROUTE_GATE:ROUTE_GATE + 1]
    g2 = meta[:, ROUTE_GATE + 1:ROUTE_GATE + 2]
    y = ya_ref[...].astype(F32) * g1 + yb_ref[...].astype(F32) * g2
    h2 = _ln_rows(alpha * h_ref[...] + y, g_ref[...], b_ref[...])
    o_ref[...] = h2
    ob_ref[...] = h2.astype(BF16)


def _combine(h, ya, yb, meta, g, b, alpha, prev, row0, tm=512):
    T, D = h.shape
    t0 = row0 // tm
    assert row0 % tm == 0 and ya.shape[0] % tm == 0
    chunk_rows = pl.BlockSpec((tm, D), lambda i: (i, 0))
    full_rows = pl.BlockSpec((tm, D), lambda i: (i + t0, 0))
    vec = pl.BlockSpec((1, D), lambda i: (0, 0))
    in_specs = [full_rows, chunk_rows, chunk_rows, pl.BlockSpec((tm, LANES), lambda i: (i + t0, 0)), vec, vec]
    args = [h, ya, yb, meta, g.reshape(1, D), b.reshape(1, D)]
    aliases = {}
    if prev is not None:
        in_specs += [pl.BlockSpec(memory_space=pl.ANY)] * 2
        aliases = {len(args): 0, len(args) + 1: 1}
        args += list(prev)
    return pl.pallas_call(
        functools.partial(_combine_kernel, alpha=alpha),
        out_shape=(jax.ShapeDtypeStruct((T, D), F32), jax.ShapeDtypeStruct((T, D), BF16)),
        grid=(ya.shape[0] // tm,),
        in_specs=in_specs,
        out_specs=(full_rows, full_rows),
        input_output_aliases=aliases,
        compiler_params=_cparams(("parallel",)),
        name="combine_ln2",
    )(*args)


def _moe(h1, h1b, logits, w_eg, w_eu, w_ed, e_off, g, b, alpha):
    T, D = h1.shape
    meta, counts = _route(logits)
    eid = meta[:, ROUTE_EID:ROUTE_EID + 2].astype(jnp.int32)
    rank = meta[:, ROUTE_RANK:ROUTE_RANK + 2].astype(jnp.int32)
    cnt = counts[0, :N_EXPERTS].astype(jnp.int32)
    padded = (cnt + MOE_ROWS - 1) // MOE_ROWS * MOE_ROWS
    pad_end = jnp.cumsum(padded)
    pad_off = pad_end - padded
    dest = pad_off[eid] + rank
    nblk = -(-(2 * T) // MOE_ROWS) + N_EXPERTS
    P = nblk * MOE_ROWS
    tok = jnp.broadcast_to(jnp.arange(T, dtype=jnp.int32)[:, None], (T, 2))
    buf_tok = jnp.zeros((P,), jnp.int32).at[dest.reshape(-1)].set(tok.reshape(-1), unique_indices=True)
    blk_start = jnp.arange(nblk, dtype=jnp.int32) * MOE_ROWS
    blk_e = jnp.minimum(jnp.sum((pad_end[None, :] <= blk_start[:, None]).astype(jnp.int32), axis=1), N_EXPERTS - 1)
    blk_e = blk_e + e_off
    n_used = (pad_end[-1:] // MOE_ROWS).astype(jnp.int32)
    assert nblk % MOE_CHUNKS == 0 and T % MOE_CHUNKS == 0
    cb = nblk // MOE_CHUNKS
    y = None
    for c in range(MOE_CHUNKS):
        xs = h1b.at[buf_tok[c * cb * MOE_ROWS:(c + 1) * cb * MOE_ROWS]].get(mode='promise_in_bounds')
        y = _experts(blk_e, n_used, xs, w_eg, w_eu, w_ed, y, c * cb, nblk)
    ct = T // MOE_CHUNKS
    out = None
    for c in range(MOE_CHUNKS):
        ya = y.at[dest[c * ct:(c + 1) * ct, 0]].get(mode='promise_in_bounds')
        yb = y.at[dest[c * ct:(c + 1) * ct, 1]].get(mode='promise_in_bounds')
        out = _combine(h1, ya, yb, meta, g, b, alpha, out, c * ct)
    return out


def kernel(x, ln0_g, ln0_b, rel_bias, w_in, b_gate, sink_a, rpb_c, w_br_a, w_br_b, w_br_c, w_out,
           ln1_g, ln1_b, w_rg, b_rg, w_re, b_re, w_eg, w_eu, w_ed, ln2_g, ln2_b):
    Bn, L, D = x.shape
    depth = w_in.shape[0]
    T = Bn * L
    alpha = float((2 * depth) ** 0.25)
    a_order = np.asarray(A_HEAD_ORDER)

    a_pairs = [(c, c + A_HEADS // 2) for c in range(A_HEADS // 2)]
    b_pairs = [(2 * c, 2 * c + 1) for c in range(B_HEADS_PER_GROUP // 2)]
    bias_a = _band_bias(rel_bias[:, :A_HEADS], a_pairs, A_HALF_WINDOW, 1)
    bias_b = []
    for g, (window, dilation) in enumerate(B_CONFIGS):
        c0 = A_HEADS + g * B_HEADS_PER_GROUP
        bias_b.append(_band_bias(rel_bias[:, c0:c0 + B_HEADS_PER_GROUP], b_pairs, window // (2 * dilation), dilation))

    w_eg_b = w_eg.astype(BF16).reshape((depth * N_EXPERTS,) + w_eg.shape[2:])
    w_eu_b = w_eu.astype(BF16).reshape((depth * N_EXPERTS,) + w_eu.shape[2:])
    w_ed_b = w_ed.astype(BF16).reshape((depth * N_EXPERTS,) + w_ed.shape[2:])

    h, hb = _layer_norm(x.reshape(T, D), ln0_g, ln0_b)
    for l in range(depth):
        w_gate = w_in[l, :, QKV_COLS:].astype(BF16)
        arr_a, arr_b0, arr_b1, arr_b2, arr_c = _in_projection(hb, _inproj_weight(w_in[l]), Bn, L)

        (o_a,) = _banded_attention(arr_a, bias_a, sink_a[l][a_order], tq=512, n_pairs=A_HEADS // 2, shared_kv=True,
                                   half=A_HALF_WINDOW, with_lse=False)
        o_bs, lses = [], []
        for g, (arr, tq) in enumerate(((arr_b0, 512), (arr_b1, 512), (arr_b2, 256))):
            window, dilation = B_CONFIGS[g]
            o, lse = _banded_attention(arr, bias_b[g], None, tq=tq, n_pairs=B_HEADS_PER_GROUP // 2, shared_kv=False,
                                       half=window // (2 * dilation), with_lse=True)
            o_bs.append(o)
            lses.append(lse)
        o_c = _neighborhood_attention(arr_c, _nbr_bias(rpb_c[l]), Bn, L)

        w_r = jnp.concatenate([w_rg[l], jnp.transpose(w_re[l], (1, 0, 2)).reshape(D, N_EXPERTS)], axis=1)
        w_r = jnp.pad(w_r, ((0, 0), (0, LANES - w_r.shape[1]))).astype(BF16)
        b_r = jnp.concatenate([b_rg[l], b_re[l].reshape(-1)]).astype(F32)
        b_r = jnp.pad(b_r, (0, LANES - b_r.shape[0])).reshape(1, LANES)
        w_a = w_br_a[l].reshape(A_HEADS, HEAD_DIM, D)[a_order].reshape(A_Q, D)

        h1, h1b, logits = _post_attention(
            h, hb, o_a.reshape(T, A_Q), o_bs, lses, o_c,
            w_gate, b_gate[l].reshape(1, N_BRANCH * D).astype(F32),
            w_a.astype(BF16), w_br_b[l].astype(BF16), w_br_c[l].astype(BF16), w_out[l].astype(BF16),
            ln1_g[l].reshape(1, D), ln1_b[l].reshape(1, D), w_r, b_r, alpha)

        h, hb = _moe(h1, h1b, logits, w_eg_b, w_eu_b, w_ed_b, l * N_EXPERTS, ln2_g[l], ln2_b[l], alpha)
    return h.reshape(Bn, L, D)
```

```python
import functools

import numpy as np
import jax
import jax.numpy as jnp
from jax import lax
from jax.experimental import pallas as pl
from jax.experimental.pallas import tpu as pltpu

F32 = jnp.float32
BF16 = jnp.bfloat16
HIGHEST = lax.Precision.HIGHEST

HEAD_DIM = 64
A_HEADS = 8
A_KV_HEADS = 2
A_HALF_WINDOW = 128
B_HEADS_PER_GROUP = 4
B_CONFIGS = ((128, 1), (512, 4), (2048, 16))
C_HEADS = 8
GRID_W = 64
NA_KH = 8
NA_KW = 16
NUM_BUCKETS = 32
REL_MAX_DIST = 2048
N_GROUPS = 4
EXPERTS_PER_GROUP = 8
N_EXPERTS = N_GROUPS * EXPERTS_PER_GROUP
D_EXPERT = 512
LN_EPS = 1e-5
NEG_INF = -1e30

A_Q = A_HEADS * HEAD_DIM
A_KV = A_KV_HEADS * HEAD_DIM
B_W = B_HEADS_PER_GROUP * HEAD_DIM
C_W = C_HEADS * HEAD_DIM
N_BRANCH = 3
QKV_COLS = A_Q + 2 * A_KV + 3 * len(B_CONFIGS) * B_W + 3 * C_W
B_COL0 = A_Q + 2 * A_KV
C_COL0 = B_COL0 + 3 * len(B_CONFIGS) * B_W

ATT_BLK = 128
LANES = 128
MOE_ROWS = 256
MOE_CHUNKS = 4
VMEM_LIMIT = 56 * 1024 * 1024

A_HEAD_ORDER = tuple(h for c in range(A_HEADS // 2) for h in (c, c + A_HEADS // 2))


def _cparams(sem):
    return pltpu.CompilerParams(dimension_semantics=sem, vmem_limit_bytes=VMEM_LIMIT)


def _ln_rows(x, g, b):
    mu = jnp.mean(x, axis=-1, keepdims=True)
    xc = x - mu
    var = jnp.mean(xc * xc, axis=-1, keepdims=True)
    return xc * lax.rsqrt(var + LN_EPS) * g + b


def _ln_kernel(x_ref, g_ref, b_ref, o_ref, ob_ref):
    y = _ln_rows(x_ref[...].astype(F32), g_ref[...], b_ref[...])
    o_ref[...] = y
    ob_ref[...] = y.astype(BF16)


def _layer_norm(x, g, b, tm=512):
    T, D = x.shape
    return pl.pallas_call(
        _ln_kernel,
        out_shape=(jax.ShapeDtypeStruct((T, D), F32), jax.ShapeDtypeStruct((T, D), BF16)),
        grid=(T // tm,),
        in_specs=[pl.BlockSpec((tm, D), lambda i: (i, 0)),
                  pl.BlockSpec((1, D), lambda i: (0, 0)),
                  pl.BlockSpec((1, D), lambda i: (0, 0))],
        out_specs=(pl.BlockSpec((tm, D), lambda i: (i, 0)), pl.BlockSpec((tm, D), lambda i: (i, 0))),
        compiler_params=_cparams(("parallel",)),
        name="ln0",
    )(x, g.reshape(1, D), b.reshape(1, D))


BAND_W = 3 * B_W
NBR_W = 3 * C_W


def _inproj_weight(w_in_l):
    scale = HEAD_DIM ** -0.5
    a_q = w_in_l[:, :A_Q].reshape(-1, A_HEADS, HEAD_DIM)[:, np.asarray(A_HEAD_ORDER)].reshape(-1, A_Q)
    cols = [w_in_l[:, A_Q:A_Q + 2 * A_KV], a_q * scale]
    for g in range(len(B_CONFIGS)):
        c0 = B_COL0 + 3 * g * B_W
        cols += [w_in_l[:, c0 + B_W:c0 + 3 * B_W], w_in_l[:, c0:c0 + B_W] * scale]
    cols += [w_in_l[:, C_COL0 + C_W:C_COL0 + 3 * C_W], w_in_l[:, C_COL0:C_COL0 + C_W] * scale]
    return jnp.concatenate(cols, axis=1).astype(BF16)


def _inproj_kernel(h_ref, w_ref, a_ref, b0_ref, b1_ref, b2_ref, c_ref, tmp_ref):
    h = h_ref[...]
    tm = h.shape[0]

    def chunk(i):
        return jnp.dot(h, w_ref[:, i * BAND_W:(i + 1) * BAND_W], preferred_element_type=F32)

    a_ref[0, 0] = chunk(0).astype(BF16)
    b0_ref[0, 0] = chunk(1).astype(BF16)
    for i, ref in ((2, b1_ref), (3, b2_ref)):
        d = ref.shape[1]
        res = chunk(i)
        n_cols = BAND_W // LANES
        for c in range(n_cols):
            tmp_ref[c] = res[:, c * LANES:(c + 1) * LANES]
        for r in range(d):
            for c in range(n_cols):
                ref[0, r, :, c * LANES:(c + 1) * LANES] = tmp_ref[c, pl.ds(r, tm // d, stride=d), :].astype(BF16)
    c_ref[:, :BAND_W] = chunk(4).astype(BF16)
    c_ref[:, BAND_W:] = chunk(5).astype(BF16)


def _in_projection(hb, w, Bn, L, tm=512):
    T, D = hb.shape
    nt = L // tm
    assert L % tm == 0 and NBR_W == 2 * BAND_W and A_Q + 2 * A_KV == BAND_W
    dils = [d for _, d in B_CONFIGS]
    assert dils[0] == 1 and len(dils) == 3

    def band_spec(d):
        return pl.BlockSpec((1, d, tm // d, BAND_W), lambda b, i: (b, 0, i, 0))

    return pl.pallas_call(
        _inproj_kernel,
        out_shape=(jax.ShapeDtypeStruct((Bn, 1, L, BAND_W), BF16),
                   jax.ShapeDtypeStruct((Bn, 1, L, BAND_W), BF16),
                   jax.ShapeDtypeStruct((Bn, dils[1], L // dils[1], BAND_W), BF16),
                   jax.ShapeDtypeStruct((Bn, dils[2], L // dils[2], BAND_W), BF16),
                   jax.ShapeDtypeStruct((T, NBR_W), BF16)),
        grid=(Bn, nt),
        in_specs=[pl.BlockSpec((tm, D), lambda b, i: (b * nt + i, 0)),
                  pl.BlockSpec(w.shape, lambda b, i: (0, 0))],
        out_specs=(band_spec(1), band_spec(1), band_spec(dils[1]), band_spec(dils[2]),
                   pl.BlockSpec((tm, NBR_W), lambda b, i: (b * nt + i, 0))),
        scratch_shapes=[pltpu.VMEM((BAND_W // LANES, tm, LANES), F32)],
        compiler_params=_cparams(("parallel", "parallel")),
        name="in_proj",
    )(hb, w)


def _t5_bucket(rel):
    half = NUM_BUCKETS // 2
    max_exact = half // 2
    ret = np.where(rel > 0, half, 0)
    n = np.abs(rel)
    large = max_exact + (np.log(np.maximum(n, max_exact) / max_exact)
                         / np.log(REL_MAX_DIST / max_exact) * (half - max_exact)).astype(np.int32)
    large = np.minimum(large, half - 1)
    return (ret + np.where(n < max_exact, n, large)).astype(np.int32)


def _one_hot_t(idx, n):
    return jnp.asarray((np.arange(n)[:, None] == np.asarray(idx).reshape(1, -1)).astype(np.float32))


def _band_bias(bias_tab, head_pairs, half, dist_scale):
    kl = ATT_BLK + 2 * half
    off = np.arange(kl)[None, :] - half - np.arange(ATT_BLK)[:, None]
    band = np.abs(off) <= half
    onehot = _one_hot_t(_t5_bucket(off * dist_scale), NUM_BUCKETS)
    order = np.asarray([h for pair in head_pairs for h in pair])
    bias = jnp.dot(bias_tab.astype(F32)[:, order].T, onehot, precision=HIGHEST)
    bias = bias.reshape(len(head_pairs), 2 * ATT_BLK, kl)
    col = np.arange(kl)
    first_ok = (col >= half)[None, :]
    last_ok = (col < ATT_BLK + half)[None, :]
    out = []
    for first, last in ((0, 0), (1, 0), (0, 1), (1, 1)):
        ok = band & (first_ok if first else True) & (last_ok if last else True)
        ok = np.concatenate([ok, ok], axis=0)
        out.append(jnp.where(jnp.asarray(ok)[None], bias, NEG_INF))
    return jnp.stack(out)


def _pair_rows(x):
    lo = lax.broadcasted_iota(jnp.int32, x.shape, 1) < HEAD_DIM
    zero = jnp.zeros_like(x)
    return jnp.concatenate([jnp.where(lo, x, zero), jnp.where(lo, zero, x)], axis=0)


def _unpair_rows(x2):
    rows = x2.shape[0] // 2
    lo = lax.broadcasted_iota(jnp.int32, (rows, LANES), 1) < HEAD_DIM
    return jnp.where(lo, x2[:rows], x2[rows:])


def _banded_kernel(*refs, n_sub, sub_per_iter, half, kv_cols, n_pairs, shared_kv, has_sink, with_lse, dilation):
    cur_ref, prev_ref, next_ref, bias_ref = refs[:4]
    pos = 4
    sink_ref = None
    if has_sink:
        sink_ref = refs[pos]
        pos += 1
    o_ref = refs[pos]
    pos += 1
    lse_ref = None
    if with_lse:
        lse_ref = refs[pos]
        pos += 1
    kv_s, s_s, p_s = refs[pos:pos + 3]

    tq = n_sub * ATT_BLK
    kl = ATT_BLK + 2 * half
    kvw = kv_cols * LANES
    d = dilation
    j = pl.program_id(1)
    last_j = pl.num_programs(1) - 1
    row_is_lo = lax.broadcasted_iota(jnp.int32, (2 * ATT_BLK, 1), 0) < ATT_BLK

    def one_residue(r, carry):
        kv_s[0:half, :] = prev_ref[0, r]
        kv_s[half:half + tq, :] = cur_ref[0, r, :, :2 * kvw]
        kv_s[half + tq:, :] = next_ref[0, r]

        def sub_blocks(it, carry2):
            units = [(ii, c) for ii in range(sub_per_iter) for c in range(n_pairs)]
            row0s, variants = [], []
            for ii in range(sub_per_iter):
                i = it * sub_per_iter + ii
                row0s.append(pl.multiple_of(i * ATT_BLK, ATT_BLK))
                is_first = jnp.logical_and(i == 0, j == 0).astype(jnp.int32)
                is_last = jnp.logical_and(i == n_sub - 1, j == last_j).astype(jnp.int32)
                variants.append(is_first + 2 * is_last)

            for u, (ii, c) in enumerate(units):
                kc = 0 if shared_kv else c
                q2 = _pair_rows(cur_ref[0, r, pl.ds(row0s[ii], ATT_BLK),
                                        2 * kvw + c * LANES:2 * kvw + (c + 1) * LANES])
                k = kv_s[pl.ds(row0s[ii], kl), kc * LANES:(kc + 1) * LANES]
                s_s[u] = lax.dot_general(q2, k, (((1,), (1,)), ((), ())), preferred_element_type=F32)

            stats = []
            for u, (ii, c) in enumerate(units):
                s = s_s[u] + bias_ref[variants[ii], c]
                m = jnp.max(s, axis=-1, keepdims=True)
                if has_sink:
                    sk = jnp.where(row_is_lo, sink_ref[2 * c], sink_ref[2 * c + 1])
                    m = jnp.maximum(m, sk)
                p = jnp.exp(s - m)
                denom = jnp.sum(p, axis=-1, keepdims=True)
                if has_sink:
                    denom = denom + jnp.exp(sk - m)
                p_s[u] = p.astype(BF16)
                stats.append((m, denom))

            for ii in range(sub_per_iter):
                o_cols, lse_cols = [], []
                for c in range(n_pairs):
                    u = ii * n_pairs + c
                    m, denom = stats[u]
                    kc = 0 if shared_kv else c
                    v = kv_s[pl.ds(row0s[ii], kl), kvw + kc * LANES:kvw + (kc + 1) * LANES]
                    o2 = jnp.dot(p_s[u], v, preferred_element_type=F32) / denom
                    o_cols.append(_unpair_rows(o2))
                    if with_lse:
                        lse_cols.append(_unpair_rows(jnp.broadcast_to(m + jnp.log(denom), (2 * ATT_BLK, LANES))))
                if d == 1:
                    rows = pl.ds(row0s[ii], ATT_BLK)
                else:
                    rows = pl.ds(row0s[ii] * d + r, ATT_BLK, stride=d)
                if with_lse:
                    for c in range(n_pairs):
                        o_ref[0, c, rows, :] = o_cols[c].astype(o_ref.dtype)
                        lse_ref[0, c, rows, :] = lse_cols[c]
                else:
                    o_ref[0, rows, :] = jnp.concatenate(o_cols, axis=-1).astype(o_ref.dtype)
            return carry2

        lax.fori_loop(0, n_sub // sub_per_iter, sub_blocks, 0)
        return carry

    lax.fori_loop(0, d, one_residue, 0)


def _banded_attention(arr, bias, sink, *, tq, n_pairs, shared_kv, half, with_lse):
    Bn, d, Ls, W = arr.shape
    L = d * Ls
    qw = n_pairs * LANES
    kv_cols = (W - qw) // (2 * LANES)
    kvw = kv_cols * LANES
    tq = min(tq, Ls)
    n_sub = tq // ATT_BLK
    nbt = Ls // tq
    sub_per_iter = max(k for k in (1, 2, 4) if n_sub % k == 0)
    n_units = sub_per_iter * n_pairs
    halo_per_tile = tq // half
    n_halo = Ls // half
    assert Ls % tq == 0 and tq % half == 0 and kvw * 2 + qw == W
    o_dtype = BF16 if d == 1 else F32

    in_specs = [
        pl.BlockSpec((1, d, tq, W), lambda b, j: (b, 0, j, 0)),
        pl.BlockSpec((1, d, half, 2 * kvw), lambda b, j: (b, 0, jnp.maximum(j * halo_per_tile - 1, 0), 0)),
        pl.BlockSpec((1, d, half, 2 * kvw),
                     lambda b, j: (b, 0, jnp.minimum((j + 1) * halo_per_tile, n_halo - 1), 0)),
        pl.BlockSpec(bias.shape, lambda b, j: (0, 0, 0, 0)),
    ]
    args = [arr, arr, arr, bias]
    if sink is not None:
        in_specs.append(pl.BlockSpec(memory_space=pltpu.SMEM))
        args.append(sink.astype(F32))
    if with_lse:
        out_block = pl.BlockSpec((1, n_pairs, tq * d, LANES), lambda b, j: (b, 0, j, 0))
        out_shape = [jax.ShapeDtypeStruct((Bn, n_pairs, L, LANES), o_dtype),
                     jax.ShapeDtypeStruct((Bn, n_pairs, L, LANES), F32)]
        out_specs = [out_block, out_block]
    else:
        assert d == 1
        out_shape = [jax.ShapeDtypeStruct((Bn, L, qw), o_dtype)]
        out_specs = [pl.BlockSpec((1, tq, qw), lambda b, j: (b, j, 0))]
    return pl.pallas_call(
        functools.partial(_banded_kernel, n_sub=n_sub, sub_per_iter=sub_per_iter, half=half, kv_cols=kv_cols,
                          n_pairs=n_pairs, shared_kv=shared_kv, has_sink=sink is not None, with_lse=with_lse,
                          dilation=d),
        out_shape=tuple(out_shape),
        grid=(Bn, nbt),
        in_specs=in_specs,
        out_specs=tuple(out_specs),
        scratch_shapes=[pltpu.VMEM((tq + 2 * half, 2 * kvw), BF16),
                        pltpu.VMEM((n_units, 2 * ATT_BLK, ATT_BLK + 2 * half), F32),
                        pltpu.VMEM((n_units, 2 * ATT_BLK, ATT_BLK + 2 * half), BF16)],
        compiler_params=_cparams(("parallel", "arbitrary")),
        name=f"banded_attn_d{d}_h{half}",
    )(*args)


NBR_ROWS_PER_STEP = 8
NBR_KEY_BLOCK_ROWS = 4
NBR_ROWS_PER_ITER = 8


def _nbr_bias(rpb):
    H = rpb.shape[0]
    c = np.arange(GRID_W)
    qstart = np.clip(c - NA_KW // 2, 0, GRID_W - NA_KW)
    kc = np.arange(GRID_W)
    valid = (kc[None, :] >= qstart[:, None]) & (kc[None, :] < qstart[:, None] + NA_KW)
    cidx = np.clip(kc[None, :] - c[:, None] + NA_KW - 1, 0, 2 * NA_KW - 2)
    rows = jnp.stack([rpb.astype(F32)[:, NA_KH - 1 - v:2 * NA_KH - 1 - v] for v in range(NA_KH)])
    col_sel = _one_hot_t(cidx, 2 * NA_KW - 1)
    tab = jnp.dot(rows.reshape(-1, 2 * NA_KW - 1), col_sel, precision=HIGHEST)
    tab = tab.reshape(NA_KH, H, NA_KH, GRID_W, GRID_W).transpose(0, 1, 3, 2, 4)
    tab = jnp.where(jnp.asarray(valid)[None, None, :, None, :], tab, NEG_INF)
    return tab.reshape(NA_KH, H // 2, 2 * GRID_W, NA_KH * GRID_W)


def _nbr_kernel(q_ref, kv0, kv1, kv2, kv3, bias_ref, o_ref, kv_s, s_s, p_s, *, rows):
    blk = NBR_KEY_BLOCK_ROWS * GRID_W
    for t, ref in enumerate((kv0, kv1, kv2, kv3)):
        kv_s[t * blk:(t + 1) * blk, :] = ref[...]
    i0 = pl.program_id(1) * NBR_ROWS_PER_STEP
    nkeys = NA_KH * GRID_W
    n_pairs = C_HEADS // 2

    def row_group(it, carry):
        offs, variants, q0s = [], [], []
        for rr in range(NBR_ROWS_PER_ITER):
            r = it * NBR_ROWS_PER_ITER + rr
            i = i0 + r
            rstart = jnp.clip(i - NA_KH // 2, 0, rows - NA_KH)
            offs.append(pl.multiple_of((rstart - i0 + NBR_KEY_BLOCK_ROWS) * GRID_W, GRID_W))
            variants.append(i - rstart)
            q0s.append(pl.multiple_of(r * GRID_W, GRID_W))
        units = [(rr, c) for rr in range(NBR_ROWS_PER_ITER) for c in range(n_pairs)]

        for u, (rr, c) in enumerate(units):
            q2 = _pair_rows(q_ref[pl.ds(q0s[rr], GRID_W), c * LANES:(c + 1) * LANES])
            k = kv_s[pl.ds(offs[rr], nkeys), c * LANES:(c + 1) * LANES]
            s_s[u] = lax.dot_general(q2, k, (((1,), (1,)), ((), ())), preferred_element_type=F32)

        denoms = []
        for u, (rr, c) in enumerate(units):
            s = s_s[u] + bias_ref[variants[rr], c]
            m = jnp.max(s, axis=-1, keepdims=True)
            p = jnp.exp(s - m)
            denoms.append(jnp.sum(p, axis=-1, keepdims=True))
            p_s[u] = p.astype(BF16)

        for rr in range(NBR_ROWS_PER_ITER):
            o_cols = []
            for c in range(n_pairs):
                u = rr * n_pairs + c
                v = kv_s[pl.ds(offs[rr], nkeys), C_W + c * LANES:C_W + (c + 1) * LANES]
                o2 = jnp.dot(p_s[u], v, preferred_element_type=F32) / denoms[u]
                o_cols.append(_unpair_rows(o2))
            o_ref[pl.ds(q0s[rr], GRID_W), :] = jnp.concatenate(o_cols, axis=-1).astype(o_ref.dtype)
        return carry

    lax.fori_loop(0, NBR_ROWS_PER_STEP // NBR_ROWS_PER_ITER, row_group, 0)


def _neighborhood_attention(arr, bias, Bn, L):
    rows = L // GRID_W
    assert rows % NBR_ROWS_PER_STEP == 0 and rows >= 2 * NBR_ROWS_PER_STEP
    tq = NBR_ROWS_PER_STEP * GRID_W
    kb = NBR_KEY_BLOCK_ROWS * GRID_W
    nkb = L // kb
    per_step = NBR_ROWS_PER_STEP // NBR_KEY_BLOCK_ROWS
    nq = L // tq

    def key_spec(t):
        return pl.BlockSpec((kb, 2 * C_W), lambda b, j: (b * nkb + jnp.clip(j * per_step - 1 + t, 0, nkb - 1), 0))

    in_specs = ([pl.BlockSpec((tq, C_W), lambda b, j: (b * nq + j, 2))]
                + [key_spec(t) for t in range(4)]
                + [pl.BlockSpec(bias.shape, lambda b, j: (0, 0, 0, 0))])
    return pl.pallas_call(
        functools.partial(_nbr_kernel, rows=rows),
        out_shape=jax.ShapeDtypeStruct((Bn * L, C_W), BF16),
        grid=(Bn, nq),
        in_specs=in_specs,
        out_specs=pl.BlockSpec((tq, C_W), lambda b, j: (b * nq + j, 0)),
        scratch_shapes=[pltpu.VMEM((4 * kb, 2 * C_W), BF16),
                        pltpu.VMEM((NBR_ROWS_PER_ITER * C_HEADS // 2, 2 * GRID_W, NA_KH * GRID_W), F32),
                        pltpu.VMEM((NBR_ROWS_PER_ITER * C_HEADS // 2, 2 * GRID_W, NA_KH * GRID_W), BF16)],
        compiler_params=_cparams(("parallel", "arbitrary")),
        name="nbr_attn",
    )(*([arr] * 5), bias)


def _post_kernel(h_ref, hb_ref, oa_ref, ob0_ref, ob1_ref, ob2_ref, l0_ref, l1_ref, l2_ref, oc_ref,
                 wg_ref, bg_ref, wa_ref, wb_ref, wc_ref, wo_ref, g_ref, b_ref, wr_ref, br_ref,
                 o_ref, obf_ref, logit_ref, *, alpha):
    D = h_ref.shape[1]
    hb = hb_ref[...]
    def planes(ref):
        return jnp.concatenate([ref[0, c] for c in range(ref.shape[1])], axis=-1).astype(F32)

    l0, l1, l2 = planes(l0_ref), planes(l1_ref), planes(l2_ref)
    lm = jnp.maximum(jnp.maximum(l0, l1), l2)
    e0, e1, e2 = jnp.exp(l0 - lm), jnp.exp(l1 - lm), jnp.exp(l2 - lm)
    inv = 1.0 / (e0 + e1 + e2)
    o_b = ((e0 * planes(ob0_ref) + e1 * planes(ob1_ref) + e2 * planes(ob2_ref)) * inv).astype(BF16)
    merged = None
    for idx, (o_br, w_ref) in enumerate(((oa_ref[...], wa_ref), (o_b, wb_ref), (oc_ref[...], wc_ref))):
        z = (jnp.dot(hb, wg_ref[:, idx * D:(idx + 1) * D], preferred_element_type=F32)
             + bg_ref[:, idx * D:(idx + 1) * D])
        gate = 0.5 * jnp.tanh(0.5 * z) + 0.5
        term = gate * jnp.dot(o_br, w_ref[...], preferred_element_type=F32)
        merged = term if merged is None else merged + term
    y = jnp.dot(merged.astype(BF16), wo_ref[...], preferred_element_type=F32)
    h1 = _ln_rows(alpha * h_ref[...] + y, g_ref[...], b_ref[...])
    h1b = h1.astype(BF16)
    o_ref[...] = h1
    obf_ref[...] = h1b
    logit_ref[...] = jnp.dot(h1b, wr_ref[...], preferred_element_type=F32) + br_ref[...]


def _post_attention(h, hb, o_a, o_bs, lses, o_c, wg, bg, wa, wb, wc, wo, g, b, wr, br, alpha, tm=256):
    T, D = h.shape
    n_planes, L = o_bs[0].shape[1:3]
    nt = L // tm
    assert L % tm == 0

    def rows(w):
        return pl.BlockSpec((tm, w), lambda i: (i, 0))

    plane_rows = pl.BlockSpec((1, n_planes, tm, LANES), lambda i: (i // nt, 0, i % nt, 0))

    def full(a):
        return pl.BlockSpec(a.shape, lambda i: (0, 0))

    weights = [wg, bg, wa, wb, wc, wo, g, b, wr, br]
    return pl.pallas_call(
        functools.partial(_post_kernel, alpha=alpha),
        out_shape=(jax.ShapeDtypeStruct((T, D), F32), jax.ShapeDtypeStruct((T, D), BF16),
                   jax.ShapeDtypeStruct((T, LANES), F32)),
        grid=(T // tm,),
        in_specs=[rows(D), rows(D), rows(A_Q)] + [plane_rows] * 6 + [rows(C_W)] + [full(w) for w in weights],
        out_specs=(rows(D), rows(D), rows(LANES)),
        compiler_params=_cparams(("parallel",)),
        name="post_attn",
    )(h, hb, o_a, *o_bs, *lses, o_c, *weights)


ROUTE_EID, ROUTE_GATE, ROUTE_RANK = 0, 2, 4


def _route_kernel(logit_ref, meta_ref, count_ref, run_ref):
    tm = logit_ref.shape[0]

    @pl.when(pl.program_id(0) == 0)
    def _():
        run_ref[...] = jnp.zeros_like(run_ref)

    x = logit_ref[...]
    lane = lax.broadcasted_iota(jnp.int32, x.shape, 1).astype(F32)
    big = float(LANES)

    def lane_max(v):
        return jnp.max(v, axis=-1, keepdims=True)

    def first_lane(mask):
        return jnp.min(jnp.where(mask, lane, big), axis=-1, keepdims=True)

    is_g = lane < N_GROUPS
    gl = jnp.where(is_g, x, NEG_INF)
    gmax = lane_max(gl)
    g_idx = first_lane(jnp.logical_and(is_g, gl == gmax))
    g_w = 1.0 / jnp.sum(jnp.where(is_g, jnp.exp(gl - gmax), 0.0), axis=-1, keepdims=True)

    e_lo = N_GROUPS + g_idx * EXPERTS_PER_GROUP
    in_grp = jnp.logical_and(lane >= e_lo, lane < e_lo + EXPERTS_PER_GROUP)
    el = jnp.where(in_grp, x, NEG_INF)
    top1 = lane_max(el)
    lane1 = first_lane(jnp.logical_and(in_grp, el == top1))
    el2 = jnp.where(lane == lane1, NEG_INF, el)
    top2 = lane_max(el2)
    lane2 = first_lane(jnp.logical_and(in_grp, jnp.logical_and(lane != lane1, el2 == top2)))
    e2 = jnp.exp(top2 - top1)
    w1 = 1.0 / (1.0 + e2)
    w2 = e2 / (1.0 + e2)

    eid1 = lane1 - N_GROUPS
    eid2 = lane2 - N_GROUPS
    onehot = jnp.logical_or(lane == eid1, lane == eid2)
    oh = jnp.where(onehot, 1.0, 0.0).astype(BF16)
    r_i = lax.broadcasted_iota(jnp.int32, (tm, tm), 0)
    c_i = lax.broadcasted_iota(jnp.int32, (tm, tm), 1)
    strict_lower = jnp.where(c_i < r_i, 1.0, 0.0).astype(BF16)
    before = jnp.dot(strict_lower, oh, preferred_element_type=F32) + run_ref[...]
    rank1 = jnp.sum(jnp.where(lane == eid1, before, 0.0), axis=-1, keepdims=True)
    rank2 = jnp.sum(jnp.where(lane == eid2, before, 0.0), axis=-1, keepdims=True)
    run_ref[...] = run_ref[...] + jnp.sum(oh.astype(F32), axis=0, keepdims=True)

    meta = jnp.zeros(x.shape, F32)
    for k, val in ((ROUTE_EID, eid1), (ROUTE_EID + 1, eid2),
                   (ROUTE_GATE, g_w * w1), (ROUTE_GATE + 1, g_w * w2),
                   (ROUTE_RANK, rank1), (ROUTE_RANK + 1, rank2)):
        meta = jnp.where(lane == k, val, meta)
    meta_ref[...] = meta
    count_ref[...] = run_ref[...]


def _route(logits, tm=512):
    T = logits.shape[0]
    return pl.pallas_call(
        _route_kernel,
        out_shape=(jax.ShapeDtypeStruct((T, LANES), F32), jax.ShapeDtypeStruct((1, LANES), F32)),
        grid=(T // tm,),
        in_specs=[pl.BlockSpec((tm, LANES), lambda i: (i, 0))],
        out_specs=(pl.BlockSpec((tm, LANES), lambda i: (i, 0)), pl.BlockSpec((1, LANES), lambda i: (0, 0))),
        scratch_shapes=[pltpu.VMEM((1, LANES), F32)],
        compiler_params=_cparams(("arbitrary",)),
        name="route",
    )(logits)


def _expert_kernel(blk_e_ref, n_used_ref, x_ref, wg_ref, wu_ref, wd_ref, *rest, blk0):
    o_ref = rest[-1]
    b = pl.program_id(0) + blk0

    @pl.when(b < n_used_ref[0])
    def _():
        x = x_ref[...]
        a = jnp.dot(x, wg_ref[0], preferred_element_type=F32)
        u = jnp.dot(x, wu_ref[0], preferred_element_type=F32)
        hid = (a * jax.nn.sigmoid(a) * u).astype(BF16)
        o_ref[...] = jnp.dot(hid, wd_ref[0], preferred_element_type=F32).astype(o_ref.dtype)

    @pl.when(b >= n_used_ref[0])
    def _():
        o_ref[...] = jnp.zeros_like(o_ref)


def _experts(blk_e, n_used, xs, wg, wu, wd, y_prev, blk0, nblk):
    rows, D = xs.shape
    de = wg.shape[2]
    in_specs = [pl.BlockSpec((MOE_ROWS, D), lambda b, be, nu: (b, 0)),
                pl.BlockSpec((1, D, de), lambda b, be, nu: (be[b + blk0], 0, 0)),
                pl.BlockSpec((1, D, de), lambda b, be, nu: (be[b + blk0], 0, 0)),
                pl.BlockSpec((1, de, D), lambda b, be, nu: (be[b + blk0], 0, 0))]
    args = [blk_e, n_used, xs, wg, wu, wd]
    aliases = {}
    if y_prev is not None:
        in_specs.append(pl.BlockSpec(memory_space=pl.ANY))
        args.append(y_prev)
        aliases = {len(args) - 1: 0}
    grid_spec = pltpu.PrefetchScalarGridSpec(
        num_scalar_prefetch=2,
        grid=(rows // MOE_ROWS,),
        in_specs=in_specs,
        out_specs=pl.BlockSpec((MOE_ROWS, D), lambda b, be, nu: (b + blk0, 0)),
    )
    return pl.pallas_call(
        functools.partial(_expert_kernel, blk0=blk0),
        out_shape=jax.ShapeDtypeStruct((nblk * MOE_ROWS, D), BF16),
        grid_spec=grid_spec,
        input_output_aliases=aliases,
        compiler_params=_cparams(("arbitrary",)),
        name="experts",
    )(*args)


def _combine_kernel(h_ref, ya_ref, yb_ref, meta_ref, g_ref, b_ref, *rest, alpha):
    o_ref, ob_ref = rest[-2:]
    meta = meta_ref[...]
    g1 = meta[:, ROUTE_GATE:ROUTE_GATE + 1]
    g2 = meta[:, ROUTE_GATE + 1:ROUTE_GATE + 2]
    y = ya_ref[...].astype(F32) * g1 + yb_ref[...].astype(F32) * g2
    h2 = _ln_rows(alpha * h_ref[...] + y, g_ref[...], b_ref[...])
    o_ref[...] = h2
    ob_ref[...] = h2.astype(BF16)


def _combine(h, ya, yb, meta, g, b, alpha, prev, row0, tm=512):
    T, D = h.shape
    t0 = row0 // tm
    assert row0 % tm == 0 and ya.shape[0] % tm == 0
    chunk_rows = pl.BlockSpec((tm, D), lambda i: (i, 0))
    full_rows = pl.BlockSpec((tm, D), lambda i: (i + t0, 0))
    vec = pl.BlockSpec((1, D), lambda i: (0, 0))
    in_specs = [full_rows, chunk_rows, chunk_rows, pl.BlockSpec((tm, LANES), lambda i: (i + t0, 0)), vec, vec]
    args = [h, ya, yb, meta, g.reshape(1, D), b.reshape(1, D)]
    aliases = {}
    if prev is not None:
        in_specs += [pl.BlockSpec(memory_space=pl.ANY)] * 2
        aliases = {len(args): 0, len(args) + 1: 1}
        args += list(prev)
    return pl.pallas_call(
        functools.partial(_combine_kernel, alpha=alpha),
        out_shape=(jax.ShapeDtypeStruct((T, D), F32), jax.ShapeDtypeStruct((T, D), BF16)),
        grid=(ya.shape[0] // tm,),
        in_specs=in_specs,
        out_specs=(full_rows, full_rows),
        input_output_aliases=aliases,
        compiler_params=_cparams(("parallel",)),
        name="combine_ln2",
    )(*args)


def _moe(h1, h1b, logits, w_eg, w_eu, w_ed, e_off, g, b, alpha):
    T, D = h1.shape
    meta, counts = _route(logits)
    eid = meta[:, ROUTE_EID:ROUTE_EID + 2].astype(jnp.int32)
    rank = meta[:, ROUTE_RANK:ROUTE_RANK + 2].astype(jnp.int32)
    cnt = counts[0, :N_EXPERTS].astype(jnp.int32)
    padded = (cnt + MOE_ROWS - 1) // MOE_ROWS * MOE_ROWS
    pad_end = jnp.cumsum(padded)
    pad_off = pad_end - padded
    dest = pad_off[eid] + rank
    nblk = -(-(2 * T) // MOE_ROWS) + N_EXPERTS
    P = nblk * MOE_ROWS
    tok = jnp.broadcast_to(jnp.arange(T, dtype=jnp.int32)[:, None], (T, 2))
    buf_tok = jnp.zeros((P,), jnp.int32).at[dest.reshape(-1)].set(tok.reshape(-1), unique_indices=True)
    blk_start = jnp.arange(nblk, dtype=jnp.int32) * MOE_ROWS
    blk_e = jnp.minimum(jnp.sum((pad_end[None, :] <= blk_start[:, None]).astype(jnp.int32), axis=1), N_EXPERTS - 1)
    blk_e = blk_e + e_off
    n_used = (pad_end[-1:] // MOE_ROWS).astype(jnp.int32)
    assert nblk % MOE_CHUNKS == 0 and T % MOE_CHUNKS == 0
    cb = nblk // MOE_CHUNKS
    y = None
    for c in range(MOE_CHUNKS):
        xs = h1b.at[buf_tok[c * cb * MOE_ROWS:(c + 1) * cb * MOE_ROWS]].get(mode='promise_in_bounds')
        y = _experts(blk_e, n_used, xs, w_eg, w_eu, w_ed, y, c * cb, nblk)
    ct = T // MOE_CHUNKS
    out = None
    for c in range(MOE_CHUNKS):
        ya = y.at[dest[c * ct:(c + 1) * ct, 0]].get(mode='promise_in_bounds')
        yb = y.at[dest[c * ct:(c + 1) * ct, 1]].get(mode='promise_in_bounds')
        out = _combine(h1, ya, yb, meta, g, b, alpha, out, c * ct)
    return out


def kernel(x, ln0_g, ln0_b, rel_bias, w_in, b_gate, sink_a, rpb_c, w_br_a, w_br_b, w_br_c, w_out,
           ln1_g, ln1_b, w_rg, b_rg, w_re, b_re, w_eg, w_eu, w_ed, ln2_g, ln2_b):
    Bn, L, D = x.shape
    depth = w_in.shape[0]
    T = Bn * L
    alpha = float((2 * depth) ** 0.25)
    a_order = np.asarray(A_HEAD_ORDER)

    a_pairs = [(c, c + A_HEADS // 2) for c in range(A_HEADS // 2)]
    b_pairs = [(2 * c, 2 * c + 1) for c in range(B_HEADS_PER_GROUP // 2)]
    bias_a = _band_bias(rel_bias[:, :A_HEADS], a_pairs, A_HALF_WINDOW, 1)
    bias_b = []
    for g, (window, dilation) in enumerate(B_CONFIGS):
        c0 = A_HEADS + g * B_HEADS_PER_GROUP
        bias_b.append(_band_bias(rel_bias[:, c0:c0 + B_HEADS_PER_GROUP], b_pairs, window // (2 * dilation), dilation))

    w_eg_b = w_eg.astype(BF16).reshape((depth * N_EXPERTS,) + w_eg.shape[2:])
    w_eu_b = w_eu.astype(BF16).reshape((depth * N_EXPERTS,) + w_eu.shape[2:])
    w_ed_b = w_ed.astype(BF16).reshape((depth * N_EXPERTS,) + w_ed.shape[2:])

    h, hb = _layer_norm(x.reshape(T, D), ln0_g, ln0_b)
    for l in range(depth):
        w_gate = w_in[l, :, QKV_COLS:].astype(BF16)
        arr_a, arr_b0, arr_b1, arr_b2, arr_c = _in_projection(hb, _inproj_weight(w_in[l]), Bn, L)

        (o_a,) = _banded_attention(arr_a, bias_a, sink_a[l][a_order], tq=512, n_pairs=A_HEADS // 2, shared_kv=True,
                                   half=A_HALF_WINDOW, with_lse=False)
        o_bs, lses = [], []
        for g, (arr, tq) in enumerate(((arr_b0, 512), (arr_b1, 512), (arr_b2, 256))):
            window, dilation = B_CONFIGS[g]
            o, lse = _banded_attention(arr, bias_b[g], None, tq=tq, n_pairs=B_HEADS_PER_GROUP // 2, shared_kv=False,
                                       half=window // (2 * dilation), with_lse=True)
            o_bs.append(o)
            lses.append(lse)
        o_c = _neighborhood_attention(arr_c, _nbr_bias(rpb_c[l]), Bn, L)

        w_r = jnp.concatenate([w_rg[l], jnp.transpose(w_re[l], (1, 0, 2)).reshape(D, N_EXPERTS)], axis=1)
        w_r = jnp.pad(w_r, ((0, 0), (0, LANES - w_r.shape[1]))).astype(BF16)
        b_r = jnp.concatenate([b_rg[l], b_re[l].reshape(-1)]).astype(F32)
        b_r = jnp.pad(b_r, (0, LANES - b_r.shape[0])).reshape(1, LANES)
        w_a = w_br_a[l].reshape(A_HEADS, HEAD_DIM, D)[a_order].reshape(A_Q, D)

        h1, h1b, logits = _post_attention(
            h, hb, o_a.reshape(T, A_Q), o_bs, lses, o_c,
            w_gate, b_gate[l].reshape(1, N_BRANCH * D).astype(F32),
            w_a.astype(BF16), w_br_b[l].astype(BF16), w_br_c[l].astype(BF16), w_out[l].astype(BF16),
            ln1_g[l].reshape(1, D), ln1_b[l].reshape(1, D), w_r, b_r, alpha)

        h, hb = _moe(h1, h1b, logits, w_eg_b, w_eu_b, w_ed_b, l * N_EXPERTS, ln2_g[l], ln2_b[l], alpha)
    return h.reshape(Bn, L, D)
```

```python
import functools

import numpy as np
import jax
import jax.numpy as jnp
from jax import lax
from jax.experimental import pallas as pl
from jax.experimental.pallas import tpu as pltpu

F32 = jnp.float32
BF16 = jnp.bfloat16
HIGHEST = lax.Precision.HIGHEST

HEAD_DIM = 64
A_HEADS = 8
A_KV_HEADS = 2
A_HALF_WINDOW = 128
B_HEADS_PER_GROUP = 4
B_CONFIGS = ((128, 1), (512, 4), (2048, 16))
C_HEADS = 8
GRID_W = 64
NA_KH = 8
NA_KW = 16
NUM_BUCKETS = 32
REL_MAX_DIST = 2048
N_GROUPS = 4
EXPERTS_PER_GROUP = 8
N_EXPERTS = N_GROUPS * EXPERTS_PER_GROUP
D_EXPERT = 512
LN_EPS = 1e-5
NEG_INF = -1e30

A_Q = A_HEADS * HEAD_DIM
A_KV = A_KV_HEADS * HEAD_DIM
B_W = B_HEADS_PER_GROUP * HEAD_DIM
C_W = C_HEADS * HEAD_DIM
N_BRANCH = 3
QKV_COLS = A_Q + 2 * A_KV + 3 * len(B_CONFIGS) * B_W + 3 * C_W
B_COL0 = A_Q + 2 * A_KV
C_COL0 = B_COL0 + 3 * len(B_CONFIGS) * B_W

ATT_BLK = 128
LANES = 128
MOE_ROWS = 256
MOE_CHUNKS = 8
VMEM_LIMIT = 56 * 1024 * 1024

A_HEAD_ORDER = tuple(h for c in range(A_HEADS // 2) for h in (c, c + A_HEADS // 2))


def _cparams(sem):
    return pltpu.CompilerParams(dimension_semantics=sem, vmem_limit_bytes=VMEM_LIMIT)


def _ln_rows(x, g, b):
    mu = jnp.mean(x, axis=-1, keepdims=True)
    xc = x - mu
    var = jnp.mean(xc * xc, axis=-1, keepdims=True)
    return xc * lax.rsqrt(var + LN_EPS) * g + b


def _ln_kernel(x_ref, g_ref, b_ref, o_ref, ob_ref):
    y = _ln_rows(x_ref[...].astype(F32), g_ref[...], b_ref[...])
    o_ref[...] = y
    ob_ref[...] = y.astype(BF16)


def _layer_norm(x, g, b, tm=512):
    T, D = x.shape
    return pl.pallas_call(
        _ln_kernel,
        out_shape=(jax.ShapeDtypeStruct((T, D), F32), jax.ShapeDtypeStruct((T, D), BF16)),
        grid=(T // tm,),
        in_specs=[pl.BlockSpec((tm, D), lambda i: (i, 0)),
                  pl.BlockSpec((1, D), lambda i: (0, 0)),
                  pl.BlockSpec((1, D), lambda i: (0, 0))],
        out_specs=(pl.BlockSpec((tm, D), lambda i: (i, 0)), pl.BlockSpec((tm, D), lambda i: (i, 0))),
        compiler_params=_cparams(("parallel",)),
        name="ln0",
    )(x, g.reshape(1, D), b.reshape(1, D))


BAND_W = 3 * B_W
NBR_W = 3 * C_W


def _inproj_weight(w_in_l):
    scale = HEAD_DIM ** -0.5
    a_q = w_in_l[:, :A_Q].reshape(-1, A_HEADS, HEAD_DIM)[:, np.asarray(A_HEAD_ORDER)].reshape(-1, A_Q)
    cols = [w_in_l[:, A_Q:A_Q + 2 * A_KV], a_q * scale]
    for g in range(len(B_CONFIGS)):
        c0 = B_COL0 + 3 * g * B_W
        cols += [w_in_l[:, c0 + B_W:c0 + 3 * B_W], w_in_l[:, c0:c0 + B_W] * scale]
    cols += [w_in_l[:, C_COL0 + C_W:C_COL0 + 3 * C_W], w_in_l[:, C_COL0:C_COL0 + C_W] * scale]
    return jnp.concatenate(cols, axis=1).astype(BF16)


def _inproj_kernel(h_ref, w_ref, a_ref, b0_ref, b1_ref, b2_ref, c_ref, tmp_ref):
    h = h_ref[...]
    tm = h.shape[0]

    def chunk(i):
        return jnp.dot(h, w_ref[:, i * BAND_W:(i + 1) * BAND_W], preferred_element_type=F32)

    a_ref[0, 0] = chunk(0).astype(BF16)
    b0_ref[0, 0] = chunk(1).astype(BF16)
    for i, ref in ((2, b1_ref), (3, b2_ref)):
        d = ref.shape[1]
        res = chunk(i)
        n_cols = BAND_W // LANES
        for c in range(n_cols):
            tmp_ref[c] = res[:, c * LANES:(c + 1) * LANES]
        for r in range(d):
            for c in range(n_cols):
                ref[0, r, :, c * LANES:(c + 1) * LANES] = tmp_ref[c, pl.ds(r, tm // d, stride=d), :].astype(BF16)
    c_ref[:, :BAND_W] = chunk(4).astype(BF16)
    c_ref[:, BAND_W:] = chunk(5).astype(BF16)


def _in_projection(hb, w, Bn, L, tm=512):
    T, D = hb.shape
    nt = L // tm
    assert L % tm == 0 and NBR_W == 2 * BAND_W and A_Q + 2 * A_KV == BAND_W
    dils = [d for _, d in B_CONFIGS]
    assert dils[0] == 1 and len(dils) == 3

    def band_spec(d):
        return pl.BlockSpec((1, d, tm // d, BAND_W), lambda b, i: (b, 0, i, 0))

    return pl.pallas_call(
        _inproj_kernel,
        out_shape=(jax.ShapeDtypeStruct((Bn, 1, L, BAND_W), BF16),
                   jax.ShapeDtypeStruct((Bn, 1, L, BAND_W), BF16),
                   jax.ShapeDtypeStruct((Bn, dils[1], L // dils[1], BAND_W), BF16),
                   jax.ShapeDtypeStruct((Bn, dils[2], L // dils[2], BAND_W), BF16),
                   jax.ShapeDtypeStruct((T, NBR_W), BF16)),
        grid=(Bn, nt),
        in_specs=[pl.BlockSpec((tm, D), lambda b, i: (b * nt + i, 0)),
                  pl.BlockSpec(w.shape, lambda b, i: (0, 0))],
        out_specs=(band_spec(1), band_spec(1), band_spec(dils[1]), band_spec(dils[2]),
                   pl.BlockSpec((tm, NBR_W), lambda b, i: (b * nt + i, 0))),
        scratch_shapes=[pltpu.VMEM((BAND_W // LANES, tm, LANES), F32)],
        compiler_params=_cparams(("parallel", "parallel")),
        name="in_proj",
    )(hb, w)


def _t5_bucket(rel):
    half = NUM_BUCKETS // 2
    max_exact = half // 2
    ret = np.where(rel > 0, half, 0)
    n = np.abs(rel)
    large = max_exact + (np.log(np.maximum(n, max_exact) / max_exact)
                         / np.log(REL_MAX_DIST / max_exact) * (half - max_exact)).astype(np.int32)
    large = np.minimum(large, half - 1)
    return (ret + np.where(n < max_exact, n, large)).astype(np.int32)


def _one_hot_t(idx, n):
    return jnp.asarray((np.arange(n)[:, None] == np.asarray(idx).reshape(1, -1)).astype(np.float32))


def _band_bias(bias_tab, head_pairs, half, dist_scale):
    kl = ATT_BLK + 2 * half
    off = np.arange(kl)[None, :] - half - np.arange(ATT_BLK)[:, None]
    band = np.abs(off) <= half
    onehot = _one_hot_t(_t5_bucket(off * dist_scale), NUM_BUCKETS)
    order = np.asarray([h for pair in head_pairs for h in pair])
    bias = jnp.dot(bias_tab.astype(F32)[:, order].T, onehot, precision=HIGHEST)
    bias = bias.reshape(len(head_pairs), 2 * ATT_BLK, kl)
    col = np.arange(kl)
    first_ok = (col >= half)[None, :]
    last_ok = (col < ATT_BLK + half)[None, :]
    out = []
    for first, last in ((0, 0), (1, 0), (0, 1), (1, 1)):
        ok = band & (first_ok if first else True) & (last_ok if last else True)
        ok = np.concatenate([ok, ok], axis=0)
        out.append(jnp.where(jnp.asarray(ok)[None], bias, NEG_INF))
    return jnp.stack(out)


def _pair_rows(x):
    lo = lax.broadcasted_iota(jnp.int32, x.shape, 1) < HEAD_DIM
    zero = jnp.zeros_like(x)
    return jnp.concatenate([jnp.where(lo, x, zero), jnp.where(lo, zero, x)], axis=0)


def _unpair_rows(x2):
    rows = x2.shape[0] // 2
    lo = lax.broadcasted_iota(jnp.int32, (rows, LANES), 1) < HEAD_DIM
    return jnp.where(lo, x2[:rows], x2[rows:])


def _banded_kernel(*refs, n_sub, sub_per_iter, half, kv_cols, n_pairs, shared_kv, has_sink, with_lse, dilation):
    cur_ref, prev_ref, next_ref, bias_ref = refs[:4]
    pos = 4
    sink_ref = None
    if has_sink:
        sink_ref = refs[pos]
        pos += 1
    o_ref = refs[pos]
    pos += 1
    lse_ref = None
    if with_lse:
        lse_ref = refs[pos]
        pos += 1
    kv_s, s_s, p_s = refs[pos:pos + 3]

    tq = n_sub * ATT_BLK
    kl = ATT_BLK + 2 * half
    kvw = kv_cols * LANES
    d = dilation
    j = pl.program_id(1)
    last_j = pl.num_programs(1) - 1
    row_is_lo = lax.broadcasted_iota(jnp.int32, (2 * ATT_BLK, 1), 0) < ATT_BLK

    def one_residue(r, carry):
        kv_s[0:half, :] = prev_ref[0, r]
        kv_s[half:half + tq, :] = cur_ref[0, r, :, :2 * kvw]
        kv_s[half + tq:, :] = next_ref[0, r]

        def sub_blocks(it, carry2):
            units = [(ii, c) for ii in range(sub_per_iter) for c in range(n_pairs)]
            row0s, variants = [], []
            for ii in range(sub_per_iter):
                i = it * sub_per_iter + ii
                row0s.append(pl.multiple_of(i * ATT_BLK, ATT_BLK))
                is_first = jnp.logical_and(i == 0, j == 0).astype(jnp.int32)
                is_last = jnp.logical_and(i == n_sub - 1, j == last_j).astype(jnp.int32)
                variants.append(is_first + 2 * is_last)

            for u, (ii, c) in enumerate(units):
                kc = 0 if shared_kv else c
                q2 = _pair_rows(cur_ref[0, r, pl.ds(row0s[ii], ATT_BLK),
                                        2 * kvw + c * LANES:2 * kvw + (c + 1) * LANES])
                k = kv_s[pl.ds(row0s[ii], kl), kc * LANES:(kc + 1) * LANES]
                s_s[u] = lax.dot_general(q2, k, (((1,), (1,)), ((), ())), preferred_element_type=F32)

            stats = []
            for u, (ii, c) in enumerate(units):
                s = s_s[u] + bias_ref[variants[ii], c]
                m = jnp.max(s, axis=-1, keepdims=True)
                if has_sink:
                    sk = jnp.where(row_is_lo, sink_ref[2 * c], sink_ref[2 * c + 1])
                    m = jnp.maximum(m, sk)
                p = jnp.exp(s - m)
                denom = jnp.sum(p, axis=-1, keepdims=True)
                if has_sink:
                    denom = denom + jnp.exp(sk - m)
                p_s[u] = p.astype(BF16)
                stats.append((m, denom))

            for ii in range(sub_per_iter):
                o_cols, lse_cols = [], []
                for c in range(n_pairs):
                    u = ii * n_pairs + c
                    m, denom = stats[u]
                    kc = 0 if shared_kv else c
                    v = kv_s[pl.ds(row0s[ii], kl), kvw + kc * LANES:kvw + (kc + 1) * LANES]
                    o2 = jnp.dot(p_s[u], v, preferred_element_type=F32) / denom
                    o_cols.append(_unpair_rows(o2))
                    if with_lse:
                        lse_cols.append(_unpair_rows(jnp.broadcast_to(m + jnp.log(denom), (2 * ATT_BLK, LANES))))
                if d == 1:
                    rows = pl.ds(row0s[ii], ATT_BLK)
                else:
                    rows = pl.ds(row0s[ii] * d + r, ATT_BLK, stride=d)
                if with_lse:
                    for c in range(n_pairs):
                        o_ref[0, c, rows, :] = o_cols[c].astype(o_ref.dtype)
                        lse_ref[0, c, rows, :] = lse_cols[c]
                else:
                    o_ref[0, rows, :] = jnp.concatenate(o_cols, axis=-1).astype(o_ref.dtype)
            return carry2

        lax.fori_loop(0, n_sub // sub_per_iter, sub_blocks, 0)
        return carry

    lax.fori_loop(0, d, one_residue, 0)


def _banded_attention(arr, bias, sink, *, tq, n_pairs, shared_kv, half, with_lse):
    Bn, d, Ls, W = arr.shape
    L = d * Ls
    qw = n_pairs * LANES
    kv_cols = (W - qw) // (2 * LANES)
    kvw = kv_cols * LANES
    tq = min(tq, Ls)
    n_sub = tq // ATT_BLK
    nbt = Ls // tq
    sub_per_iter = max(k for k in (1, 2, 4) if n_sub % k == 0)
    n_units = sub_per_iter * n_pairs
    halo_per_tile = tq // half
    n_halo = Ls // half
    assert Ls % tq == 0 and tq % half == 0 and kvw * 2 + qw == W
    o_dtype = BF16 if d == 1 else F32

    in_specs = [
        pl.BlockSpec((1, d, tq, W), lambda b, j: (b, 0, j, 0)),
        pl.BlockSpec((1, d, half, 2 * kvw), lambda b, j: (b, 0, jnp.maximum(j * halo_per_tile - 1, 0), 0)),
        pl.BlockSpec((1, d, half, 2 * kvw),
                     lambda b, j: (b, 0, jnp.minimum((j + 1) * halo_per_tile, n_halo - 1), 0)),
        pl.BlockSpec(bias.shape, lambda b, j: (0, 0, 0, 0)),
    ]
    args = [arr, arr, arr, bias]
    if sink is not None:
        in_specs.append(pl.BlockSpec(memory_space=pltpu.SMEM))
        args.append(sink.astype(F32))
    if with_lse:
        out_block = pl.BlockSpec((1, n_pairs, tq * d, LANES), lambda b, j: (b, 0, j, 0))
        out_shape = [jax.ShapeDtypeStruct((Bn, n_pairs, L, LANES), o_dtype),
                     jax.ShapeDtypeStruct((Bn, n_pairs, L, LANES), F32)]
        out_specs = [out_block, out_block]
    else:
        assert d == 1
        out_shape = [jax.ShapeDtypeStruct((Bn, L, qw), o_dtype)]
        out_specs = [pl.BlockSpec((1, tq, qw), lambda b, j: (b, j, 0))]
    return pl.pallas_call(
        functools.partial(_banded_kernel, n_sub=n_sub, sub_per_iter=sub_per_iter, half=half, kv_cols=kv_cols,
                          n_pairs=n_pairs, shared_kv=shared_kv, has_sink=sink is not None, with_lse=with_lse,
                          dilation=d),
        out_shape=tuple(out_shape),
        grid=(Bn, nbt),
        in_specs=in_specs,
        out_specs=tuple(out_specs),
        scratch_shapes=[pltpu.VMEM((tq + 2 * half, 2 * kvw), BF16),
                        pltpu.VMEM((n_units, 2 * ATT_BLK, ATT_BLK + 2 * half), F32),
                        pltpu.VMEM((n_units, 2 * ATT_BLK, ATT_BLK + 2 * half), BF16)],
        compiler_params=_cparams(("parallel", "arbitrary")),
        name=f"banded_attn_d{d}_h{half}",
    )(*args)


NBR_ROWS_PER_STEP = 8
NBR_KEY_BLOCK_ROWS = 4
NBR_ROWS_PER_ITER = 8


def _nbr_bias(rpb):
    H = rpb.shape[0]
    c = np.arange(GRID_W)
    qstart = np.clip(c - NA_KW // 2, 0, GRID_W - NA_KW)
    kc = np.arange(GRID_W)
    valid = (kc[None, :] >= qstart[:, None]) & (kc[None, :] < qstart[:, None] + NA_KW)
    cidx = np.clip(kc[None, :] - c[:, None] + NA_KW - 1, 0, 2 * NA_KW - 2)
    rows = jnp.stack([rpb.astype(F32)[:, NA_KH - 1 - v:2 * NA_KH - 1 - v] for v in range(NA_KH)])
    col_sel = _one_hot_t(cidx, 2 * NA_KW - 1)
    tab = jnp.dot(rows.reshape(-1, 2 * NA_KW - 1), col_sel, precision=HIGHEST)
    tab = tab.reshape(NA_KH, H, NA_KH, GRID_W, GRID_W).transpose(0, 1, 3, 2, 4)
    tab = jnp.where(jnp.asarray(valid)[None, None, :, None, :], tab, NEG_INF)
    return tab.reshape(NA_KH, H // 2, 2 * GRID_W, NA_KH * GRID_W)


def _nbr_kernel(q_ref, kv0, kv1, kv2, kv3, bias_ref, o_ref, kv_s, s_s, p_s, *, rows):
    blk = NBR_KEY_BLOCK_ROWS * GRID_W
    for t, ref in enumerate((kv0, kv1, kv2, kv3)):
        kv_s[t * blk:(t + 1) * blk, :] = ref[...]
    i0 = pl.program_id(1) * NBR_ROWS_PER_STEP
    nkeys = NA_KH * GRID_W
    n_pairs = C_HEADS // 2

    def row_group(it, carry):
        offs, variants, q0s = [], [], []
        for rr in range(NBR_ROWS_PER_ITER):
            r = it * NBR_ROWS_PER_ITER + rr
            i = i0 + r
            rstart = jnp.clip(i - NA_KH // 2, 0, rows - NA_KH)
            offs.append(pl.multiple_of((rstart - i0 + NBR_KEY_BLOCK_ROWS) * GRID_W, GRID_W))
            variants.append(i - rstart)
            q0s.append(pl.multiple_of(r * GRID_W, GRID_W))
        units = [(rr, c) for rr in range(NBR_ROWS_PER_ITER) for c in range(n_pairs)]

        for u, (rr, c) in enumerate(units):
            q2 = _pair_rows(q_ref[pl.ds(q0s[rr], GRID_W), c * LANES:(c + 1) * LANES])
            k = kv_s[pl.ds(offs[rr], nkeys), c * LANES:(c + 1) * LANES]
            s_s[u] = lax.dot_general(q2, k, (((1,), (1,)), ((), ())), preferred_element_type=F32)

        denoms = []
        for u, (rr, c) in enumerate(units):
            s = s_s[u] + bias_ref[variants[rr], c]
            m = jnp.max(s, axis=-1, keepdims=True)
            p = jnp.exp(s - m)
            denoms.append(jnp.sum(p, axis=-1, keepdims=True))
            p_s[u] = p.astype(BF16)

        for rr in range(NBR_ROWS_PER_ITER):
            o_cols = []
            for c in range(n_pairs):
                u = rr * n_pairs + c
                v = kv_s[pl.ds(offs[rr], nkeys), C_W + c * LANES:C_W + (c + 1) * LANES]
                o2 = jnp.dot(p_s[u], v, preferred_element_type=F32) / denoms[u]
                o_cols.append(_unpair_rows(o2))
            o_ref[pl.ds(q0s[rr], GRID_W), :] = jnp.concatenate(o_cols, axis=-1).astype(o_ref.dtype)
        return carry

    lax.fori_loop(0, NBR_ROWS_PER_STEP // NBR_ROWS_PER_ITER, row_group, 0)


def _neighborhood_attention(arr, bias, Bn, L):
    rows = L // GRID_W
    assert rows % NBR_ROWS_PER_STEP == 0 and rows >= 2 * NBR_ROWS_PER_STEP
    tq = NBR_ROWS_PER_STEP * GRID_W
    kb = NBR_KEY_BLOCK_ROWS * GRID_W
    nkb = L // kb
    per_step = NBR_ROWS_PER_STEP // NBR_KEY_BLOCK_ROWS
    nq = L // tq

    def key_spec(t):
        return pl.BlockSpec((kb, 2 * C_W), lambda b, j: (b * nkb + jnp.clip(j * per_step - 1 + t, 0, nkb - 1), 0))

    in_specs = ([pl.BlockSpec((tq, C_W), lambda b, j: (b * nq + j, 2))]
                + [key_spec(t) for t in range(4)]
                + [pl.BlockSpec(bias.shape, lambda b, j: (0, 0, 0, 0))])
    return pl.pallas_call(
        functools.partial(_nbr_kernel, rows=rows),
        out_shape=jax.ShapeDtypeStruct((Bn * L, C_W), BF16),
        grid=(Bn, nq),
        in_specs=in_specs,
        out_specs=pl.BlockSpec((tq, C_W), lambda b, j: (b * nq + j, 0)),
        scratch_shapes=[pltpu.VMEM((4 * kb, 2 * C_W), BF16),
                        pltpu.VMEM((NBR_ROWS_PER_ITER * C_HEADS // 2, 2 * GRID_W, NA_KH * GRID_W), F32),
                        pltpu.VMEM((NBR_ROWS_PER_ITER * C_HEADS // 2, 2 * GRID_W, NA_KH * GRID_W), BF16)],
        compiler_params=_cparams(("parallel", "arbitrary")),
        name="nbr_attn",
    )(*([arr] * 5), bias)


def _post_kernel(h_ref, hb_ref, oa_ref, ob0_ref, ob1_ref, ob2_ref, l0_ref, l1_ref, l2_ref, oc_ref,
                 wg_ref, bg_ref, wa_ref, wb_ref, wc_ref, wo_ref, g_ref, b_ref, wr_ref, br_ref,
                 o_ref, obf_ref, logit_ref, *, alpha):
    D = h_ref.shape[1]
    hb = hb_ref[...]
    def planes(ref):
        return jnp.concatenate([ref[0, c] for c in range(ref.shape[1])], axis=-1).astype(F32)

    l0, l1, l2 = planes(l0_ref), planes(l1_ref), planes(l2_ref)
    lm = jnp.maximum(jnp.maximum(l0, l1), l2)
    e0, e1, e2 = jnp.exp(l0 - lm), jnp.exp(l1 - lm), jnp.exp(l2 - lm)
    inv = 1.0 / (e0 + e1 + e2)
    o_b = ((e0 * planes(ob0_ref) + e1 * planes(ob1_ref) + e2 * planes(ob2_ref)) * inv).astype(BF16)
    merged = None
    for idx, (o_br, w_ref) in enumerate(((oa_ref[...], wa_ref), (o_b, wb_ref), (oc_ref[...], wc_ref))):
        z = (jnp.dot(hb, wg_ref[:, idx * D:(idx + 1) * D], preferred_element_type=F32)
             + bg_ref[:, idx * D:(idx + 1) * D])
        gate = 0.5 * jnp.tanh(0.5 * z) + 0.5
        term = gate * jnp.dot(o_br, w_ref[...], preferred_element_type=F32)
        merged = term if merged is None else merged + term
    y = jnp.dot(merged.astype(BF16), wo_ref[...], preferred_element_type=F32)
    h1 = _ln_rows(alpha * h_ref[...] + y, g_ref[...], b_ref[...])
    h1b = h1.astype(BF16)
    o_ref[...] = h1
    obf_ref[...] = h1b
    logit_ref[...] = jnp.dot(h1b, wr_ref[...], preferred_element_type=F32) + br_ref[...]


def _post_attention(h, hb, o_a, o_bs, lses, o_c, wg, bg, wa, wb, wc, wo, g, b, wr, br, alpha, tm=256):
    T, D = h.shape
    n_planes, L = o_bs[0].shape[1:3]
    nt = L // tm
    assert L % tm == 0

    def rows(w):
        return pl.BlockSpec((tm, w), lambda i: (i, 0))

    plane_rows = pl.BlockSpec((1, n_planes, tm, LANES), lambda i: (i // nt, 0, i % nt, 0))

    def full(a):
        return pl.BlockSpec(a.shape, lambda i: (0, 0))

    weights = [wg, bg, wa, wb, wc, wo, g, b, wr, br]
    return pl.pallas_call(
        functools.partial(_post_kernel, alpha=alpha),
        out_shape=(jax.ShapeDtypeStruct((T, D), F32), jax.ShapeDtypeStruct((T, D), BF16),
                   jax.ShapeDtypeStruct((T, LANES), F32)),
        grid=(T // tm,),
        in_specs=[rows(D), rows(D), rows(A_Q)] + [plane_rows] * 6 + [rows(C_W)] + [full(w) for w in weights],
        out_specs=(rows(D), rows(D), rows(LANES)),
        compiler_params=_cparams(("parallel",)),
        name="post_attn",
    )(h, hb, o_a, *o_bs, *lses, o_c, *weights)


ROUTE_EID, ROUTE_GATE, ROUTE_RANK = 0, 2, 4


def _route_kernel(logit_ref, meta_ref, count_ref, run_ref):
    tm = logit_ref.shape[0]

    @pl.when(pl.program_id(0) == 0)
    def _():
        run_ref[...] = jnp.zeros_like(run_ref)

    x = logit_ref[...]
    lane = lax.broadcasted_iota(jnp.int32, x.shape, 1).astype(F32)
    big = float(LANES)

    def lane_max(v):
        return jnp.max(v, axis=-1, keepdims=True)

    def first_lane(mask):
        return jnp.min(jnp.where(mask, lane, big), axis=-1, keepdims=True)

    is_g = lane < N_GROUPS
    gl = jnp.where(is_g, x, NEG_INF)
    gmax = lane_max(gl)
    g_idx = first_lane(jnp.logical_and(is_g, gl == gmax))
    g_w = 1.0 / jnp.sum(jnp.where(is_g, jnp.exp(gl - gmax), 0.0), axis=-1, keepdims=True)

    e_lo = N_GROUPS + g_idx * EXPERTS_PER_GROUP
    in_grp = jnp.logical_and(lane >= e_lo, lane < e_lo + EXPERTS_PER_GROUP)
    el = jnp.where(in_grp, x, NEG_INF)
    top1 = lane_max(el)
    lane1 = first_lane(jnp.logical_and(in_grp, el == top1))
    el2 = jnp.where(lane == lane1, NEG_INF, el)
    top2 = lane_max(el2)
    lane2 = first_lane(jnp.logical_and(in_grp, jnp.logical_and(lane != lane1, el2 == top2)))
    e2 = jnp.exp(top2 - top1)
    w1 = 1.0 / (1.0 + e2)
    w2 = e2 / (1.0 + e2)

    eid1 = lane1 - N_GROUPS
    eid2 = lane2 - N_GROUPS
    onehot = jnp.logical_or(lane == eid1, lane == eid2)
    oh = jnp.where(onehot, 1.0, 0.0).astype(BF16)
    r_i = lax.broadcasted_iota(jnp.int32, (tm, tm), 0)
    c_i = lax.broadcasted_iota(jnp.int32, (tm, tm), 1)
    strict_lower = jnp.where(c_i < r_i, 1.0, 0.0).astype(BF16)
    before = jnp.dot(strict_lower, oh, preferred_element_type=F32) + run_ref[...]
    rank1 = jnp.sum(jnp.where(lane == eid1, before, 0.0), axis=-1, keepdims=True)
    rank2 = jnp.sum(jnp.where(lane == eid2, before, 0.0), axis=-1, keepdims=True)
    run_ref[...] = run_ref[...] + jnp.sum(oh.astype(F32), axis=0, keepdims=True)

    meta = jnp.zeros(x.shape, F32)
    for k, val in ((ROUTE_EID, eid1), (ROUTE_EID + 1, eid2),
                   (ROUTE_GATE, g_w * w1), (ROUTE_GATE + 1, g_w * w2),
                   (ROUTE_RANK, rank1), (ROUTE_RANK + 1, rank2)):
        meta = jnp.where(lane == k, val, meta)
    meta_ref[...] = meta
    count_ref[...] = run_ref[...]


def _route(logits, tm=512):
    T = logits.shape[0]
    return pl.pallas_call(
        _route_kernel,
        out_shape=(jax.ShapeDtypeStruct((T, LANES), F32), jax.ShapeDtypeStruct((1, LANES), F32)),
        grid=(T // tm,),
        in_specs=[pl.BlockSpec((tm, LANES), lambda i: (i, 0))],
        out_specs=(pl.BlockSpec((tm, LANES), lambda i: (i, 0)), pl.BlockSpec((1, LANES), lambda i: (0, 0))),
        scratch_shapes=[pltpu.VMEM((1, LANES), F32)],
        compiler_params=_cparams(("arbitrary",)),
        name="route",
    )(logits)


def _expert_kernel(blk_e_ref, n_used_ref, x_ref, wg_ref, wu_ref, wd_ref, *rest, blk0):
    o_ref = rest[-1]
    b = pl.program_id(0) + blk0

    @pl.when(b < n_used_ref[0])
    def _():
        x = x_ref[...]
        a = jnp.dot(x, wg_ref[0], preferred_element_type=F32)
        u = jnp.dot(x, wu_ref[0], preferred_element_type=F32)
        hid = (a * jax.nn.sigmoid(a) * u).astype(BF16)
        o_ref[...] = jnp.dot(hid, wd_ref[0], preferred_element_type=F32).astype(o_ref.dtype)

    @pl.when(b >= n_used_ref[0])
    def _():
        o_ref[...] = jnp.zeros_like(o_ref)


def _experts(blk_e, n_used, xs, wg, wu, wd, y_prev, blk0, nblk):
    rows, D = xs.shape
    de = wg.shape[2]
    in_specs = [pl.BlockSpec((MOE_ROWS, D), lambda b, be, nu: (b, 0)),
                pl.BlockSpec((1, D, de), lambda b, be, nu: (be[b + blk0], 0, 0)),
                pl.BlockSpec((1, D, de), lambda b, be, nu: (be[b + blk0], 0, 0)),
                pl.BlockSpec((1, de, D), lambda b, be, nu: (be[b + blk0], 0, 0))]
    args = [blk_e, n_used, xs, wg, wu, wd]
    aliases = {}
    if y_prev is not None:
        in_specs.append(pl.BlockSpec(memory_space=pl.ANY))
        args.append(y_prev)
        aliases = {len(args) - 1: 0}
    grid_spec = pltpu.PrefetchScalarGridSpec(
        num_scalar_prefetch=2,
        grid=(rows // MOE_ROWS,),
        in_specs=in_specs,
        out_specs=pl.BlockSpec((MOE_ROWS, D), lambda b, be, nu: (b + blk0, 0)),
    )
    return pl.pallas_call(
        functools.partial(_expert_kernel, blk0=blk0),
        out_shape=jax.ShapeDtypeStruct((nblk * MOE_ROWS, D), BF16),
        grid_spec=grid_spec,
        input_output_aliases=aliases,
        compiler_params=_cparams(("arbitrary",)),
        name="experts",
    )(*args)


def _combine_kernel(h_ref, ya_ref, yb_ref, meta_ref, g_ref, b_ref, *rest, alpha):
    o_ref, ob_ref = rest[-2:]
    meta = meta_ref[...]
    g1 = meta[:, ROUTE_GATE:ROUTE_GATE + 1]
    g2 = meta[:, ROUTE_GATE + 1:ROUTE_GATE + 2]
    y = ya_ref[...].astype(F32) * g1 + yb_ref[...].astype(F32) * g2
    h2 = _ln_rows(alpha * h_ref[...] + y, g_ref[...], b_ref[...])
    o_ref[...] = h2
    ob_ref[...] = h2.astype(BF16)


def _combine(h, ya, yb, meta, g, b, alpha, prev, row0, tm=512):
    T, D = h.shape
    t0 = row0 // tm
    assert row0 % tm == 0 and ya.shape[0] % tm == 0
    chunk_rows = pl.BlockSpec((tm, D), lambda i: (i, 0))
    full_rows = pl.BlockSpec((tm, D), lambda i: (i + t0, 0))
    vec = pl.BlockSpec((1, D), lambda i: (0, 0))
    in_specs = [full_rows, chunk_rows, chunk_rows, pl.BlockSpec((tm, LANES), lambda i: (i + t0, 0)), vec, vec]
    args = [h, ya, yb, meta, g.reshape(1, D), b.reshape(1, D)]
    aliases = {}
    if prev is not None:
        in_specs += [pl.BlockSpec(memory_space=pl.ANY)] * 2
        aliases = {len(args): 0, len(args) + 1: 1}
        args += list(prev)
    return pl.pallas_call(
        functools.partial(_combine_kernel, alpha=alpha),
        out_shape=(jax.ShapeDtypeStruct((T, D), F32), jax.ShapeDtypeStruct((T, D), BF16)),
        grid=(ya.shape[0] // tm,),
        in_specs=in_specs,
        out_specs=(full_rows, full_rows),
        input_output_aliases=aliases,
        compiler_params=_cparams(("parallel",)),
        name="combine_ln2",
    )(*args)


def _moe(h1, h1b, logits, w_eg, w_eu, w_ed, e_off, g, b, alpha):
    T, D = h1.shape
    meta, counts = _route(logits)
    eid = meta[:, ROUTE_EID:ROUTE_EID + 2].astype(jnp.int32)
    rank = meta[:, ROUTE_RANK:ROUTE_RANK + 2].astype(jnp.int32)
    cnt = counts[0, :N_EXPERTS].astype(jnp.int32)
    padded = (cnt + MOE_ROWS - 1) // MOE_ROWS * MOE_ROWS
    pad_end = jnp.cumsum(padded)
    pad_off = pad_end - padded
    dest = pad_off[eid] + rank
    nblk = -(-(2 * T) // MOE_ROWS) + N_EXPERTS
    P = nblk * MOE_ROWS
    tok = jnp.broadcast_to(jnp.arange(T, dtype=jnp.int32)[:, None], (T, 2))
    buf_tok = jnp.zeros((P,), jnp.int32).at[dest.reshape(-1)].set(tok.reshape(-1), unique_indices=True)
    blk_start = jnp.arange(nblk, dtype=jnp.int32) * MOE_ROWS
    blk_e = jnp.minimum(jnp.sum((pad_end[None, :] <= blk_start[:, None]).astype(jnp.int32), axis=1), N_EXPERTS - 1)
    blk_e = blk_e + e_off
    n_used = (pad_end[-1:] // MOE_ROWS).astype(jnp.int32)
    assert nblk % MOE_CHUNKS == 0 and T % MOE_CHUNKS == 0
    cb = nblk // MOE_CHUNKS
    y = None
    for c in range(MOE_CHUNKS):
        xs = h1b.at[buf_tok[c * cb * MOE_ROWS:(c + 1) * cb * MOE_ROWS]].get(mode='promise_in_bounds')
        y = _experts(blk_e, n_used, xs, w_eg, w_eu, w_ed, y, c * cb, nblk)
    ct = T // MOE_CHUNKS
    out = None
    for c in range(MOE_CHUNKS):
        ya = y.at[dest[c * ct:(c + 1) * ct, 0]].get(mode='promise_in_bounds')
        yb = y.at[dest[c * ct:(c + 1) * ct, 1]].get(mode='promise_in_bounds')
        out = _combine(h1, ya, yb, meta, g, b, alpha, out, c * ct)
    return out


def kernel(x, ln0_g, ln0_b, rel_bias, w_in, b_gate, sink_a, rpb_c, w_br_a, w_br_b, w_br_c, w_out,
           ln1_g, ln1_b, w_rg, b_rg, w_re, b_re, w_eg, w_eu, w_ed, ln2_g, ln2_b):
    Bn, L, D = x.shape
    depth = w_in.shape[0]
    T = Bn * L
    alpha = float((2 * depth) ** 0.25)
    a_order = np.asarray(A_HEAD_ORDER)

    a_pairs = [(c, c + A_HEADS // 2) for c in range(A_HEADS // 2)]
    b_pairs = [(2 * c, 2 * c + 1) for c in range(B_HEADS_PER_GROUP // 2)]
    bias_a = _band_bias(rel_bias[:, :A_HEADS], a_pairs, A_HALF_WINDOW, 1)
    bias_b = []
    for g, (window, dilation) in enumerate(B_CONFIGS):
        c0 = A_HEADS + g * B_HEADS_PER_GROUP
        bias_b.append(_band_bias(rel_bias[:, c0:c0 + B_HEADS_PER_GROUP], b_pairs, window // (2 * dilation), dilation))

    w_eg_b = w_eg.astype(BF16).reshape((depth * N_EXPERTS,) + w_eg.shape[2:])
    w_eu_b = w_eu.astype(BF16).reshape((depth * N_EXPERTS,) + w_eu.shape[2:])
    w_ed_b = w_ed.astype(BF16).reshape((depth * N_EXPERTS,) + w_ed.shape[2:])

    h, hb = _layer_norm(x.reshape(T, D), ln0_g, ln0_b)
    for l in range(depth):
        w_gate = w_in[l, :, QKV_COLS:].astype(BF16)
        arr_a, arr_b0, arr_b1, arr_b2, arr_c = _in_projection(hb, _inproj_weight(w_in[l]), Bn, L)

        (o_a,) = _banded_attention(arr_a, bias_a, sink_a[l][a_order], tq=512, n_pairs=A_HEADS // 2, shared_kv=True,
                                   half=A_HALF_WINDOW, with_lse=False)
        o_bs, lses = [], []
        for g, (arr, tq) in enumerate(((arr_b0, 512), (arr_b1, 512), (arr_b2, 256))):
            window, dilation = B_CONFIGS[g]
            o, lse = _banded_attention(arr, bias_b[g], None, tq=tq, n_pairs=B_HEADS_PER_GROUP // 2, shared_kv=False,
                                       half=window // (2 * dilation), with_lse=True)
            o_bs.append(o)
            lses.append(lse)
        o_c = _neighborhood_attention(arr_c, _nbr_bias(rpb_c[l]), Bn, L)

        w_r = jnp.concatenate([w_rg[l], jnp.transpose(w_re[l], (1, 0, 2)).reshape(D, N_EXPERTS)], axis=1)
        w_r = jnp.pad(w_r, ((0, 0), (0, LANES - w_r.shape[1]))).astype(BF16)
        b_r = jnp.concatenate([b_rg[l], b_re[l].reshape(-1)]).astype(F32)
        b_r = jnp.pad(b_r, (0, LANES - b_r.shape[0])).reshape(1, LANES)
        w_a = w_br_a[l].reshape(A_HEADS, HEAD_DIM, D)[a_order].reshape(A_Q, D)

        h1, h1b, logits = _post_attention(
            h, hb, o_a.reshape(T, A_Q), o_bs, lses, o_c,
            w_gate, b_gate[l].reshape(1, N_BRANCH * D).astype(F32),
            w_a.astype(BF16), w_br_b[l].astype(BF16), w_br_c[l].astype(BF16), w_out[l].astype(BF16),
            ln1_g[l].reshape(1, D), ln1_b[l].reshape(1, D), w_r, b_r, alpha)

        h, hb = _moe(h1, h1b, logits, w_eg_b, w_eu_b, w_ed_b, l * N_EXPERTS, ln2_g[l], ln2_b[l], alpha)
    return h.reshape(Bn, L, D)
```

```python
import functools

import numpy as np
import jax
import jax.numpy as jnp
from jax import lax
from jax.experimental import pallas as pl
from jax.experimental.pallas import tpu as pltpu

F32 = jnp.float32
BF16 = jnp.bfloat16
HIGHEST = lax.Precision.HIGHEST

HEAD_DIM = 64
A_HEADS = 8
A_KV_HEADS = 2
A_HALF_WINDOW = 128
B_HEADS_PER_GROUP = 4
B_CONFIGS = ((128, 1), (512, 4), (2048, 16))
C_HEADS = 8
GRID_W = 64
NA_KH = 8
NA_KW = 16
NUM_BUCKETS = 32
REL_MAX_DIST = 2048
N_GROUPS = 4
EXPERTS_PER_GROUP = 8
N_EXPERTS = N_GROUPS * EXPERTS_PER_GROUP
D_EXPERT = 512
LN_EPS = 1e-5
NEG_INF = -1e30

A_Q = A_HEADS * HEAD_DIM
A_KV = A_KV_HEADS * HEAD_DIM
B_W = B_HEADS_PER_GROUP * HEAD_DIM
C_W = C_HEADS * HEAD_DIM
N_BRANCH = 3
QKV_COLS = A_Q + 2 * A_KV + 3 * len(B_CONFIGS) * B_W + 3 * C_W
B_COL0 = A_Q + 2 * A_KV
C_COL0 = B_COL0 + 3 * len(B_CONFIGS) * B_W

ATT_BLK = 128
LANES = 128
MOE_ROWS = 256
MOE_CHUNKS = 4
VMEM_LIMIT = 56 * 1024 * 1024

A_HEAD_ORDER = tuple(h for c in range(A_HEADS // 2) for h in (c, c + A_HEADS // 2))


def _cparams(sem):
    return pltpu.CompilerParams(dimension_semantics=sem, vmem_limit_bytes=VMEM_LIMIT)


def _ln_rows(x, g, b):
    mu = jnp.mean(x, axis=-1, keepdims=True)
    xc = x - mu
    var = jnp.mean(xc * xc, axis=-1, keepdims=True)
    return xc * lax.rsqrt(var + LN_EPS) * g + b


def _ln_kernel(x_ref, g_ref, b_ref, o_ref, ob_ref):
    y = _ln_rows(x_ref[...].astype(F32), g_ref[...], b_ref[...])
    o_ref[...] = y
    ob_ref[...] = y.astype(BF16)


def _layer_norm(x, g, b, tm=512):
    T, D = x.shape
    return pl.pallas_call(
        _ln_kernel,
        out_shape=(jax.ShapeDtypeStruct((T, D), F32), jax.ShapeDtypeStruct((T, D), BF16)),
        grid=(T // tm,),
        in_specs=[pl.BlockSpec((tm, D), lambda i: (i, 0)),
                  pl.BlockSpec((1, D), lambda i: (0, 0)),
                  pl.BlockSpec((1, D), lambda i: (0, 0))],
        out_specs=(pl.BlockSpec((tm, D), lambda i: (i, 0)), pl.BlockSpec((tm, D), lambda i: (i, 0))),
        compiler_params=_cparams(("parallel",)),
        name="ln0",
    )(x, g.reshape(1, D), b.reshape(1, D))


BAND_W = 3 * B_W
NBR_W = 3 * C_W


def _inproj_weight(w_in_l):
    scale = HEAD_DIM ** -0.5
    a_q = w_in_l[:, :A_Q].reshape(-1, A_HEADS, HEAD_DIM)[:, np.asarray(A_HEAD_ORDER)].reshape(-1, A_Q)
    cols = [w_in_l[:, A_Q:A_Q + 2 * A_KV], a_q * scale]
    for g in range(len(B_CONFIGS)):
        c0 = B_COL0 + 3 * g * B_W
        cols += [w_in_l[:, c0 + B_W:c0 + 3 * B_W], w_in_l[:, c0:c0 + B_W] * scale]
    cols += [w_in_l[:, C_COL0 + C_W:C_COL0 + 3 * C_W], w_in_l[:, C_COL0:C_COL0 + C_W] * scale]
    return jnp.concatenate(cols, axis=1).astype(BF16)


def _inproj_kernel(h_ref, w_ref, a_ref, b0_ref, b1_ref, b2_ref, c_ref, tmp_ref):
    h = h_ref[...]
    tm = h.shape[0]

    def chunk(i):
        return jnp.dot(h, w_ref[:, i * BAND_W:(i + 1) * BAND_W], preferred_element_type=F32)

    a_ref[0, 0] = chunk(0).astype(BF16)
    b0_ref[0, 0] = chunk(1).astype(BF16)
    for i, ref in ((2, b1_ref), (3, b2_ref)):
        d = ref.shape[1]
        res = chunk(i)
        n_cols = BAND_W // LANES
        for c in range(n_cols):
            tmp_ref[c] = res[:, c * LANES:(c + 1) * LANES]
        for r in range(d):
            for c in range(n_cols):
                ref[0, r, :, c * LANES:(c + 1) * LANES] = tmp_ref[c, pl.ds(r, tm // d, stride=d), :].astype(BF16)
    c_ref[:, :BAND_W] = chunk(4).astype(BF16)
    c_ref[:, BAND_W:] = chunk(5).astype(BF16)


def _in_projection(hb, w, Bn, L, tm=512):
    T, D = hb.shape
    nt = L // tm
    assert L % tm == 0 and NBR_W == 2 * BAND_W and A_Q + 2 * A_KV == BAND_W
    dils = [d for _, d in B_CONFIGS]
    assert dils[0] == 1 and len(dils) == 3

    def band_spec(d):
        return pl.BlockSpec((1, d, tm // d, BAND_W), lambda b, i: (b, 0, i, 0))

    return pl.pallas_call(
        _inproj_kernel,
        out_shape=(jax.ShapeDtypeStruct((Bn, 1, L, BAND_W), BF16),
                   jax.ShapeDtypeStruct((Bn, 1, L, BAND_W), BF16),
                   jax.ShapeDtypeStruct((Bn, dils[1], L // dils[1], BAND_W), BF16),
                   jax.ShapeDtypeStruct((Bn, dils[2], L // dils[2], BAND_W), BF16),
                   jax.ShapeDtypeStruct((T, NBR_W), BF16)),
        grid=(Bn, nt),
        in_specs=[pl.BlockSpec((tm, D), lambda b, i: (b * nt + i, 0)),
                  pl.BlockSpec(w.shape, lambda b, i: (0, 0))],
        out_specs=(band_spec(1), band_spec(1), band_spec(dils[1]), band_spec(dils[2]),
                   pl.BlockSpec((tm, NBR_W), lambda b, i: (b * nt + i, 0))),
        scratch_shapes=[pltpu.VMEM((BAND_W // LANES, tm, LANES), F32)],
        compiler_params=_cparams(("parallel", "parallel")),
        name="in_proj",
    )(hb, w)


def _t5_bucket(rel):
    half = NUM_BUCKETS // 2
    max_exact = half // 2
    ret = np.where(rel > 0, half, 0)
    n = np.abs(rel)
    large = max_exact + (np.log(np.maximum(n, max_exact) / max_exact)
                         / np.log(REL_MAX_DIST / max_exact) * (half - max_exact)).astype(np.int32)
    large = np.minimum(large, half - 1)
    return (ret + np.where(n < max_exact, n, large)).astype(np.int32)


def _one_hot_t(idx, n):
    return jnp.asarray((np.arange(n)[:, None] == np.asarray(idx).reshape(1, -1)).astype(np.float32))


def _band_bias(bias_tab, head_pairs, half, dist_scale):
    kl = ATT_BLK + 2 * half
    off = np.arange(kl)[None, :] - half - np.arange(ATT_BLK)[:, None]
    band = np.abs(off) <= half
    onehot = _one_hot_t(_t5_bucket(off * dist_scale), NUM_BUCKETS)
    order = np.asarray([h for pair in head_pairs for h in pair])
    bias = jnp.dot(bias_tab.astype(F32)[:, order].T, onehot, precision=HIGHEST)
    bias = bias.reshape(len(head_pairs), 2 * ATT_BLK, kl)
    col = np.arange(kl)
    first_ok = (col >= half)[None, :]
    last_ok = (col < ATT_BLK + half)[None, :]
    out = []
    for first, last in ((0, 0), (1, 0), (0, 1), (1, 1)):
        ok = band & (first_ok if first else True) & (last_ok if last else True)
        ok = np.concatenate([ok, ok], axis=0)
        out.append(jnp.where(jnp.asarray(ok)[None], bias, NEG_INF))
    return jnp.stack(out)


def _pair_rows(x):
    lo = lax.broadcasted_iota(jnp.int32, x.shape, 1) < HEAD_DIM
    zero = jnp.zeros_like(x)
    return jnp.concatenate([jnp.where(lo, x, zero), jnp.where(lo, zero, x)], axis=0)


def _unpair_rows(x2):
    rows = x2.shape[0] // 2
    lo = lax.broadcasted_iota(jnp.int32, (rows, LANES), 1) < HEAD_DIM
    return jnp.where(lo, x2[:rows], x2[rows:])


def _banded_kernel(*refs, n_sub, sub_per_iter, half, kv_cols, n_pairs, shared_kv, has_sink, with_lse, dilation):
    cur_ref, prev_ref, next_ref, bias_ref = refs[:4]
    pos = 4
    sink_ref = None
    if has_sink:
        sink_ref = refs[pos]
        pos += 1
    o_ref = refs[pos]
    pos += 1
    lse_ref = None
    if with_lse:
        lse_ref = refs[pos]
        pos += 1
    kv_s, s_s, p_s = refs[pos:pos + 3]

    tq = n_sub * ATT_BLK
    kl = ATT_BLK + 2 * half
    kvw = kv_cols * LANES
    d = dilation
    j = pl.program_id(1)
    last_j = pl.num_programs(1) - 1
    row_is_lo = lax.broadcasted_iota(jnp.int32, (2 * ATT_BLK, 1), 0) < ATT_BLK

    def one_residue(r, carry):
        kv_s[0:half, :] = prev_ref[0, r]
        kv_s[half:half + tq, :] = cur_ref[0, r, :, :2 * kvw]
        kv_s[half + tq:, :] = next_ref[0, r]

        def sub_blocks(it, carry2):
            units = [(ii, c) for ii in range(sub_per_iter) for c in range(n_pairs)]
            row0s, variants = [], []
            for ii in range(sub_per_iter):
                i = it * sub_per_iter + ii
                row0s.append(pl.multiple_of(i * ATT_BLK, ATT_BLK))
                is_first = jnp.logical_and(i == 0, j == 0).astype(jnp.int32)
                is_last = jnp.logical_and(i == n_sub - 1, j == last_j).astype(jnp.int32)
                variants.append(is_first + 2 * is_last)

            for u, (ii, c) in enumerate(units):
                kc = 0 if shared_kv else c
                q2 = _pair_rows(cur_ref[0, r, pl.ds(row0s[ii], ATT_BLK),
                                        2 * kvw + c * LANES:2 * kvw + (c + 1) * LANES])
                k = kv_s[pl.ds(row0s[ii], kl), kc * LANES:(kc + 1) * LANES]
                s_s[u] = lax.dot_general(q2, k, (((1,), (1,)), ((), ())), preferred_element_type=F32)

            stats = []
            for u, (ii, c) in enumerate(units):
                s = s_s[u] + bias_ref[variants[ii], c]
                m = jnp.max(s, axis=-1, keepdims=True)
                if has_sink:
                    sk = jnp.where(row_is_lo, sink_ref[2 * c], sink_ref[2 * c + 1])
                    m = jnp.maximum(m, sk)
                p = jnp.exp(s - m)
                denom = jnp.sum(p, axis=-1, keepdims=True)
                if has_sink:
                    denom = denom + jnp.exp(sk - m)
                p_s[u] = p.astype(BF16)
                stats.append((m, denom))

            for ii in range(sub_per_iter):
                o_cols, lse_cols = [], []
                for c in range(n_pairs):
                    u = ii * n_pairs + c
                    m, denom = stats[u]
                    kc = 0 if shared_kv else c
                    v = kv_s[pl.ds(row0s[ii], kl), kvw + kc * LANES:kvw + (kc + 1) * LANES]
                    o2 = jnp.dot(p_s[u], v, preferred_element_type=F32) / denom
                    o_cols.append(_unpair_rows(o2))
                    if with_lse:
                        lse_cols.append(_unpair_rows(jnp.broadcast_to(m + jnp.log(denom), (2 * ATT_BLK, LANES))))
                if d == 1:
                    rows = pl.ds(row0s[ii], ATT_BLK)
                else:
                    rows = pl.ds(row0s[ii] * d + r, ATT_BLK, stride=d)
                if with_lse:
                    for c in range(n_pairs):
                        o_ref[0, c, rows, :] = o_cols[c].astype(o_ref.dtype)
                        lse_ref[0, c, rows, :] = lse_cols[c]
                else:
                    o_ref[0, rows, :] = jnp.concatenate(o_cols, axis=-1).astype(o_ref.dtype)
            return carry2

        lax.fori_loop(0, n_sub // sub_per_iter, sub_blocks, 0)
        return carry

    lax.fori_loop(0, d, one_residue, 0)


def _banded_attention(arr, bias, sink, *, tq, n_pairs, shared_kv, half, with_lse):
    Bn, d, Ls, W = arr.shape
    L = d * Ls
    qw = n_pairs * LANES
    kv_cols = (W - qw) // (2 * LANES)
    kvw = kv_cols * LANES
    tq = min(tq, Ls)
    n_sub = tq // ATT_BLK
    nbt = Ls // tq
    sub_per_iter = max(k for k in (1, 2, 4) if n_sub % k == 0)
    n_units = sub_per_iter * n_pairs
    halo_per_tile = tq // half
    n_halo = Ls // half
    assert Ls % tq == 0 and tq % half == 0 and kvw * 2 + qw == W
    o_dtype = BF16 if d == 1 else F32

    in_specs = [
        pl.BlockSpec((1, d, tq, W), lambda b, j: (b, 0, j, 0)),
        pl.BlockSpec((1, d, half, 2 * kvw), lambda b, j: (b, 0, jnp.maximum(j * halo_per_tile - 1, 0), 0)),
        pl.BlockSpec((1, d, half, 2 * kvw),
                     lambda b, j: (b, 0, jnp.minimum((j + 1) * halo_per_tile, n_halo - 1), 0)),
        pl.BlockSpec(bias.shape, lambda b, j: (0, 0, 0, 0)),
    ]
    args = [arr, arr, arr, bias]
    if sink is not None:
        in_specs.append(pl.BlockSpec(memory_space=pltpu.SMEM))
        args.append(sink.astype(F32))
    if with_lse:
        out_block = pl.BlockSpec((1, n_pairs, tq * d, LANES), lambda b, j: (b, 0, j, 0))
        out_shape = [jax.ShapeDtypeStruct((Bn, n_pairs, L, LANES), o_dtype),
                     jax.ShapeDtypeStruct((Bn, n_pairs, L, LANES), F32)]
        out_specs = [out_block, out_block]
    else:
        assert d == 1
        out_shape = [jax.ShapeDtypeStruct((Bn, L, qw), o_dtype)]
        out_specs = [pl.BlockSpec((1, tq, qw), lambda b, j: (b, j, 0))]
    return pl.pallas_call(
        functools.partial(_banded_kernel, n_sub=n_sub, sub_per_iter=sub_per_iter, half=half, kv_cols=kv_cols,
                          n_pairs=n_pairs, shared_kv=shared_kv, has_sink=sink is not None, with_lse=with_lse,
                          dilation=d),
        out_shape=tuple(out_shape),
        grid=(Bn, nbt),
        in_specs=in_specs,
        out_specs=tuple(out_specs),
        scratch_shapes=[pltpu.VMEM((tq + 2 * half, 2 * kvw), BF16),
                        pltpu.VMEM((n_units, 2 * ATT_BLK, ATT_BLK + 2 * half), F32),
                        pltpu.VMEM((n_units, 2 * ATT_BLK, ATT_BLK + 2 * half), BF16)],
        compiler_params=_cparams(("parallel", "arbitrary")),
        name=f"banded_attn_d{d}_h{half}",
    )(*args)


NBR_ROWS_PER_STEP = 8
NBR_KEY_BLOCK_ROWS = 4
NBR_ROWS_PER_ITER = 8


def _nbr_bias(rpb):
    H = rpb.shape[0]
    c = np.arange(GRID_W)
    qstart = np.clip(c - NA_KW // 2, 0, GRID_W - NA_KW)
    kc = np.arange(GRID_W)
    valid = (kc[None, :] >= qstart[:, None]) & (kc[None, :] < qstart[:, None] + NA_KW)
    cidx = np.clip(kc[None, :] - c[:, None] + NA_KW - 1, 0, 2 * NA_KW - 2)
    rows = jnp.stack([rpb.astype(F32)[:, NA_KH - 1 - v:2 * NA_KH - 1 - v] for v in range(NA_KH)])
    col_sel = _one_hot_t(cidx, 2 * NA_KW - 1)
    tab = jnp.dot(rows.reshape(-1, 2 * NA_KW - 1), col_sel, precision=HIGHEST)
    tab = tab.reshape(NA_KH, H, NA_KH, GRID_W, GRID_W).transpose(0, 1, 3, 2, 4)
    tab = jnp.where(jnp.asarray(valid)[None, None, :, None, :], tab, NEG_INF)
    return tab.reshape(NA_KH, H // 2, 2 * GRID_W, NA_KH * GRID_W)


def _nbr_kernel(q_ref, kv0, kv1, kv2, kv3, bias_ref, o_ref, kv_s, s_s, p_s, *, rows):
    blk = NBR_KEY_BLOCK_ROWS * GRID_W
    for t, ref in enumerate((kv0, kv1, kv2, kv3)):
        kv_s[t * blk:(t + 1) * blk, :] = ref[...]
    i0 = pl.program_id(1) * NBR_ROWS_PER_STEP
    nkeys = NA_KH * GRID_W
    n_pairs = C_HEADS // 2

    def row_group(it, carry):
        offs, variants, q0s = [], [], []
        for rr in range(NBR_ROWS_PER_ITER):
            r = it * NBR_ROWS_PER_ITER + rr
            i = i0 + r
            rstart = jnp.clip(i - NA_KH // 2, 0, rows - NA_KH)
            offs.append(pl.multiple_of((rstart - i0 + NBR_KEY_BLOCK_ROWS) * GRID_W, GRID_W))
            variants.append(i - rstart)
            q0s.append(pl.multiple_of(r * GRID_W, GRID_W))
        units = [(rr, c) for rr in range(NBR_ROWS_PER_ITER) for c in range(n_pairs)]

        for u, (rr, c) in enumerate(units):
            q2 = _pair_rows(q_ref[pl.ds(q0s[rr], GRID_W), c * LANES:(c + 1) * LANES])
            k = kv_s[pl.ds(offs[rr], nkeys), c * LANES:(c + 1) * LANES]
            s_s[u] = lax.dot_general(q2, k, (((1,), (1,)), ((), ())), preferred_element_type=F32)

        denoms = []
        for u, (rr, c) in enumerate(units):
            s = s_s[u] + bias_ref[variants[rr], c]
            m = jnp.max(s, axis=-1, keepdims=True)
            p = jnp.exp(s - m)
            denoms.append(jnp.sum(p, axis=-1, keepdims=True))
            p_s[u] = p.astype(BF16)

        for rr in range(NBR_ROWS_PER_ITER):
            o_cols = []
            for c in range(n_pairs):
                u = rr * n_pairs + c
                v = kv_s[pl.ds(offs[rr], nkeys), C_W + c * LANES:C_W + (c + 1) * LANES]
                o2 = jnp.dot(p_s[u], v, preferred_element_type=F32) / denoms[u]
                o_cols.append(_unpair_rows(o2))
            o_ref[pl.ds(q0s[rr], GRID_W), :] = jnp.concatenate(o_cols, axis=-1).astype(o_ref.dtype)
        return carry

    lax.fori_loop(0, NBR_ROWS_PER_STEP // NBR_ROWS_PER_ITER, row_group, 0)


def _neighborhood_attention(arr, bias, Bn, L):
    rows = L // GRID_W
    assert rows % NBR_ROWS_PER_STEP == 0 and rows >= 2 * NBR_ROWS_PER_STEP
    tq = NBR_ROWS_PER_STEP * GRID_W
    kb = NBR_KEY_BLOCK_ROWS * GRID_W
    nkb = L // kb
    per_step = NBR_ROWS_PER_STEP // NBR_KEY_BLOCK_ROWS
    nq = L // tq

    def key_spec(t):
        return pl.BlockSpec((kb, 2 * C_W), lambda b, j: (b * nkb + jnp.clip(j * per_step - 1 + t, 0, nkb - 1), 0))

    in_specs = ([pl.BlockSpec((tq, C_W), lambda b, j: (b * nq + j, 2))]
                + [key_spec(t) for t in range(4)]
                + [pl.BlockSpec(bias.shape, lambda b, j: (0, 0, 0, 0))])
    return pl.pallas_call(
        functools.partial(_nbr_kernel, rows=rows),
        out_shape=jax.ShapeDtypeStruct((Bn * L, C_W), BF16),
        grid=(Bn, nq),
        in_specs=in_specs,
        out_specs=pl.BlockSpec((tq, C_W), lambda b, j: (b * nq + j, 0)),
        scratch_shapes=[pltpu.VMEM((4 * kb, 2 * C_W), BF16),
                        pltpu.VMEM((NBR_ROWS_PER_ITER * C_HEADS // 2, 2 * GRID_W, NA_KH * GRID_W), F32),
                        pltpu.VMEM((NBR_ROWS_PER_ITER * C_HEADS // 2, 2 * GRID_W, NA_KH * GRID_W), BF16)],
        compiler_params=_cparams(("parallel", "arbitrary")),
        name="nbr_attn",
    )(*([arr] * 5), bias)


def _post_kernel(h_ref, hb_ref, oa_ref, ob0_ref, ob1_ref, ob2_ref, l0_ref, l1_ref, l2_ref, oc_ref,
                 wg_ref, bg_ref, wa_ref, wb_ref, wc_ref, wo_ref, g_ref, b_ref, wr_ref, br_ref,
                 o_ref, obf_ref, logit_ref, *, alpha):
    D = h_ref.shape[1]
    hb = hb_ref[...]
    def planes(ref):
        return jnp.concatenate([ref[0, c] for c in range(ref.shape[1])], axis=-1).astype(F32)

    l0, l1, l2 = planes(l0_ref), planes(l1_ref), planes(l2_ref)
    lm = jnp.maximum(jnp.maximum(l0, l1), l2)
    e0, e1, e2 = jnp.exp(l0 - lm), jnp.exp(l1 - lm), jnp.exp(l2 - lm)
    inv = 1.0 / (e0 + e1 + e2)
    o_b = ((e0 * planes(ob0_ref) + e1 * planes(ob1_ref) + e2 * planes(ob2_ref)) * inv).astype(BF16)
    merged = None
    for idx, (o_br, w_ref) in enumerate(((oa_ref[...], wa_ref), (o_b, wb_ref), (oc_ref[...], wc_ref))):
        z = (jnp.dot(hb, wg_ref[:, idx * D:(idx + 1) * D], preferred_element_type=F32)
             + bg_ref[:, idx * D:(idx + 1) * D])
        gate = 0.5 * jnp.tanh(0.5 * z) + 0.5
        term = gate * jnp.dot(o_br, w_ref[...], preferred_element_type=F32)
        merged = term if merged is None else merged + term
    y = jnp.dot(merged.astype(BF16), wo_ref[...], preferred_element_type=F32)
    h1 = _ln_rows(alpha * h_ref[...] + y, g_ref[...], b_ref[...])
    h1b = h1.astype(BF16)
    o_ref[...] = h1
    obf_ref[...] = h1b
    logit_ref[...] = jnp.dot(h1b, wr_ref[...], preferred_element_type=F32) + br_ref[...]


def _post_attention(h, hb, o_a, o_bs, lses, o_c, wg, bg, wa, wb, wc, wo, g, b, wr, br, alpha, tm=512):
    T, D = h.shape
    n_planes, L = o_bs[0].shape[1:3]
    nt = L // tm
    assert L % tm == 0

    def rows(w):
        return pl.BlockSpec((tm, w), lambda i: (i, 0))

    plane_rows = pl.BlockSpec((1, n_planes, tm, LANES), lambda i: (i // nt, 0, i % nt, 0))

    def full(a):
        return pl.BlockSpec(a.shape, lambda i: (0, 0), pipeline_mode=pl.Buffered(1))

    weights = [wg, bg, wa, wb, wc, wo, g, b, wr, br]
    return pl.pallas_call(
        functools.partial(_post_kernel, alpha=alpha),
        out_shape=(jax.ShapeDtypeStruct((T, D), F32), jax.ShapeDtypeStruct((T, D), BF16),
                   jax.ShapeDtypeStruct((T, LANES), F32)),
        grid=(T // tm,),
        in_specs=[rows(D), rows(D), rows(A_Q)] + [plane_rows] * 6 + [rows(C_W)] + [full(w) for w in weights],
        out_specs=(rows(D), rows(D), rows(LANES)),
        compiler_params=_cparams(("parallel",)),
        name="post_attn",
    )(h, hb, o_a, *o_bs, *lses, o_c, *weights)


ROUTE_EID, ROUTE_GATE, ROUTE_RANK = 0, 2, 4


def _route_kernel(logit_ref, meta_ref, count_ref, run_ref):
    tm = logit_ref.shape[0]

    @pl.when(pl.program_id(0) == 0)
    def _():
        run_ref[...] = jnp.zeros_like(run_ref)

    x = logit_ref[...]
    lane = lax.broadcasted_iota(jnp.int32, x.shape, 1).astype(F32)
    big = float(LANES)

    def lane_max(v):
        return jnp.max(v, axis=-1, keepdims=True)

    def first_lane(mask):
        return jnp.min(jnp.where(mask, lane, big), axis=-1, keepdims=True)

    is_g = lane < N_GROUPS
    gl = jnp.where(is_g, x, NEG_INF)
    gmax = lane_max(gl)
    g_idx = first_lane(jnp.logical_and(is_g, gl == gmax))
    g_w = 1.0 / jnp.sum(jnp.where(is_g, jnp.exp(gl - gmax), 0.0), axis=-1, keepdims=True)

    e_lo = N_GROUPS + g_idx * EXPERTS_PER_GROUP
    in_grp = jnp.logical_and(lane >= e_lo, lane < e_lo + EXPERTS_PER_GROUP)
    el = jnp.where(in_grp, x, NEG_INF)
    top1 = lane_max(el)
    lane1 = first_lane(jnp.logical_and(in_grp, el == top1))
    el2 = jnp.where(lane == lane1, NEG_INF, el)
    top2 = lane_max(el2)
    lane2 = first_lane(jnp.logical_and(in_grp, jnp.logical_and(lane != lane1, el2 == top2)))
    e2 = jnp.exp(top2 - top1)
    w1 = 1.0 / (1.0 + e2)
    w2 = e2 / (1.0 + e2)

    eid1 = lane1 - N_GROUPS
    eid2 = lane2 - N_GROUPS
    onehot = jnp.logical_or(lane == eid1, lane == eid2)
    oh = jnp.where(onehot, 1.0, 0.0).astype(BF16)
    r_i = lax.broadcasted_iota(jnp.int32, (tm, tm), 0)
    c_i = lax.broadcasted_iota(jnp.int32, (tm, tm), 1)
    strict_lower = jnp.where(c_i < r_i, 1.0, 0.0).astype(BF16)
    before = jnp.dot(strict_lower, oh, preferred_element_type=F32) + run_ref[...]
    rank1 = jnp.sum(jnp.where(lane == eid1, before, 0.0), axis=-1, keepdims=True)
    rank2 = jnp.sum(jnp.where(lane == eid2, before, 0.0), axis=-1, keepdims=True)
    run_ref[...] = run_ref[...] + jnp.sum(oh.astype(F32), axis=0, keepdims=True)

    meta = jnp.zeros(x.shape, F32)
    for k, val in ((ROUTE_EID, eid1), (ROUTE_EID + 1, eid2),
                   (ROUTE_GATE, g_w * w1), (ROUTE_GATE + 1, g_w * w2),
                   (ROUTE_RANK, rank1), (ROUTE_RANK + 1, rank2)):
        meta = jnp.where(lane == k, val, meta)
    meta_ref[...] = meta
    count_ref[...] = run_ref[...]


def _route(logits, tm=512):
    T = logits.shape[0]
    return pl.pallas_call(
        _route_kernel,
        out_shape=(jax.ShapeDtypeStruct((T, LANES), F32), jax.ShapeDtypeStruct((1, LANES), F32)),
        grid=(T // tm,),
        in_specs=[pl.BlockSpec((tm, LANES), lambda i: (i, 0))],
        out_specs=(pl.BlockSpec((tm, LANES), lambda i: (i, 0)), pl.BlockSpec((1, LANES), lambda i: (0, 0))),
        scratch_shapes=[pltpu.VMEM((1, LANES), F32)],
        compiler_params=_cparams(("arbitrary",)),
        name="route",
    )(logits)


def _expert_kernel(blk_e_ref, n_used_ref, x_ref, wg_ref, wu_ref, wd_ref, *rest, blk0):
    o_ref = rest[-1]
    b = pl.program_id(0) + blk0

    @pl.when(b < n_used_ref[0])
    def _():
        x = x_ref[...]
        a = jnp.dot(x, wg_ref[0], preferred_element_type=F32)
        u = jnp.dot(x, wu_ref[0], preferred_element_type=F32)
        hid = (a * jax.nn.sigmoid(a) * u).astype(BF16)
        o_ref[...] = jnp.dot(hid, wd_ref[0], preferred_element_type=F32).astype(o_ref.dtype)

    @pl.when(b >= n_used_ref[0])
    def _():
        o_ref[...] = jnp.zeros_like(o_ref)


def _experts(blk_e, n_used, xs, wg, wu, wd, y_prev, blk0, nblk):
    rows, D = xs.shape
    de = wg.shape[2]
    in_specs = [pl.BlockSpec((MOE_ROWS, D), lambda b, be, nu: (b, 0)),
                pl.BlockSpec((1, D, de), lambda b, be, nu: (be[b + blk0], 0, 0)),
                pl.BlockSpec((1, D, de), lambda b, be, nu: (be[b + blk0], 0, 0)),
                pl.BlockSpec((1, de, D), lambda b, be, nu: (be[b + blk0], 0, 0))]
    args = [blk_e, n_used, xs, wg, wu, wd]
    aliases = {}
    if y_prev is not None:
        in_specs.append(pl.BlockSpec(memory_space=pl.ANY))
        args.append(y_prev)
        aliases = {len(args) - 1: 0}
    grid_spec = pltpu.PrefetchScalarGridSpec(
        num_scalar_prefetch=2,
        grid=(rows // MOE_ROWS,),
        in_specs=in_specs,
        out_specs=pl.BlockSpec((MOE_ROWS, D), lambda b, be, nu: (b + blk0, 0)),
    )
    return pl.pallas_call(
        functools.partial(_expert_kernel, blk0=blk0),
        out_shape=jax.ShapeDtypeStruct((nblk * MOE_ROWS, D), BF16),
        grid_spec=grid_spec,
        input_output_aliases=aliases,
        compiler_params=_cparams(("arbitrary",)),
        name="experts",
    )(*args)


def _combine_kernel(h_ref, ya_ref, yb_ref, meta_ref, g_ref, b_ref, *rest, alpha):
    o_ref, ob_ref = rest[-2:]
    meta = meta_ref[...]
    g1 = meta[:, ROUTE_GATE:ROUTE_GATE + 1]
    g2 = meta[:, ROUTE_GATE + 1:ROUTE_GATE + 2]
    y = ya_ref[...].astype(F32) * g1 + yb_ref[...].astype(F32) * g2
    h2 = _ln_rows(alpha * h_ref[...] + y, g_ref[...], b_ref[...])
    o_ref[...] = h2
    ob_ref[...] = h2.astype(BF16)


def _combine(h, ya, yb, meta, g, b, alpha, prev, row0, tm=512):
    T, D = h.shape
    t0 = row0 // tm
    assert row0 % tm == 0 and ya.shape[0] % tm == 0
    chunk_rows = pl.BlockSpec((tm, D), lambda i: (i, 0))
    full_rows = pl.BlockSpec((tm, D), lambda i: (i + t0, 0))
    vec = pl.BlockSpec((1, D), lambda i: (0, 0))
    in_specs = [full_rows, chunk_rows, chunk_rows, pl.BlockSpec((tm, LANES), lambda i: (i + t0, 0)), vec, vec]
    args = [h, ya, yb, meta, g.reshape(1, D), b.reshape(1, D)]
    aliases = {}
    if prev is not None:
        in_specs += [pl.BlockSpec(memory_space=pl.ANY)] * 2
        aliases = {len(args): 0, len(args) + 1: 1}
        args += list(prev)
    return pl.pallas_call(
        functools.partial(_combine_kernel, alpha=alpha),
        out_shape=(jax.ShapeDtypeStruct((T, D), F32), jax.ShapeDtypeStruct((T, D), BF16)),
        grid=(ya.shape[0] // tm,),
        in_specs=in_specs,
        out_specs=(full_rows, full_rows),
        input_output_aliases=aliases,
        compiler_params=_cparams(("parallel",)),
        name="combine_ln2",
    )(*args)


def _moe(h1, h1b, logits, w_eg, w_eu, w_ed, e_off, g, b, alpha):
    T, D = h1.shape
    meta, counts = _route(logits)
    eid = meta[:, ROUTE_EID:ROUTE_EID + 2].astype(jnp.int32)
    rank = meta[:, ROUTE_RANK:ROUTE_RANK + 2].astype(jnp.int32)
    cnt = counts[0, :N_EXPERTS].astype(jnp.int32)
    padded = (cnt + MOE_ROWS - 1) // MOE_ROWS * MOE_ROWS
    pad_end = jnp.cumsum(padded)
    pad_off = pad_end - padded
    dest = pad_off[eid] + rank
    nblk = -(-(2 * T) // MOE_ROWS) + N_EXPERTS
    P = nblk * MOE_ROWS
    tok = jnp.broadcast_to(jnp.arange(T, dtype=jnp.int32)[:, None], (T, 2))
    buf_tok = jnp.zeros((P,), jnp.int32).at[dest.reshape(-1)].set(tok.reshape(-1), unique_indices=True)
    blk_start = jnp.arange(nblk, dtype=jnp.int32) * MOE_ROWS
    blk_e = jnp.minimum(jnp.sum((pad_end[None, :] <= blk_start[:, None]).astype(jnp.int32), axis=1), N_EXPERTS - 1)
    blk_e = blk_e + e_off
    n_used = (pad_end[-1:] // MOE_ROWS).astype(jnp.int32)
    assert nblk % MOE_CHUNKS == 0 and T % MOE_CHUNKS == 0
    cb = nblk // MOE_CHUNKS
    y = None
    for c in range(MOE_CHUNKS):
        xs = h1b.at[buf_tok[c * cb * MOE_ROWS:(c + 1) * cb * MOE_ROWS]].get(mode='promise_in_bounds')
        y = _experts(blk_e, n_used, xs, w_eg, w_eu, w_ed, y, c * cb, nblk)
    ct = T // MOE_CHUNKS
    out = None
    for c in range(MOE_CHUNKS):
        ya = y.at[dest[c * ct:(c + 1) * ct, 0]].get(mode='promise_in_bounds')
        yb = y.at[dest[c * ct:(c + 1) * ct, 1]].get(mode='promise_in_bounds')
        out = _combine(h1, ya, yb, meta, g, b, alpha, out, c * ct)
    return out


def kernel(x, ln0_g, ln0_b, rel_bias, w_in, b_gate, sink_a, rpb_c, w_br_a, w_br_b, w_br_c, w_out,
           ln1_g, ln1_b, w_rg, b_rg, w_re, b_re, w_eg, w_eu, w_ed, ln2_g, ln2_b):
    Bn, L, D = x.shape
    depth = w_in.shape[0]
    T = Bn * L
    alpha = float((2 * depth) ** 0.25)
    a_order = np.asarray(A_HEAD_ORDER)

    a_pairs = [(c, c + A_HEADS // 2) for c in range(A_HEADS // 2)]
    b_pairs = [(2 * c, 2 * c + 1) for c in range(B_HEADS_PER_GROUP // 2)]
    bias_a = _band_bias(rel_bias[:, :A_HEADS], a_pairs, A_HALF_WINDOW, 1)
    bias_b = []
    for g, (window, dilation) in enumerate(B_CONFIGS):
        c0 = A_HEADS + g * B_HEADS_PER_GROUP
        bias_b.append(_band_bias(rel_bias[:, c0:c0 + B_HEADS_PER_GROUP], b_pairs, window // (2 * dilation), dilation))

    w_eg_b = w_eg.astype(BF16).reshape((depth * N_EXPERTS,) + w_eg.shape[2:])
    w_eu_b = w_eu.astype(BF16).reshape((depth * N_EXPERTS,) + w_eu.shape[2:])
    w_ed_b = w_ed.astype(BF16).reshape((depth * N_EXPERTS,) + w_ed.shape[2:])

    h, hb = _layer_norm(x.reshape(T, D), ln0_g, ln0_b)
    for l in range(depth):
        w_gate = w_in[l, :, QKV_COLS:].astype(BF16)
        arr_a, arr_b0, arr_b1, arr_b2, arr_c = _in_projection(hb, _inproj_weight(w_in[l]), Bn, L)

        (o_a,) = _banded_attention(arr_a, bias_a, sink_a[l][a_order], tq=512, n_pairs=A_HEADS // 2, shared_kv=True,
                                   half=A_HALF_WINDOW, with_lse=False)
        o_bs, lses = [], []
        for g, (arr, tq) in enumerate(((arr_b0, 512), (arr_b1, 512), (arr_b2, 256))):
            window, dilation = B_CONFIGS[g]
            o, lse = _banded_attention(arr, bias_b[g], None, tq=tq, n_pairs=B_HEADS_PER_GROUP // 2, shared_kv=False,
                                       half=window // (2 * dilation), with_lse=True)
            o_bs.append(o)
            lses.append(lse)
        o_c = _neighborhood_attention(arr_c, _nbr_bias(rpb_c[l]), Bn, L)

        w_r = jnp.concatenate([w_rg[l], jnp.transpose(w_re[l], (1, 0, 2)).reshape(D, N_EXPERTS)], axis=1)
        w_r = jnp.pad(w_r, ((0, 0), (0, LANES - w_r.shape[1]))).astype(BF16)
        b_r = jnp.concatenate([b_rg[l], b_re[l].reshape(-1)]).astype(F32)
        b_r = jnp.pad(b_r, (0, LANES - b_r.shape[0])).reshape(1, LANES)
        w_a = w_br_a[l].reshape(A_HEADS, HEAD_DIM, D)[a_order].reshape(A_Q, D)

        h1, h1b, logits = _post_attention(
            h, hb, o_a.reshape(T, A_Q), o_bs, lses, o_c,
            w_gate, b_gate[l].reshape(1, N_BRANCH * D).astype(F32),
            w_a.astype(BF16), w_br_b[l].astype(BF16), w_br_c[l].astype(BF16), w_out[l].astype(BF16),
            ln1_g[l].reshape(1, D), ln1_b[l].reshape(1, D), w_r, b_r, alpha)

        h, hb = _moe(h1, h1b, logits, w_eg_b, w_eu_b, w_ed_b, l * N_EXPERTS, ln2_g[l], ln2_b[l], alpha)
    return h.reshape(Bn, L, D)
```

```python
import functools

import numpy as np
import jax
import jax.numpy as jnp
from jax import lax
from jax.experimental import pallas as pl
from jax.experimental.pallas import tpu as pltpu

F32 = jnp.float32
BF16 = jnp.bfloat16
HIGHEST = lax.Precision.HIGHEST

HEAD_DIM = 64
A_HEADS = 8
A_KV_HEADS = 2
A_HALF_WINDOW = 128
B_HEADS_PER_GROUP = 4
B_CONFIGS = ((128, 1), (512, 4), (2048, 16))
C_HEADS = 8
GRID_W = 64
NA_KH = 8
NA_KW = 16
NUM_BUCKETS = 32
REL_MAX_DIST = 2048
N_GROUPS = 4
EXPERTS_PER_GROUP = 8
N_EXPERTS = N_GROUPS * EXPERTS_PER_GROUP
D_EXPERT = 512
LN_EPS = 1e-5
NEG_INF = -1e30

A_Q = A_HEADS * HEAD_DIM
A_KV = A_KV_HEADS * HEAD_DIM
B_W = B_HEADS_PER_GROUP * HEAD_DIM
C_W = C_HEADS * HEAD_DIM
N_BRANCH = 3
QKV_COLS = A_Q + 2 * A_KV + 3 * len(B_CONFIGS) * B_W + 3 * C_W
B_COL0 = A_Q + 2 * A_KV
C_COL0 = B_COL0 + 3 * len(B_CONFIGS) * B_W

ATT_BLK = 128
LANES = 128
MOE_ROWS = 512
MOE_CHUNKS = 4
VMEM_LIMIT = 56 * 1024 * 1024

A_HEAD_ORDER = tuple(h for c in range(A_HEADS // 2) for h in (c, c + A_HEADS // 2))


def _cparams(sem):
    return pltpu.CompilerParams(dimension_semantics=sem, vmem_limit_bytes=VMEM_LIMIT)


def _ln_rows(x, g, b):
    mu = jnp.mean(x, axis=-1, keepdims=True)
    xc = x - mu
    var = jnp.mean(xc * xc, axis=-1, keepdims=True)
    return xc * lax.rsqrt(var + LN_EPS) * g + b


def _ln_kernel(x_ref, g_ref, b_ref, o_ref, ob_ref):
    y = _ln_rows(x_ref[...].astype(F32), g_ref[...], b_ref[...])
    o_ref[...] = y
    ob_ref[...] = y.astype(BF16)


def _layer_norm(x, g, b, tm=512):
    T, D = x.shape
    return pl.pallas_call(
        _ln_kernel,
        out_shape=(jax.ShapeDtypeStruct((T, D), F32), jax.ShapeDtypeStruct((T, D), BF16)),
        grid=(T // tm,),
        in_specs=[pl.BlockSpec((tm, D), lambda i: (i, 0)),
                  pl.BlockSpec((1, D), lambda i: (0, 0)),
                  pl.BlockSpec((1, D), lambda i: (0, 0))],
        out_specs=(pl.BlockSpec((tm, D), lambda i: (i, 0)), pl.BlockSpec((tm, D), lambda i: (i, 0))),
        compiler_params=_cparams(("parallel",)),
        name="ln0",
    )(x, g.reshape(1, D), b.reshape(1, D))


BAND_W = 3 * B_W
NBR_W = 3 * C_W


def _inproj_weight(w_in_l):
    scale = HEAD_DIM ** -0.5
    a_q = w_in_l[:, :A_Q].reshape(-1, A_HEADS, HEAD_DIM)[:, np.asarray(A_HEAD_ORDER)].reshape(-1, A_Q)
    cols = [w_in_l[:, A_Q:A_Q + 2 * A_KV], a_q * scale]
    for g in range(len(B_CONFIGS)):
        c0 = B_COL0 + 3 * g * B_W
        cols += [w_in_l[:, c0 + B_W:c0 + 3 * B_W], w_in_l[:, c0:c0 + B_W] * scale]
    cols += [w_in_l[:, C_COL0 + C_W:C_COL0 + 3 * C_W], w_in_l[:, C_COL0:C_COL0 + C_W] * scale]
    return jnp.concatenate(cols, axis=1).astype(BF16)


def _inproj_kernel(h_ref, w_ref, a_ref, b0_ref, b1_ref, b2_ref, c_ref, tmp_ref):
    h = h_ref[...]
    tm = h.shape[0]

    def chunk(i):
        return jnp.dot(h, w_ref[:, i * BAND_W:(i + 1) * BAND_W], preferred_element_type=F32)

    a_ref[0, 0] = chunk(0).astype(BF16)
    b0_ref[0, 0] = chunk(1).astype(BF16)
    for i, ref in ((2, b1_ref), (3, b2_ref)):
        d = ref.shape[1]
        res = chunk(i)
        n_cols = BAND_W // LANES
        for c in range(n_cols):
            tmp_ref[c] = res[:, c * LANES:(c + 1) * LANES]
        for r in range(d):
            for c in range(n_cols):
                ref[0, r, :, c * LANES:(c + 1) * LANES] = tmp_ref[c, pl.ds(r, tm // d, stride=d), :].astype(BF16)
    c_ref[:, :BAND_W] = chunk(4).astype(BF16)
    c_ref[:, BAND_W:] = chunk(5).astype(BF16)


def _in_projection(hb, w, Bn, L, tm=512):
    T, D = hb.shape
    nt = L // tm
    assert L % tm == 0 and NBR_W == 2 * BAND_W and A_Q + 2 * A_KV == BAND_W
    dils = [d for _, d in B_CONFIGS]
    assert dils[0] == 1 and len(dils) == 3

    def band_spec(d):
        return pl.BlockSpec((1, d, tm // d, BAND_W), lambda b, i: (b, 0, i, 0))

    return pl.pallas_call(
        _inproj_kernel,
        out_shape=(jax.ShapeDtypeStruct((Bn, 1, L, BAND_W), BF16),
                   jax.ShapeDtypeStruct((Bn, 1, L, BAND_W), BF16),
                   jax.ShapeDtypeStruct((Bn, dils[1], L // dils[1], BAND_W), BF16),
                   jax.ShapeDtypeStruct((Bn, dils[2], L // dils[2], BAND_W), BF16),
                   jax.ShapeDtypeStruct((T, NBR_W), BF16)),
        grid=(Bn, nt),
        in_specs=[pl.BlockSpec((tm, D), lambda b, i: (b * nt + i, 0)),
                  pl.BlockSpec(w.shape, lambda b, i: (0, 0))],
        out_specs=(band_spec(1), band_spec(1), band_spec(dils[1]), band_spec(dils[2]),
                   pl.BlockSpec((tm, NBR_W), lambda b, i: (b * nt + i, 0))),
        scratch_shapes=[pltpu.VMEM((BAND_W // LANES, tm, LANES), F32)],
        compiler_params=_cparams(("parallel", "parallel")),
        name="in_proj",
    )(hb, w)


def _t5_bucket(rel):
    half = NUM_BUCKETS // 2
    max_exact = half // 2
    ret = np.where(rel > 0, half, 0)
    n = np.abs(rel)
    large = max_exact + (np.log(np.maximum(n, max_exact) / max_exact)
                         / np.log(REL_MAX_DIST / max_exact) * (half - max_exact)).astype(np.int32)
    large = np.minimum(large, half - 1)
    return (ret + np.where(n < max_exact, n, large)).astype(np.int32)


def _one_hot_t(idx, n):
    return jnp.asarray((np.arange(n)[:, None] == np.asarray(idx).reshape(1, -1)).astype(np.float32))


def _band_bias(bias_tab, head_pairs, half, dist_scale):
    kl = ATT_BLK + 2 * half
    off = np.arange(kl)[None, :] - half - np.arange(ATT_BLK)[:, None]
    band = np.abs(off) <= half
    onehot = _one_hot_t(_t5_bucket(off * dist_scale), NUM_BUCKETS)
    order = np.asarray([h for pair in head_pairs for h in pair])
    bias = jnp.dot(bias_tab.astype(F32)[:, order].T, onehot, precision=HIGHEST)
    bias = bias.reshape(len(head_pairs), 2 * ATT_BLK, kl)
    col = np.arange(kl)
    first_ok = (col >= half)[None, :]
    last_ok = (col < ATT_BLK + half)[None, :]
    out = []
    for first, last in ((0, 0), (1, 0), (0, 1), (1, 1)):
        ok = band & (first_ok if first else True) & (last_ok if last else True)
        ok = np.concatenate([ok, ok], axis=0)
        out.append(jnp.where(jnp.asarray(ok)[None], bias, NEG_INF))
    return jnp.stack(out)


def _pair_rows(x):
    lo = lax.broadcasted_iota(jnp.int32, x.shape, 1) < HEAD_DIM
    zero = jnp.zeros_like(x)
    return jnp.concatenate([jnp.where(lo, x, zero), jnp.where(lo, zero, x)], axis=0)


def _unpair_rows(x2):
    rows = x2.shape[0] // 2
    lo = lax.broadcasted_iota(jnp.int32, (rows, LANES), 1) < HEAD_DIM
    return jnp.where(lo, x2[:rows], x2[rows:])


def _banded_kernel(*refs, n_sub, sub_per_iter, half, kv_cols, n_pairs, shared_kv, has_sink, with_lse, dilation):
    cur_ref, prev_ref, next_ref, bias_ref = refs[:4]
    pos = 4
    sink_ref = None
    if has_sink:
        sink_ref = refs[pos]
        pos += 1
    o_ref = refs[pos]
    pos += 1
    lse_ref = None
    if with_lse:
        lse_ref = refs[pos]
        pos += 1
    kv_s, s_s, p_s = refs[pos:pos + 3]

    tq = n_sub * ATT_BLK
    kl = ATT_BLK + 2 * half
    kvw = kv_cols * LANES
    d = dilation
    j = pl.program_id(1)
    last_j = pl.num_programs(1) - 1
    row_is_lo = lax.broadcasted_iota(jnp.int32, (2 * ATT_BLK, 1), 0) < ATT_BLK

    def one_residue(r, carry):
        kv_s[0:half, :] = prev_ref[0, r]
        kv_s[half:half + tq, :] = cur_ref[0, r, :, :2 * kvw]
        kv_s[half + tq:, :] = next_ref[0, r]

        def sub_blocks(it, carry2):
            units = [(ii, c) for ii in range(sub_per_iter) for c in range(n_pairs)]
            row0s, variants = [], []
            for ii in range(sub_per_iter):
                i = it * sub_per_iter + ii
                row0s.append(pl.multiple_of(i * ATT_BLK, ATT_BLK))
                is_first = jnp.logical_and(i == 0, j == 0).astype(jnp.int32)
                is_last = jnp.logical_and(i == n_sub - 1, j == last_j).astype(jnp.int32)
                variants.append(is_first + 2 * is_last)

            for u, (ii, c) in enumerate(units):
                kc = 0 if shared_kv else c
                q2 = _pair_rows(cur_ref[0, r, pl.ds(row0s[ii], ATT_BLK),
                                        2 * kvw + c * LANES:2 * kvw + (c + 1) * LANES])
                k = kv_s[pl.ds(row0s[ii], kl), kc * LANES:(kc + 1) * LANES]
                s_s[u] = lax.dot_general(q2, k, (((1,), (1,)), ((), ())), preferred_element_type=F32)

            stats = []
            for u, (ii, c) in enumerate(units):
                s = s_s[u] + bias_ref[variants[ii], c]
                m = jnp.max(s, axis=-1, keepdims=True)
                if has_sink:
                    sk = jnp.where(row_is_lo, sink_ref[2 * c], sink_ref[2 * c + 1])
                    m = jnp.maximum(m, sk)
                p = jnp.exp(s - m)
                denom = jnp.sum(p, axis=-1, keepdims=True)
                if has_sink:
                    denom = denom + jnp.exp(sk - m)
                p_s[u] = p.astype(BF16)
                stats.append((m, denom))

            for ii in range(sub_per_iter):
                o_cols, lse_cols = [], []
                for c in range(n_pairs):
                    u = ii * n_pairs + c
                    m, denom = stats[u]
                    kc = 0 if shared_kv else c
                    v = kv_s[pl.ds(row0s[ii], kl), kvw + kc * LANES:kvw + (kc + 1) * LANES]
                    o2 = jnp.dot(p_s[u], v, preferred_element_type=F32) / denom
                    o_cols.append(_unpair_rows(o2))
                    if with_lse:
                        lse_cols.append(_unpair_rows(jnp.broadcast_to(m + jnp.log(denom), (2 * ATT_BLK, LANES))))
                if d == 1:
                    rows = pl.ds(row0s[ii], ATT_BLK)
                else:
                    rows = pl.ds(row0s[ii] * d + r, ATT_BLK, stride=d)
                if with_lse:
                    for c in range(n_pairs):
                        o_ref[0, c, rows, :] = o_cols[c].astype(o_ref.dtype)
                        lse_ref[0, c, rows, :] = lse_cols[c]
                else:
                    o_ref[0, rows, :] = jnp.concatenate(o_cols, axis=-1).astype(o_ref.dtype)
            return carry2

        lax.fori_loop(0, n_sub // sub_per_iter, sub_blocks, 0)
        return carry

    lax.fori_loop(0, d, one_residue, 0)


def _banded_attention(arr, bias, sink, *, tq, n_pairs, shared_kv, half, with_lse):
    Bn, d, Ls, W = arr.shape
    L = d * Ls
    qw = n_pairs * LANES
    kv_cols = (W - qw) // (2 * LANES)
    kvw = kv_cols * LANES
    tq = min(tq, Ls)
    n_sub = tq // ATT_BLK
    nbt = Ls // tq
    sub_per_iter = max(k for k in (1, 2, 4) if n_sub % k == 0)
    n_units = sub_per_iter * n_pairs
    halo_per_tile = tq // half
    n_halo = Ls // half
    assert Ls % tq == 0 and tq % half == 0 and kvw * 2 + qw == W
    o_dtype = BF16 if d == 1 else F32

    in_specs = [
        pl.BlockSpec((1, d, tq, W), lambda b, j: (b, 0, j, 0)),
        pl.BlockSpec((1, d, half, 2 * kvw), lambda b, j: (b, 0, jnp.maximum(j * halo_per_tile - 1, 0), 0)),
        pl.BlockSpec((1, d, half, 2 * kvw),
                     lambda b, j: (b, 0, jnp.minimum((j + 1) * halo_per_tile, n_halo - 1), 0)),
        pl.BlockSpec(bias.shape, lambda b, j: (0, 0, 0, 0)),
    ]
    args = [arr, arr, arr, bias]
    if sink is not None:
        in_specs.append(pl.BlockSpec(memory_space=pltpu.SMEM))
        args.append(sink.astype(F32))
    if with_lse:
        out_block = pl.BlockSpec((1, n_pairs, tq * d, LANES), lambda b, j: (b, 0, j, 0))
        out_shape = [jax.ShapeDtypeStruct((Bn, n_pairs, L, LANES), o_dtype),
                     jax.ShapeDtypeStruct((Bn, n_pairs, L, LANES), F32)]
        out_specs = [out_block, out_block]
    else:
        assert d == 1
        out_shape = [jax.ShapeDtypeStruct((Bn, L, qw), o_dtype)]
        out_specs = [pl.BlockSpec((1, tq, qw), lambda b, j: (b, j, 0))]
    return pl.pallas_call(
        functools.partial(_banded_kernel, n_sub=n_sub, sub_per_iter=sub_per_iter, half=half, kv_cols=kv_cols,
                          n_pairs=n_pairs, shared_kv=shared_kv, has_sink=sink is not None, with_lse=with_lse,
                          dilation=d),
        out_shape=tuple(out_shape),
        grid=(Bn, nbt),
        in_specs=in_specs,
        out_specs=tuple(out_specs),
        scratch_shapes=[pltpu.VMEM((tq + 2 * half, 2 * kvw), BF16),
                        pltpu.VMEM((n_units, 2 * ATT_BLK, ATT_BLK + 2 * half), F32),
                        pltpu.VMEM((n_units, 2 * ATT_BLK, ATT_BLK + 2 * half), BF16)],
        compiler_params=_cparams(("parallel", "arbitrary")),
        name=f"banded_attn_d{d}_h{half}",
    )(*args)


NBR_ROWS_PER_STEP = 8
NBR_KEY_BLOCK_ROWS = 4
NBR_ROWS_PER_ITER = 8


def _nbr_bias(rpb):
    H = rpb.shape[0]
    c = np.arange(GRID_W)
    qstart = np.clip(c - NA_KW // 2, 0, GRID_W - NA_KW)
    kc = np.arange(GRID_W)
    valid = (kc[None, :] >= qstart[:, None]) & (kc[None, :] < qstart[:, None] + NA_KW)
    cidx = np.clip(kc[None, :] - c[:, None] + NA_KW - 1, 0, 2 * NA_KW - 2)
    rows = jnp.stack([rpb.astype(F32)[:, NA_KH - 1 - v:2 * NA_KH - 1 - v] for v in range(NA_KH)])
    col_sel = _one_hot_t(cidx, 2 * NA_KW - 1)
    tab = jnp.dot(rows.reshape(-1, 2 * NA_KW - 1), col_sel, precision=HIGHEST)
    tab = tab.reshape(NA_KH, H, NA_KH, GRID_W, GRID_W).transpose(0, 1, 3, 2, 4)
    tab = jnp.where(jnp.asarray(valid)[None, None, :, None, :], tab, NEG_INF)
    return tab.reshape(NA_KH, H // 2, 2 * GRID_W, NA_KH * GRID_W)


def _nbr_kernel(q_ref, kv0, kv1, kv2, kv3, bias_ref, o_ref, kv_s, s_s, p_s, *, rows):
    blk = NBR_KEY_BLOCK_ROWS * GRID_W
    for t, ref in enumerate((kv0, kv1, kv2, kv3)):
        kv_s[t * blk:(t + 1) * blk, :] = ref[...]
    i0 = pl.program_id(1) * NBR_ROWS_PER_STEP
    nkeys = NA_KH * GRID_W
    n_pairs = C_HEADS // 2

    def row_group(it, carry):
        offs, variants, q0s = [], [], []
        for rr in range(NBR_ROWS_PER_ITER):
            r = it * NBR_ROWS_PER_ITER + rr
            i = i0 + r
            rstart = jnp.clip(i - NA_KH // 2, 0, rows - NA_KH)
            offs.append(pl.multiple_of((rstart - i0 + NBR_KEY_BLOCK_ROWS) * GRID_W, GRID_W))
            variants.append(i - rstart)
            q0s.append(pl.multiple_of(r * GRID_W, GRID_W))
        units = [(rr, c) for rr in range(NBR_ROWS_PER_ITER) for c in range(n_pairs)]

        for u, (rr, c) in enumerate(units):
            q2 = _pair_rows(q_ref[pl.ds(q0s[rr], GRID_W), c * LANES:(c + 1) * LANES])
            k = kv_s[pl.ds(offs[rr], nkeys), c * LANES:(c + 1) * LANES]
            s_s[u] = lax.dot_general(q2, k, (((1,), (1,)), ((), ())), preferred_element_type=F32)

        denoms = []
        for u, (rr, c) in enumerate(units):
            s = s_s[u] + bias_ref[variants[rr], c]
            m = jnp.max(s, axis=-1, keepdims=True)
            p = jnp.exp(s - m)
            denoms.append(jnp.sum(p, axis=-1, keepdims=True))
            p_s[u] = p.astype(BF16)

        for rr in range(NBR_ROWS_PER_ITER):
            o_cols = []
            for c in range(n_pairs):
                u = rr * n_pairs + c
                v = kv_s[pl.ds(offs[rr], nkeys), C_W + c * LANES:C_W + (c + 1) * LANES]
                o2 = jnp.dot(p_s[u], v, preferred_element_type=F32) / denoms[u]
                o_cols.append(_unpair_rows(o2))
            o_ref[pl.ds(q0s[rr], GRID_W), :] = jnp.concatenate(o_cols, axis=-1).astype(o_ref.dtype)
        return carry

    lax.fori_loop(0, NBR_ROWS_PER_STEP // NBR_ROWS_PER_ITER, row_group, 0)


def _neighborhood_attention(arr, bias, Bn, L):
    rows = L // GRID_W
    assert rows % NBR_ROWS_PER_STEP == 0 and rows >= 2 * NBR_ROWS_PER_STEP
    tq = NBR_ROWS_PER_STEP * GRID_W
    kb = NBR_KEY_BLOCK_ROWS * GRID_W
    nkb = L // kb
    per_step = NBR_ROWS_PER_STEP // NBR_KEY_BLOCK_ROWS
    nq = L // tq

    def key_spec(t):
        return pl.BlockSpec((kb, 2 * C_W), lambda b, j: (b * nkb + jnp.clip(j * per_step - 1 + t, 0, nkb - 1), 0))

    in_specs = ([pl.BlockSpec((tq, C_W), lambda b, j: (b * nq + j, 2))]
                + [key_spec(t) for t in range(4)]
                + [pl.BlockSpec(bias.shape, lambda b, j: (0, 0, 0, 0))])
    return pl.pallas_call(
        functools.partial(_nbr_kernel, rows=rows),
        out_shape=jax.ShapeDtypeStruct((Bn * L, C_W), BF16),
        grid=(Bn, nq),
        in_specs=in_specs,
        out_specs=pl.BlockSpec((tq, C_W), lambda b, j: (b * nq + j, 0)),
        scratch_shapes=[pltpu.VMEM((4 * kb, 2 * C_W), BF16),
                        pltpu.VMEM((NBR_ROWS_PER_ITER * C_HEADS // 2, 2 * GRID_W, NA_KH * GRID_W), F32),
                        pltpu.VMEM((NBR_ROWS_PER_ITER * C_HEADS // 2, 2 * GRID_W, NA_KH * GRID_W), BF16)],
        compiler_params=_cparams(("parallel", "arbitrary")),
        name="nbr_attn",
    )(*([arr] * 5), bias)


def _post_kernel(h_ref, hb_ref, oa_ref, ob0_ref, ob1_ref, ob2_ref, l0_ref, l1_ref, l2_ref, oc_ref,
                 wg_ref, bg_ref, wa_ref, wb_ref, wc_ref, wo_ref, g_ref, b_ref, wr_ref, br_ref,
                 o_ref, obf_ref, logit_ref, *, alpha):
    D = h_ref.shape[1]
    hb = hb_ref[...]
    def planes(ref):
        return jnp.concatenate([ref[0, c] for c in range(ref.shape[1])], axis=-1).astype(F32)

    l0, l1, l2 = planes(l0_ref), planes(l1_ref), planes(l2_ref)
    lm = jnp.maximum(jnp.maximum(l0, l1), l2)
    e0, e1, e2 = jnp.exp(l0 - lm), jnp.exp(l1 - lm), jnp.exp(l2 - lm)
    inv = 1.0 / (e0 + e1 + e2)
    o_b = ((e0 * planes(ob0_ref) + e1 * planes(ob1_ref) + e2 * planes(ob2_ref)) * inv).astype(BF16)
    merged = None
    for idx, (o_br, w_ref) in enumerate(((oa_ref[...], wa_ref), (o_b, wb_ref), (oc_ref[...], wc_ref))):
        z = (jnp.dot(hb, wg_ref[:, idx * D:(idx + 1) * D], preferred_element_type=F32)
             + bg_ref[:, idx * D:(idx + 1) * D])
        gate = 0.5 * jnp.tanh(0.5 * z) + 0.5
        term = gate * jnp.dot(o_br, w_ref[...], preferred_element_type=F32)
        merged = term if merged is None else merged + term
    y = jnp.dot(merged.astype(BF16), wo_ref[...], preferred_element_type=F32)
    h1 = _ln_rows(alpha * h_ref[...] + y, g_ref[...], b_ref[...])
    h1b = h1.astype(BF16)
    o_ref[...] = h1
    obf_ref[...] = h1b
    logit_ref[...] = jnp.dot(h1b, wr_ref[...], preferred_element_type=F32) + br_ref[...]


def _post_attention(h, hb, o_a, o_bs, lses, o_c, wg, bg, wa, wb, wc, wo, g, b, wr, br, alpha, tm=256):
    T, D = h.shape
    n_planes, L = o_bs[0].shape[1:3]
    nt = L // tm
    assert L % tm == 0

    def rows(w):
        return pl.BlockSpec((tm, w), lambda i: (i, 0))

    plane_rows = pl.BlockSpec((1, n_planes, tm, LANES), lambda i: (i // nt, 0, i % nt, 0))

    def full(a):
        return pl.BlockSpec(a.shape, lambda i: (0, 0))

    weights = [wg, bg, wa, wb, wc, wo, g, b, wr, br]
    return pl.pallas_call(
        functools.partial(_post_kernel, alpha=alpha),
        out_shape=(jax.ShapeDtypeStruct((T, D), F32), jax.ShapeDtypeStruct((T, D), BF16),
                   jax.ShapeDtypeStruct((T, LANES), F32)),
        grid=(T // tm,),
        in_specs=[rows(D), rows(D), rows(A_Q)] + [plane_rows] * 6 + [rows(C_W)] + [full(w) for w in weights],
        out_specs=(rows(D), rows(D), rows(LANES)),
        compiler_params=_cparams(("parallel",)),
        name="post_attn",
    )(h, hb, o_a, *o_bs, *lses, o_c, *weights)


ROUTE_EID, ROUTE_GATE, ROUTE_RANK = 0, 2, 4


def _route_kernel(logit_ref, meta_ref, count_ref, run_ref):
    tm = logit_ref.shape[0]

    @pl.when(pl.program_id(0) == 0)
    def _():
        run_ref[...] = jnp.zeros_like(run_ref)

    x = logit_ref[...]
    lane = lax.broadcasted_iota(jnp.int32, x.shape, 1).astype(F32)
    big = float(LANES)

    def lane_max(v):
        return jnp.max(v, axis=-1, keepdims=True)

    def first_lane(mask):
        return jnp.min(jnp.where(mask, lane, big), axis=-1, keepdims=True)

    is_g = lane < N_GROUPS
    gl = jnp.where(is_g, x, NEG_INF)
    gmax = lane_max(gl)
    g_idx = first_lane(jnp.logical_and(is_g, gl == gmax))
    g_w = 1.0 / jnp.sum(jnp.where(is_g, jnp.exp(gl - gmax), 0.0), axis=-1, keepdims=True)

    e_lo = N_GROUPS + g_idx * EXPERTS_PER_GROUP
    in_grp = jnp.logical_and(lane >= e_lo, lane < e_lo + EXPERTS_PER_GROUP)
    el = jnp.where(in_grp, x, NEG_INF)
    top1 = lane_max(el)
    lane1 = first_lane(jnp.logical_and(in_grp, el == top1))
    el2 = jnp.where(lane == lane1, NEG_INF, el)
    top2 = lane_max(el2)
    lane2 = first_lane(jnp.logical_and(in_grp, jnp.logical_and(lane != lane1, el2 == top2)))
    e2 = jnp.exp(top2 - top1)
    w1 = 1.0 / (1.0 + e2)
    w2 = e2 / (1.0 + e2)

    eid1 = lane1 - N_GROUPS
    eid2 = lane2 - N_GROUPS
    onehot = jnp.logical_or(lane == eid1, lane == eid2)
    oh = jnp.where(onehot, 1.0, 0.0).astype(BF16)
    r_i = lax.broadcasted_iota(jnp.int32, (tm, tm), 0)
    c_i = lax.broadcasted_iota(jnp.int32, (tm, tm), 1)
    strict_lower = jnp.where(c_i < r_i, 1.0, 0.0).astype(BF16)
    before = jnp.dot(strict_lower, oh, preferred_element_type=F32) + run_ref[...]
    rank1 = jnp.sum(jnp.where(lane == eid1, before, 0.0), axis=-1, keepdims=True)
    rank2 = jnp.sum(jnp.where(lane == eid2, before, 0.0), axis=-1, keepdims=True)
    run_ref[...] = run_ref[...] + jnp.sum(oh.astype(F32), axis=0, keepdims=True)

    meta = jnp.zeros(x.shape, F32)
    for k, val in ((ROUTE_EID, eid1), (ROUTE_EID + 1, eid2),
                   (ROUTE_GATE, g_w * w1), (ROUTE_GATE + 1, g_w * w2),
                   (ROUTE_RANK, rank1), (ROUTE_RANK + 1, rank2)):
        meta = jnp.where(lane == k, val, meta)
    meta_ref[...] = meta
    count_ref[...] = run_ref[...]


def _route(logits, tm=512):
    T = logits.shape[0]
    return pl.pallas_call(
        _route_kernel,
        out_shape=(jax.ShapeDtypeStruct((T, LANES), F32), jax.ShapeDtypeStruct((1, LANES), F32)),
        grid=(T // tm,),
        in_specs=[pl.BlockSpec((tm, LANES), lambda i: (i, 0))],
        out_specs=(pl.BlockSpec((tm, LANES), lambda i: (i, 0)), pl.BlockSpec((1, LANES), lambda i: (0, 0))),
        scratch_shapes=[pltpu.VMEM((1, LANES), F32)],
        compiler_params=_cparams(("arbitrary",)),
        name="route",
    )(logits)


def _expert_kernel(blk_e_ref, n_used_ref, x_ref, wg_ref, wu_ref, wd_ref, *rest, blk0):
    o_ref = rest[-1]
    b = pl.program_id(0) + blk0

    @pl.when(b < n_used_ref[0])
    def _():
        x = x_ref[...]
        a = jnp.dot(x, wg_ref[0], preferred_element_type=F32)
        u = jnp.dot(x, wu_ref[0], preferred_element_type=F32)
        hid = (a * jax.nn.sigmoid(a) * u).astype(BF16)
        o_ref[...] = jnp.dot(hid, wd_ref[0], preferred_element_type=F32).astype(o_ref.dtype)

    @pl.when(b >= n_used_ref[0])
    def _():
        o_ref[...] = jnp.zeros_like(o_ref)


def _experts(blk_e, n_used, xs, wg, wu, wd, y_prev, blk0, nblk):
    rows, D = xs.shape
    de = wg.shape[2]
    in_specs = [pl.BlockSpec((MOE_ROWS, D), lambda b, be, nu: (b, 0)),
                pl.BlockSpec((1, D, de), lambda b, be, nu: (be[b + blk0], 0, 0)),
                pl.BlockSpec((1, D, de), lambda b, be, nu: (be[b + blk0], 0, 0)),
                pl.BlockSpec((1, de, D), lambda b, be, nu: (be[b + blk0], 0, 0))]
    args = [blk_e, n_used, xs, wg, wu, wd]
    aliases = {}
    if y_prev is not None:
        in_specs.append(pl.BlockSpec(memory_space=pl.ANY))
        args.append(y_prev)
        aliases = {len(args) - 1: 0}
    grid_spec = pltpu.PrefetchScalarGridSpec(
        num_scalar_prefetch=2,
        grid=(rows // MOE_ROWS,),
        in_specs=in_specs,
        out_specs=pl.BlockSpec((MOE_ROWS, D), lambda b, be, nu: (b + blk0, 0)),
    )
    return pl.pallas_call(
        functools.partial(_expert_kernel, blk0=blk0),
        out_shape=jax.ShapeDtypeStruct((nblk * MOE_ROWS, D), BF16),
        grid_spec=grid_spec,
        input_output_aliases=aliases,
        compiler_params=_cparams(("arbitrary",)),
        name="experts",
    )(*args)


def _combine_kernel(h_ref, ya_ref, yb_ref, meta_ref, g_ref, b_ref, *rest, alpha):
    o_ref, ob_ref = rest[-2:]
    meta = meta_ref[...]
    g1 = meta[:, ROUTE_GATE:ROUTE_GATE + 1]
    g2 = meta[:, ROUTE_GATE + 1:ROUTE_GATE + 2]
    y = ya_ref[...].astype(F32) * g1 + yb_ref[...].astype(F32) * g2
    h2 = _ln_rows(alpha * h_ref[...] + y, g_ref[...], b_ref[...])
    o_ref[...] = h2
    ob_ref[...] = h2.astype(BF16)


def _combine(h, ya, yb, meta, g, b, alpha, prev, row0, tm=512):
    T, D = h.shape
    t0 = row0 // tm
    assert row0 % tm == 0 and ya.shape[0] % tm == 0
    chunk_rows = pl.BlockSpec((tm, D), lambda i: (i, 0))
    full_rows = pl.BlockSpec((tm, D), lambda i: (i + t0, 0))
    vec = pl.BlockSpec((1, D), lambda i: (0, 0))
    in_specs = [full_rows, chunk_rows, chunk_rows, pl.BlockSpec((tm, LANES), lambda i: (i + t0, 0)), vec, vec]
    args = [h, ya, yb, meta, g.reshape(1, D), b.reshape(1, D)]
    aliases = {}
    if prev is not None:
        in_specs += [pl.BlockSpec(memory_space=pl.ANY)] * 2
        aliases = {len(args): 0, len(args) + 1: 1}
        args += list(prev)
    return pl.pallas_call(
        functools.partial(_combine_kernel, alpha=alpha),
        out_shape=(jax.ShapeDtypeStruct((T, D), F32), jax.ShapeDtypeStruct((T, D), BF16)),
        grid=(ya.shape[0] // tm,),
        in_specs=in_specs,
        out_specs=(full_rows, full_rows),
        input_output_aliases=aliases,
        compiler_params=_cparams(("parallel",)),
        name="combine_ln2",
    )(*args)


def _moe(h1, h1b, logits, w_eg, w_eu, w_ed, e_off, g, b, alpha):
    T, D = h1.shape
    meta, counts = _route(logits)
    eid = meta[:, ROUTE_EID:ROUTE_EID + 2].astype(jnp.int32)
    rank = meta[:, ROUTE_RANK:ROUTE_RANK + 2].astype(jnp.int32)
    cnt = counts[0, :N_EXPERTS].astype(jnp.int32)
    padded = (cnt + MOE_ROWS - 1) // MOE_ROWS * MOE_ROWS
    pad_end = jnp.cumsum(padded)
    pad_off = pad_end - padded
    dest = pad_off[eid] + rank
    nblk = -(-(2 * T) // MOE_ROWS) + N_EXPERTS
    P = nblk * MOE_ROWS
    tok = jnp.broadcast_to(jnp.arange(T, dtype=jnp.int32)[:, None], (T, 2))
    buf_tok = jnp.zeros((P,), jnp.int32).at[dest.reshape(-1)].set(tok.reshape(-1), unique_indices=True)
    blk_start = jnp.arange(nblk, dtype=jnp.int32) * MOE_ROWS
    blk_e = jnp.minimum(jnp.sum((pad_end[None, :] <= blk_start[:, None]).astype(jnp.int32), axis=1), N_EXPERTS - 1)
    blk_e = blk_e + e_off
    n_used = (pad_end[-1:] // MOE_ROWS).astype(jnp.int32)
    assert nblk % MOE_CHUNKS == 0 and T % MOE_CHUNKS == 0
    cb = nblk // MOE_CHUNKS
    y = None
    for c in range(MOE_CHUNKS):
        xs = h1b.at[buf_tok[c * cb * MOE_ROWS:(c + 1) * cb * MOE_ROWS]].get(mode='promise_in_bounds')
        y = _experts(blk_e, n_used, xs, w_eg, w_eu, w_ed, y, c * cb, nblk)
    ct = T // MOE_CHUNKS
    out = None
    for c in range(MOE_CHUNKS):
        ya = y.at[dest[c * ct:(c + 1) * ct, 0]].get(mode='promise_in_bounds')
        yb = y.at[dest[c * ct:(c + 1) * ct, 1]].get(mode='promise_in_bounds')
        out = _combine(h1, ya, yb, meta, g, b, alpha, out, c * ct)
    return out


def kernel(x, ln0_g, ln0_b, rel_bias, w_in, b_gate, sink_a, rpb_c, w_br_a, w_br_b, w_br_c, w_out,
           ln1_g, ln1_b, w_rg, b_rg, w_re, b_re, w_eg, w_eu, w_ed, ln2_g, ln2_b):
    Bn, L, D = x.shape
    depth = w_in.shape[0]
    T = Bn * L
    alpha = float((2 * depth) ** 0.25)
    a_order = np.asarray(A_HEAD_ORDER)

    a_pairs = [(c, c + A_HEADS // 2) for c in range(A_HEADS // 2)]
    b_pairs = [(2 * c, 2 * c + 1) for c in range(B_HEADS_PER_GROUP // 2)]
    bias_a = _band_bias(rel_bias[:, :A_HEADS], a_pairs, A_HALF_WINDOW, 1)
    bias_b = []
    for g, (window, dilation) in enumerate(B_CONFIGS):
        c0 = A_HEADS + g * B_HEADS_PER_GROUP
        bias_b.append(_band_bias(rel_bias[:, c0:c0 + B_HEADS_PER_GROUP], b_pairs, window // (2 * dilation), dilation))

    w_eg_b = w_eg.astype(BF16).reshape((depth * N_EXPERTS,) + w_eg.shape[2:])
    w_eu_b = w_eu.astype(BF16).reshape((depth * N_EXPERTS,) + w_eu.shape[2:])
    w_ed_b = w_ed.astype(BF16).reshape((depth * N_EXPERTS,) + w_ed.shape[2:])

    h, hb = _layer_norm(x.reshape(T, D), ln0_g, ln0_b)
    for l in range(depth):
        w_gate = w_in[l, :, QKV_COLS:].astype(BF16)
        arr_a, arr_b0, arr_b1, arr_b2, arr_c = _in_projection(hb, _inproj_weight(w_in[l]), Bn, L)

        (o_a,) = _banded_attention(arr_a, bias_a, sink_a[l][a_order], tq=512, n_pairs=A_HEADS // 2, shared_kv=True,
                                   half=A_HALF_WINDOW, with_lse=False)
        o_bs, lses = [], []
        for g, (arr, tq) in enumerate(((arr_b0, 512), (arr_b1, 512), (arr_b2, 256))):
            window, dilation = B_CONFIGS[g]
            o, lse = _banded_attention(arr, bias_b[g], None, tq=tq, n_pairs=B_HEADS_PER_GROUP // 2, shared_kv=False,
                                       half=window // (2 * dilation), with_lse=True)
            o_bs.append(o)
            lses.append(lse)
        o_c = _neighborhood_attention(arr_c, _nbr_bias(rpb_c[l]), Bn, L)

        w_r = jnp.concatenate([w_rg[l], jnp.transpose(w_re[l], (1, 0, 2)).reshape(D, N_EXPERTS)], axis=1)
        w_r = jnp.pad(w_r, ((0, 0), (0, LANES - w_r.shape[1]))).astype(BF16)
        b_r = jnp.concatenate([b_rg[l], b_re[l].reshape(-1)]).astype(F32)
        b_r = jnp.pad(b_r, (0, LANES - b_r.shape[0])).reshape(1, LANES)
        w_a = w_br_a[l].reshape(A_HEADS, HEAD_DIM, D)[a_order].reshape(A_Q, D)

        h1, h1b, logits = _post_attention(
            h, hb, o_a.reshape(T, A_Q), o_bs, lses, o_c,
            w_gate, b_gate[l].reshape(1, N_BRANCH * D).astype(F32),
            w_a.astype(BF16), w_br_b[l].astype(BF16), w_br_c[l].astype(BF16), w_out[l].astype(BF16),
            ln1_g[l].reshape(1, D), ln1_b[l].reshape(1, D), w_r, b_r, alpha)

        h, hb = _moe(h1, h1b, logits, w_eg_b, w_eu_b, w_ed_b, l * N_EXPERTS, ln2_g[l], ln2_b[l], alpha)
    return h.reshape(Bn, L, D)
```
